```python
import math
import jax
import jax.numpy as jnp
from jax import lax
import numpy as np

D_MODEL = 1024
BATCH = 2
SEQ = 8192
DEPTH = 2
DEC_BATCH = 32
DEC_SEQ = 8
PAST_LEN = 16384
PAGE_SIZE = 128

N_A_LAYERS = DEPTH // 2
N_B_LAYERS = DEPTH - N_A_LAYERS
A_HEADS = 8
A_DK = 128
A_DV = 128
A_QK_W = A_HEADS * A_DK
A_V_W = A_HEADS * A_DV
A_CONV_DIM = 2 * A_QK_W + A_V_W
A_IN_W = A_CONV_DIM + A_V_W + 2 * A_HEADS
CONV_W = 4
DELTA_CHUNK = 64
B_HEADS = 16
B_HD = 64
B_KV_HEADS = 4
B_GROUP = B_HEADS // B_KV_HEADS
MOBA_BLOCK = 256
MOBA_TOPK = 3
Q_BLOCK = 128
ROPE_THETA = 10000.0
D_FF = ((8 * D_MODEL + 3 * 256 - 1) // (3 * 256)) * 256
DEEPNORM_ALPHA = (2.0 * DEPTH) ** 0.25
DEEPNORM_BETA = (8.0 * DEPTH) ** -0.25
LN_EPS = 1e-5
RMS_EPS = 1e-6
L2_EPS = 1e-6
F32 = jnp.float32

kernel_name = 'yoco_gdn_moba_deepnorm_step'


def layer_norm(x, g, b):
    xf = x.astype(F32)
    mu = jnp.mean(xf, -1, keepdims=True)
    var = jnp.mean(jnp.square(xf - mu), -1, keepdims=True)
    return ((xf - mu) * lax.rsqrt(var + LN_EPS) * g.astype(F32) + b.astype(F32)).astype(x.dtype)


def swiglu(x, w_gate, w_up, w_down):
    return (jax.nn.silu(x @ w_gate) * (x @ w_up)) @ w_down


def rope(x, pos):
    half = x.shape[-1] // 2
    inv_freq = ROPE_THETA ** (-jnp.arange(half, dtype=F32) / half)
    ang = pos.astype(F32)[:, None] * inv_freq[None, :]
    cos = jnp.cos(ang)[None, :, None, :]
    sin = jnp.sin(ang)[None, :, None, :]
    x1 = x[..., :half].astype(F32)
    x2 = x[..., half:].astype(F32)
    return jnp.concatenate([x1 * cos - x2 * sin, x2 * cos + x1 * sin], -1).astype(x.dtype)


def l2norm(x):
    return x * lax.rsqrt(jnp.sum(jnp.square(x), -1, keepdims=True) + L2_EPS)


def short_conv(x, buf, w):
    t = x.shape[1]
    xp = jnp.concatenate([buf.astype(x.dtype), x], axis=1)
    y = xp[:, 0:t] * w[0]
    for i in range(1, CONV_W):
        y = y + xp[:, i:i + t] * w[i]
    return jax.nn.silu(y), xp[:, xp.shape[1] - (CONV_W - 1):]


def gated_delta_chunked(q, k, v, g, beta, s0):
    b, t, h, _ = q.shape
    c = DELTA_CHUNK
    n = t // c

    def blk(a):
        a = a.reshape((b, n, c) + a.shape[2:])
        return jnp.swapaxes(a, 2, 3)

    q, k, v, g, beta = blk(q), blk(k), blk(v), blk(g), blk(beta)
    G = jnp.cumsum(g, axis=-1)
    causal = jnp.tril(jnp.ones((c, c), bool))
    strict = jnp.tril(jnp.ones((c, c), bool), -1)
    diff = G[..., :, None] - G[..., None, :]
    decay = jnp.where(causal, jnp.exp(jnp.where(causal, diff, 0.0)), 0.0)
    a_mat = jnp.where(strict, beta[..., :, None] * jnp.einsum('bnhik,bnhjk->bnhij', k, k) * decay, 0.0)
    m = a_mat + jnp.eye(c, dtype=F32)
    rhs = jnp.concatenate([v * beta[..., None], k * (beta * jnp.exp(G))[..., None]], -1)
    sol = lax.linalg.triangular_solve(m, rhs, left_side=True, lower=True, unit_diagonal=True)
    u, w = sol[..., :A_DV], sol[..., A_DV:]
    qk = jnp.where(causal, jnp.einsum('bnhik,bnhjk->bnhij', q, k) * decay, 0.0)
    q_dec = q * jnp.exp(G)[..., None]
    k_dec = k * jnp.exp(G[..., -1:] - G)[..., None]
    g_last = jnp.exp(G[..., -1])

    def step(s, xs):
        u_c, w_c, qk_c, qd_c, kd_c, gl_c = xs
        v_new = u_c - jnp.einsum('bhck,bhkv->bhcv', w_c, s)
        o = jnp.einsum('bhck,bhkv->bhcv', qd_c, s) + jnp.einsum('bhij,bhjv->bhiv', qk_c, v_new)
        s = s * gl_c[..., None, None] + jnp.einsum('bhck,bhcv->bhkv', kd_c, v_new)
        return s, o

    xs = tuple(jnp.moveaxis(a, 1, 0) for a in (u, w, qk, q_dec, k_dec, g_last))
    s, o = lax.scan(step, s0, xs)
    o = jnp.transpose(o, (1, 0, 3, 2, 4)).reshape(b, t, h, A_DV)
    return o, s


def gated_delta_recurrent(q, k, v, g, beta, s0):
    def step(s, xs):
        q_t, k_t, v_t, g_t, b_t = xs
        s = s * jnp.exp(g_t)[..., None, None]
        err = v_t - jnp.einsum('bhk,bhkv->bhv', k_t, s)
        s = s + jnp.einsum('bhk,bhv->bhkv', k_t, b_t[..., None] * err)
        return s, jnp.einsum('bhk,bhkv->bhv', q_t, s)

    xs = tuple(jnp.moveaxis(a, 1, 0) for a in (q, k, v, g, beta))
    s, o = lax.scan(step, s0, xs)
    return jnp.moveaxis(o, 0, 1), s


def gated_delta_mixer(x, conv_buf, s0, chunked, w_in, conv_w, a_log, dt_bias, norm_w, w_out):
    b, t, _ = x.shape
    proj = x @ w_in
    qkv = proj[..., :A_CONV_DIM]
    z = proj[..., A_CONV_DIM:A_CONV_DIM + A_V_W]
    a = proj[..., A_CONV_DIM + A_V_W:A_CONV_DIM + A_V_W + A_HEADS]
    bb = proj[..., A_CONV_DIM + A_V_W + A_HEADS:]
    qkv, new_buf = short_conv(qkv, conv_buf, conv_w)
    qkv = qkv.astype(F32)
    q = l2norm(qkv[..., :A_QK_W].reshape(b, t, A_HEADS, A_DK)) * (A_DK ** -0.5)
    k = l2norm(qkv[..., A_QK_W:2 * A_QK_W].reshape(b, t, A_HEADS, A_DK))
    v = qkv[..., 2 * A_QK_W:].reshape(b, t, A_HEADS, A_DV)
    g = -jnp.exp(a_log.astype(F32)) * jax.nn.softplus(a.astype(F32) + dt_bias.astype(F32))
    beta = jax.nn.sigmoid(bb.astype(F32))
    if chunked:
        o, s = gated_delta_chunked(q, k, v, g, beta, s0.astype(F32))
    else:
        o, s = gated_delta_recurrent(q, k, v, g, beta, s0.astype(F32))
    o = o * lax.rsqrt(jnp.mean(jnp.square(o), -1, keepdims=True) + RMS_EPS) * norm_w.astype(F32)
    o = o * jax.nn.silu(z.astype(F32).reshape(b, t, A_HEADS, A_DV))
    out = o.reshape(b, t, A_V_W).astype(x.dtype) @ w_out
    return out, new_buf, s.astype(x.dtype)


def to_blocks(k, v):
    b, t = k.shape[:2]
    nb = -(-t // MOBA_BLOCK)
    pad = nb * MOBA_BLOCK - t
    k = jnp.pad(k, ((0, 0), (0, pad), (0, 0), (0, 0)))
    v = jnp.pad(v, ((0, 0), (0, pad), (0, 0), (0, 0)))
    kb = k.reshape(b, nb, MOBA_BLOCK, B_KV_HEADS, B_HD)
    vb = v.reshape(b, nb, MOBA_BLOCK, B_KV_HEADS, B_HD)
    means = jnp.mean(kb.astype(F32), axis=2)
    return kb, vb, means


def moba_attend(q, q_pos, kb, vb, means):
    b, nq, h, _ = q.shape
    nb = kb.shape[1]
    qf = q.astype(F32)
    own = q_pos // MOBA_BLOCK
    gate = jnp.einsum('bqkgd,bnkd->bqkgn', qf.reshape(b, nq, B_KV_HEADS, B_GROUP, B_HD), means)
    gate = gate.reshape(b, nq, h, nb)
    past = jnp.arange(nb)[None, :] < own[:, None]
    gate = jnp.where(past[None, :, None, :], gate, -jnp.inf)
    n_sel = min(MOBA_TOPK, nb)
    _, top = lax.top_k(gate, n_sel)
    sel_valid = jnp.arange(n_sel)[None, :] < own[:, None]
    idx = jnp.concatenate([top, jnp.broadcast_to(own[None, :, None, None], (b, nq, h, 1)).astype(top.dtype)], -1)
    valid = jnp.concatenate([jnp.broadcast_to(sel_valid[None, :, None, :], (1, nq, 1, n_sel)),
                             jnp.ones((1, nq, 1, 1), bool)], -1)
    b_ix = jnp.arange(b)[:, None, None, None]
    h_ix = (jnp.arange(h) // B_GROUP)[None, None, :, None]
    kg = kb[b_ix, idx, :, h_ix, :].astype(F32)
    vg = vb[b_ix, idx, :, h_ix, :].astype(F32)
    key_pos = idx[..., None] * MOBA_BLOCK + jnp.arange(MOBA_BLOCK)
    mask = valid[..., None] & (key_pos <= q_pos[None, :, None, None, None])
    logits = jnp.einsum('bqhd,bqhrkd->bqhrk', qf, kg) * (B_HD ** -0.5)
    logits = jnp.where(mask, logits, -jnp.inf)
    p = jax.nn.softmax(logits.reshape(b, nq, h, -1), axis=-1).reshape(logits.shape)
    return jnp.einsum('bqhrk,bqhrkd->bqhd', p, vg).astype(q.dtype)


def moba_prompt(q, k, v):
    b, t = q.shape[:2]
    kb, vb, means = to_blocks(k, v)
    nqb = t // Q_BLOCK
    qc = jnp.swapaxes(q.reshape(b, nqb, Q_BLOCK, B_HEADS, B_HD), 0, 1)
    starts = jnp.arange(nqb, dtype=jnp.int32) * Q_BLOCK

    def one_block(args):
        q_i, s = args
        return moba_attend(q_i, s + jnp.arange(Q_BLOCK, dtype=jnp.int32), kb, vb, means)

    o = lax.map(one_block, (qc, starts))
    return jnp.swapaxes(o, 0, 1).reshape(b, t, B_HEADS, B_HD)


def moba_sample(q, k_new, v_new, cache_k, cache_v, page_table):
    db, t = q.shape[:2]
    past = page_table.shape[1] * PAGE_SIZE
    k_past = cache_k[page_table].reshape(db, past, B_KV_HEADS, B_HD).astype(k_new.dtype)
    v_past = cache_v[page_table].reshape(db, past, B_KV_HEADS, B_HD).astype(v_new.dtype)
    kb, vb, means = to_blocks(jnp.concatenate([k_past, k_new], 1), jnp.concatenate([v_past, v_new], 1))
    pos = past + jnp.arange(t, dtype=jnp.int32)
    return moba_attend(q, pos, kb, vb, means)


def trunk(x, pos, conv_state, delta_state, chunked, attend, ln_g, ln_b, a_w_in, a_conv, a_log_decay,
          a_dt_bias, a_norm, a_w_out, kv_w_k, kv_w_v, b_w_q, b_w_o, ffn_w_gate, ffn_w_up, ffn_w_down):
    b, t, _ = x.shape
    new_conv, new_delta = [], []
    k_sh = None
    v_sh = None
    for layer in range(DEPTH):
        if layer < N_A_LAYERS:
            mix, cb, s = gated_delta_mixer(x, conv_state[layer], delta_state[layer], chunked,
                                           a_w_in[layer], a_conv[layer], a_log_decay[layer],
                                           a_dt_bias[layer], a_norm[layer], a_w_out[layer])
            new_conv.append(cb)
            new_delta.append(s)
        else:
            if layer == N_A_LAYERS:
                k_sh = rope((x @ kv_w_k).reshape(b, t, B_KV_HEADS, B_HD), pos)
                v_sh = (x @ kv_w_v).reshape(b, t, B_KV_HEADS, B_HD)
            j = layer - N_A_LAYERS
            q = rope((x @ b_w_q[j]).reshape(b, t, B_HEADS, B_HD), pos)
            mix = attend(q, k_sh, v_sh).reshape(b, t, B_HEADS * B_HD) @ b_w_o[j]
        h = layer_norm(DEEPNORM_ALPHA * x + mix, ln_g[layer, 0], ln_b[layer, 0])
        x = layer_norm(DEEPNORM_ALPHA * h + swiglu(h, ffn_w_gate[layer], ffn_w_up[layer], ffn_w_down[layer]),
                       ln_g[layer, 1], ln_b[layer, 1])
    return x, jnp.stack(new_conv), jnp.stack(new_delta), k_sh, v_sh


def setup_inputs(seed: int = 0) -> dict:
    key = jax.random.key(seed)
    ks = jax.random.split(key, 24)
    n_pages = PAST_LEN // PAGE_SIZE
    n_pool = (DEC_BATCH * n_pages * 5) // 4

    def nrm(k, shape, scale):
        return jax.random.normal(k, shape, F32) * scale

    x_prompt = nrm(ks[0], (BATCH, SEQ, D_MODEL), 1.0)
    x_sample = nrm(ks[1], (DEC_BATCH, DEC_SEQ, D_MODEL), 1.0)
    state_conv = nrm(ks[2], (N_A_LAYERS, DEC_BATCH, CONV_W - 1, A_CONV_DIM), 1.0)
    state_delta = nrm(ks[3], (N_A_LAYERS, DEC_BATCH, A_HEADS, A_DK, A_DV), 0.1)
    cache_k = nrm(ks[4], (n_pool, PAGE_SIZE, B_KV_HEADS, B_HD), 1.0)
    cache_v = nrm(ks[5], (n_pool, PAGE_SIZE, B_KV_HEADS, B_HD), 1.0)
    page_table = jax.random.permutation(ks[6], n_pool)[:DEC_BATCH * n_pages].reshape(DEC_BATCH, n_pages).astype(jnp.int32)
    ln_g = 1.0 + nrm(ks[7], (DEPTH, 2, D_MODEL), 0.02)
    ln_b = nrm(ks[8], (DEPTH, 2, D_MODEL), 0.02)
    a_w_in = nrm(ks[9], (N_A_LAYERS, D_MODEL, A_IN_W), D_MODEL ** -0.5)
    a_conv = nrm(ks[10], (N_A_LAYERS, CONV_W, A_CONV_DIM), CONV_W ** -0.5)
    a_log_decay = jnp.log(jax.random.uniform(ks[11], (N_A_LAYERS, A_HEADS), F32, 1.0, 16.0))
    dt = jnp.exp(jax.random.uniform(ks[12], (N_A_LAYERS, A_HEADS), F32, math.log(1e-3), math.log(1e-1)))
    a_dt_bias = dt + jnp.log(-jnp.expm1(-dt))
    a_norm = 1.0 + nrm(ks[13], (N_A_LAYERS, A_DV), 0.02)
    a_w_out = nrm(ks[14], (N_A_LAYERS, A_V_W, D_MODEL), (A_V_W ** -0.5) * DEEPNORM_BETA)
    kv_w_k = nrm(ks[15], (D_MODEL, B_KV_HEADS * B_HD), D_MODEL ** -0.5)
    kv_w_v = nrm(ks[16], (D_MODEL, B_KV_HEADS * B_HD), (D_MODEL ** -0.5) * DEEPNORM_BETA)
    b_w_q = nrm(ks[17], (N_B_LAYERS, D_MODEL, B_HEADS * B_HD), D_MODEL ** -0.5)
    b_w_o = nrm(ks[18], (N_B_LAYERS, B_HEADS * B_HD, D_MODEL), ((B_HEADS * B_HD) ** -0.5) * DEEPNORM_BETA)
    ffn_w_gate = nrm(ks[19], (DEPTH, D_MODEL, D_FF), D_MODEL ** -0.5)
    ffn_w_up = nrm(ks[20], (DEPTH, D_MODEL, D_FF), (D_MODEL ** -0.5) * DEEPNORM_BETA)
    ffn_w_down = nrm(ks[21], (DEPTH, D_FF, D_MODEL), (D_FF ** -0.5) * DEEPNORM_BETA)
    return {'x_prompt': x_prompt, 'x_sample': x_sample, 'state_conv': state_conv,
            'state_delta': state_delta, 'cache_k': cache_k, 'cache_v': cache_v,
            'page_table': page_table, 'ln_g': ln_g, 'ln_b': ln_b, 'a_w_in': a_w_in,
            'a_conv': a_conv, 'a_log_decay': a_log_decay, 'a_dt_bias': a_dt_bias,
            'a_norm': a_norm, 'a_w_out': a_w_out, 'kv_w_k': kv_w_k, 'kv_w_v': kv_w_v,
            'b_w_q': b_w_q, 'b_w_o': b_w_o, 'ffn_w_gate': ffn_w_gate, 'ffn_w_up': ffn_w_up,
            'ffn_w_down': ffn_w_down}


def reference(x_prompt, x_sample, state_conv, state_delta, cache_k, cache_v, page_table,
              ln_g, ln_b, a_w_in, a_conv, a_log_decay, a_dt_bias, a_norm, a_w_out,
              kv_w_k, kv_w_v, b_w_q, b_w_o, ffn_w_gate, ffn_w_up, ffn_w_down):
    weights = (ln_g, ln_b, a_w_in, a_conv, a_log_decay, a_dt_bias, a_norm, a_w_out,
               kv_w_k, kv_w_v, b_w_q, b_w_o, ffn_w_gate, ffn_w_up, ffn_w_down)
    bp, tp, _ = x_prompt.shape
    p_pos = jnp.arange(tp, dtype=jnp.int32)
    p_conv0 = jnp.zeros((N_A_LAYERS, bp, CONV_W - 1, A_CONV_DIM), x_prompt.dtype)
    p_delta0 = jnp.zeros((N_A_LAYERS, bp, A_HEADS, A_DK, A_DV), F32)
    y_prompt, p_conv, p_delta, p_k, p_v = trunk(x_prompt, p_pos, p_conv0, p_delta0, True, moba_prompt, *weights)
    past_len = page_table.shape[1] * PAGE_SIZE
    s_pos = past_len + jnp.arange(x_sample.shape[1], dtype=jnp.int32)

    def attend_sample(q, k_new, v_new):
        return moba_sample(q, k_new, v_new, cache_k, cache_v, page_table)

    y_sample, s_conv, s_delta, s_k, s_v = trunk(x_sample, s_pos, state_conv, state_delta, False, attend_sample, *weights)
    return (y_prompt, y_sample, p_conv, p_delta, p_k, p_v, s_conv, s_delta, s_k, s_v)
```

```python
import functools

import numpy as np
import jax
import jax.numpy as jnp
from jax import lax
from jax.experimental import pallas as pl
from jax.experimental.pallas import tpu as pltpu

F32 = jnp.float32
BF16 = jnp.bfloat16
HI = lax.Precision.HIGHEST

LANES = 128
SUBLANES = 8
VMEM_LIMIT_MB = 56

MOBA_BLOCK = 256
MOBA_TOPK = 3
DELTA_CHUNK = 64
ROPE_THETA = 10000.0
LN_EPS = 1e-5
RMS_EPS = 1e-6
L2_EPS = 1e-6
NEG_INF = float("-inf")


def _cparams(sem):
    return pltpu.CompilerParams(dimension_semantics=sem,
                                vmem_limit_bytes=VMEM_LIMIT_MB * 1024 * 1024)


def _dot(a, b, prec=None):
    return jnp.dot(a, b, precision=prec, preferred_element_type=F32)


def _dot_nt(a, b, prec=None):
    return lax.dot_general(a, b, (((1,), (1,)), ((), ())), precision=prec,
                           preferred_element_type=F32)


def _dot_tn(a, b, prec=None):
    return lax.dot_general(a, b, (((0,), (0,)), ((), ())), precision=prec,
                           preferred_element_type=F32)


def _sigmoid(x):
    return 1.0 / (1.0 + jnp.exp(-x))


def _layer_norm(v, g, b):
    mu = jnp.mean(v, axis=-1, keepdims=True)
    d = v - mu
    var = jnp.mean(d * d, axis=-1, keepdims=True)
    return d * lax.rsqrt(var + LN_EPS) * g + b


def _mm_kernel(x_ref, w_ref, o_ref):
    o_ref[...] = _dot(x_ref[...].astype(BF16), w_ref[...])


def _matmul(x, w, tm, tn, name):
    r, k = x.shape
    n = w.shape[1]
    return pl.pallas_call(
        _mm_kernel,
        grid=(r // tm, n // tn),
        in_specs=[pl.BlockSpec((tm, k), lambda i, j: (i, 0)),
                  pl.BlockSpec((k, tn), lambda i, j: (0, j))],
        out_specs=pl.BlockSpec((tm, tn), lambda i, j: (i, j)),
        out_shape=jax.ShapeDtypeStruct((r, n), F32),
        compiler_params=_cparams(("parallel", "parallel")),
        name=name,
    )(x, w)


def _prep_kernel(cur_ref, prev_ref, st_ref, ab_ref, cw_ref, alog_ref, dtb_ref,
                 qkv_ref, gb_ref, xp_ref, *, tt, conv_w, n_q, n_qk, nh, q_scale):
    t = pl.program_id(1)
    hist = SUBLANES

    @pl.when(t == 0)
    def _():
        xp_ref[0:hist, :] = st_ref[...]

    @pl.when(t != 0)
    def _():
        xp_ref[0:hist, :] = prev_ref[...]

    xp_ref[hist:hist + tt, :] = cur_ref[...]
    base = hist - (conv_w - 1)
    for c in range(cur_ref.shape[1] // LANES):
        cs = slice(c * LANES, (c + 1) * LANES)
        y = xp_ref[base:base + tt, cs] * cw_ref[0:1, cs]
        for i in range(1, conv_w):
            y = y + xp_ref[base + i:base + i + tt, cs] * cw_ref[i:i + 1, cs]
        y = y * _sigmoid(y)
        if c < n_qk:
            y = y * lax.rsqrt(jnp.sum(y * y, axis=-1, keepdims=True) + L2_EPS)
            if c < n_q:
                y = y * q_scale
        qkv_ref[:, cs] = y
    ab = ab_ref[...]
    lane = lax.broadcasted_iota(jnp.int32, ab.shape, 1)
    sp = ab + dtb_ref[...]
    softplus = jnp.maximum(sp, 0.0) + jnp.log1p(jnp.exp(-jnp.abs(sp)))
    g = -jnp.exp(alog_ref[...]) * softplus
    gb_ref[...] = jnp.where(lane < nh, g, _sigmoid(ab))


def _gdn_prep(proj, state8, conv_w, alog, dtb, *, nseq, t_len, tt, conv_dim, nh, dk):
    rows = nseq * t_len
    nt = t_len // tt
    cw = conv_w.shape[0]
    ab_blk = (conv_dim + nh * dk) // LANES
    kern = functools.partial(_prep_kernel, tt=tt, conv_w=cw, n_q=nh, n_qk=2 * nh, nh=nh,
                             q_scale=float(dk) ** -0.5)
    tpb = tt // SUBLANES
    return pl.pallas_call(
        kern,
        grid=(nseq, nt),
        in_specs=[
            pl.BlockSpec((tt, conv_dim), lambda s, t: (s * nt + t, 0)),
            pl.BlockSpec((SUBLANES, conv_dim),
                         lambda s, t: (jnp.maximum((s * nt + t) * tpb - 1, 0), 0)),
            pl.BlockSpec((None, SUBLANES, conv_dim), lambda s, t: (s, 0, 0)),
            pl.BlockSpec((tt, LANES), lambda s, t: (s * nt + t, ab_blk)),
            pl.BlockSpec((cw, conv_dim), lambda s, t: (0, 0)),
            pl.BlockSpec((1, LANES), lambda s, t: (0, 0)),
            pl.BlockSpec((1, LANES), lambda s, t: (0, 0)),
        ],
        out_specs=[pl.BlockSpec((tt, conv_dim), lambda s, t: (s * nt + t, 0)),
                   pl.BlockSpec((tt, LANES), lambda s, t: (s * nt + t, 0))],
        out_shape=[jax.ShapeDtypeStruct((rows, conv_dim), F32),
                   jax.ShapeDtypeStruct((rows, LANES), F32)],
        scratch_shapes=[pltpu.VMEM((tt + SUBLANES, conv_dim), F32)],
        compiler_params=_cparams(("parallel", "arbitrary")),
        name="gdn_prep",
    )(proj, proj, state8, proj, conv_w, alog, dtb)


def _delta_kernel(q_ref, k_ref, v_ref, gb_ref, s0_ref, o_ref, s_ref, st_scr, *, ct, c, nh, dk):
    t = pl.program_id(1)

    @pl.when(t == 0)
    def _():
        st_scr[...] = s0_ref[...]

    row = lax.broadcasted_iota(jnp.int32, (c, c), 0)
    col = lax.broadcasted_iota(jnp.int32, (c, c), 1)
    causal = row >= col
    strict = row > col
    tril = causal.astype(F32)
    eye = (row == col).astype(F32)
    n_sq = max(int(np.ceil(np.log2(c))) - 1, 0)

    def chunk(r0):
        rs = pl.ds(r0, c)
        gbc = gb_ref[rs, :]
        gc = _dot(tril, gbc, HI)
        gt = gc.T
        eg = jnp.exp(gc)
        for h in range(nh):
            cs = slice(h * dk, (h + 1) * dk)
            qh = q_ref[rs, cs]
            kh = k_ref[rs, cs]
            vh = v_ref[rs, cs]
            gcol = gc[:, h:h + 1]
            grow = gt[h:h + 1, :]
            bcol = gbc[:, nh + h:nh + h + 1]
            egcol = eg[:, h:h + 1]
            decay = jnp.where(causal, jnp.exp(jnp.where(causal, gcol - grow, 0.0)), 0.0)
            kk = _dot_nt(kh, kh, HI)
            qk = _dot_nt(qh, kh, HI)
            x = -jnp.where(strict, bcol * kk * decay, 0.0)
            tm = eye + x
            p = x
            for _ in range(n_sq):
                p = _dot(p, p, HI)
                tm = tm + _dot(tm, p, HI)
            rhs = jnp.concatenate([vh * bcol, kh * (bcol * egcol)], axis=-1)
            sol = _dot(tm, rhs, HI)
            u = sol[:, :dk]
            w = sol[:, dk:]
            s = st_scr[h]
            v_new = u - _dot(w, s, HI)
            o = _dot(qh * egcol, s, HI) + _dot(jnp.where(causal, qk * decay, 0.0), v_new, HI)
            glast = grow[:, c - 1:c]
            kd = kh * jnp.exp(glast - gcol)
            st_scr[h] = s * jnp.exp(glast) + _dot_tn(kd, v_new, HI)
            o_ref[rs, cs] = o

    n_chunks = ct // c
    if n_chunks == 1:
        chunk(0)
    else:
        def body(ci, carry):
            chunk(pl.multiple_of(ci * c, c))
            return carry
        lax.fori_loop(0, n_chunks, body, 0)

    @pl.when(t == pl.num_programs(1) - 1)
    def _():
        s_ref[...] = st_scr[...]


def _delta_rule(qkv, gb, s0, *, nseq, t_len, ct, c, nh, dk):
    rows = nseq * t_len
    nt = t_len // ct
    w = nh * dk
    kern = functools.partial(_delta_kernel, ct=ct, c=c, nh=nh, dk=dk)
    return pl.pallas_call(
        kern,
        grid=(nseq, nt),
        in_specs=[
            pl.BlockSpec((ct, w), lambda s, t: (s * nt + t, 0)),
            pl.BlockSpec((ct, w), lambda s, t: (s * nt + t, 1)),
            pl.BlockSpec((ct, w), lambda s, t: (s * nt + t, 2)),
            pl.BlockSpec((ct, LANES), lambda s, t: (s * nt + t, 0)),
            pl.BlockSpec((None, nh, dk, dk), lambda s, t: (s, 0, 0, 0)),
        ],
        out_specs=[pl.BlockSpec((ct, w), lambda s, t: (s * nt + t, 0)),
                   pl.BlockSpec((None, nh, dk, dk), lambda s, t: (s, 0, 0, 0))],
        out_shape=[jax.ShapeDtypeStruct((rows, w), F32),
                   jax.ShapeDtypeStruct((nseq, nh, dk, dk), F32)],
        scratch_shapes=[pltpu.VMEM((nh, dk, dk), F32)],
        compiler_params=_cparams(("parallel", "arbitrary")),
        name="delta_rule",
    )(qkv, qkv, qkv, gb, s0)


def _outproj_kernel(*refs, gated, nh, dv, alpha):
    if gated:
        o_ref, z_ref, nw_ref, x_ref, w_ref, g_ref, b_ref, out_ref, a_scr = refs
        for h in range(nh):
            cs = slice(h * dv, (h + 1) * dv)
            oh = o_ref[:, cs]
            zh = z_ref[:, cs]
            oh = oh * lax.rsqrt(jnp.mean(oh * oh, axis=-1, keepdims=True) + RMS_EPS) * nw_ref[...]
            a_scr[:, cs] = (oh * (zh * _sigmoid(zh))).astype(BF16)
        a = a_scr[...]
    else:
        o_ref, x_ref, w_ref, g_ref, b_ref, out_ref = refs
        a = o_ref[...].astype(BF16)
    v = alpha * x_ref[...] + _dot(a, w_ref[...])
    out_ref[...] = _layer_norm(v, g_ref[...], b_ref[...])


def _out_proj(o, x, w, g, b, *, tm, alpha, z_src=None, z_blk=0, norm_w=None, nh=1, dv=1):
    rows, d_in = o.shape
    d = x.shape[1]
    gated = z_src is not None
    kern = functools.partial(_outproj_kernel, gated=gated, nh=nh, dv=dv, alpha=alpha)
    row_spec = lambda width: pl.BlockSpec((tm, width), lambda i: (i, 0))
    const = lambda shape: pl.BlockSpec(shape, lambda i: (0, 0))
    in_specs = [row_spec(d_in)]
    args = [o]
    scratch = []
    if gated:
        in_specs += [pl.BlockSpec((tm, d_in), lambda i: (i, z_blk)), const((1, dv))]
        args += [z_src, norm_w]
        scratch = [pltpu.VMEM((tm, d_in), BF16)]
    in_specs += [row_spec(d), const((d_in, d)), const((1, d)), const((1, d))]
    args += [x, w, g, b]
    return pl.pallas_call(
        kern,
        grid=(rows // tm,),
        in_specs=in_specs,
        out_specs=row_spec(d),
        out_shape=jax.ShapeDtypeStruct((rows, d), F32),
        scratch_shapes=scratch,
        compiler_params=_cparams(("parallel",)),
        name="out_proj_ln",
    )(*args)


def _ffn_kernel(h_ref, wg_ref, wu_ref, wd_ref, g_ref, b_ref, out_ref, acc_ref, *, tf, alpha):
    hb = h_ref[...].astype(BF16)
    for c in range(wg_ref.shape[1] // tf):
        fs = slice(c * tf, (c + 1) * tf)
        gate = _dot(hb, wg_ref[:, fs])
        up = _dot(hb, wu_ref[:, fs])
        act = (gate * _sigmoid(gate) * up).astype(BF16)
        down = _dot(act, wd_ref[fs, :])
        if c == 0:
            acc_ref[...] = down
        else:
            acc_ref[...] += down
    v = alpha * h_ref[...] + acc_ref[...]
    out_ref[...] = _layer_norm(v, g_ref[...], b_ref[...])


def _ffn(h, wg, wu, wd, g, b, *, tm, tf, alpha):
    rows, d = h.shape
    f = wg.shape[1]
    kern = functools.partial(_ffn_kernel, tf=tf, alpha=alpha)
    resident = lambda shape: pl.BlockSpec(shape, lambda i: (0, 0), pipeline_mode=pl.Buffered(1))
    return pl.pallas_call(
        kern,
        grid=(rows // tm,),
        in_specs=[pl.BlockSpec((tm, d), lambda i: (i, 0)),
                  resident((d, f)), resident((d, f)), resident((f, d)),
                  pl.BlockSpec((1, d), lambda i: (0, 0)),
                  pl.BlockSpec((1, d), lambda i: (0, 0))],
        out_specs=pl.BlockSpec((tm, d), lambda i: (i, 0)),
        out_shape=jax.ShapeDtypeStruct((rows, d), F32),
        scratch_shapes=[pltpu.VMEM((tm, d), F32)],
        compiler_params=_cparams(("parallel",)),
        name="ffn_ln",
    )(h, wg, wu, wd, g, b)


def _qkv_kernel(x_ref, w_ref, cos_ref, sin_ref, q_ref, k_ref, v_ref, *, nq, nk, half):
    y = _dot(x_ref[...].astype(BF16), w_ref[...])
    cos = cos_ref[...]
    sin = sin_ref[...]
    lane = lax.broadcasted_iota(jnp.int32, cos.shape, 1)
    first = jnp.bitwise_and(lane, 2 * half - 1) < half

    def rope(xg):
        partner = jnp.where(first, pltpu.roll(xg, LANES - half, 1), pltpu.roll(xg, half, 1))
        return xg * cos + partner * sin

    for c in range(nq // LANES):
        q_ref[:, c * LANES:(c + 1) * LANES] = rope(y[:, c * LANES:(c + 1) * LANES])
    for c in range(nk // LANES):
        k_ref[:, c * LANES:(c + 1) * LANES] = rope(y[:, nq + c * LANES:nq + (c + 1) * LANES])
    v_ref[...] = y[:, nq + nk:]


def _qkv_rope(x, w, cos, sin, *, tm, nq, nk, half):
    rows, d = x.shape
    n_tab = cos.shape[0] // tm
    kern = functools.partial(_qkv_kernel, nq=nq, nk=nk, half=half)
    return pl.pallas_call(
        kern,
        grid=(rows // tm,),
        in_specs=[pl.BlockSpec((tm, d), lambda i: (i, 0)),
                  pl.BlockSpec((d, nq + 2 * nk), lambda i: (0, 0)),
                  pl.BlockSpec((tm, LANES), lambda i: (i % n_tab, 0)),
                  pl.BlockSpec((tm, LANES), lambda i: (i % n_tab, 0))],
        out_specs=[pl.BlockSpec((tm, nq), lambda i: (i, 0)),
                   pl.BlockSpec((tm, nk), lambda i: (i, 0)),
                   pl.BlockSpec((tm, nk), lambda i: (i, 0))],
        out_shape=[jax.ShapeDtypeStruct((rows, nq), F32),
                   jax.ShapeDtypeStruct((rows, nk), F32),
                   jax.ShapeDtypeStruct((rows, nk), F32)],
        compiler_params=_cparams(("parallel",)),
        name="qkv_rope",
    )(x, w, cos, sin)


def _kmeans_kernel(k_ref, m_ref, *, nblk):
    kb = k_ref[...].reshape(nblk, MOBA_BLOCK, k_ref.shape[1])
    m_ref[...] = jnp.sum(kb, axis=1) * (1.0 / MOBA_BLOCK)


def _block_means(k, *, nblk_step):
    rows, w = k.shape
    total = rows // MOBA_BLOCK
    return pl.pallas_call(
        functools.partial(_kmeans_kernel, nblk=nblk_step),
        grid=(total // nblk_step,),
        in_specs=[pl.BlockSpec((nblk_step * MOBA_BLOCK, w), lambda i: (i, 0))],
        out_specs=pl.BlockSpec((nblk_step, w), lambda i: (i, 0)),
        out_shape=jax.ShapeDtypeStruct((total, w), F32),
        compiler_params=_cparams(("parallel",)),
        name="block_means",
    )(k)


def _top_select(gate, n_past, axis):
    n = gate.shape[axis]
    idx = lax.broadcasted_iota(jnp.int32, gate.shape, axis).astype(F32)
    gm = jnp.where(idx < n_past, gate, NEG_INF)
    sel = jnp.zeros(gate.shape, F32)
    for r in range(MOBA_TOPK):
        mx = jnp.max(gm, axis=axis, keepdims=True)
        first = jnp.min(jnp.where(gm == mx, idx, float(n)), axis=axis, keepdims=True)
        pick = idx == first
        keep = jnp.where(r < n_past, 1.0, 0.0).astype(F32)
        sel = jnp.maximum(sel, jnp.where(pick, keep, 0.0))
        gm = jnp.where(pick, NEG_INF, gm)
    return sel


def _moba_prompt_kernel(qi_ref, kj_ref, q_ref, k_ref, v_ref, mean_ref, o_ref,
                        qs_scr, sel_scr, m_scr, l_scr, acc_scr, *, tq, ngrp, grp, hd, scale):
    p = pl.program_id(1)
    qi = qi_ref[p]
    kj = kj_ref[p]
    rows = grp * tq

    @pl.when(kj == 0)
    def _():
        for g in range(ngrp):
            qg = jnp.concatenate(
                [q_ref[:, (g * grp + hh) * hd:(g * grp + hh + 1) * hd] for hh in range(grp)], axis=0)
            qs_scr[g] = (qg * scale).astype(BF16)
            gate = _dot_nt(qg, mean_ref[:, g * hd:(g + 1) * hd], HI)
            sel_scr[g] = _top_select(gate, qi, 1)
        m_scr[...] = jnp.full(m_scr.shape, NEG_INF, F32)
        l_scr[...] = jnp.zeros(l_scr.shape, F32)
        acc_scr[...] = jnp.zeros(acc_scr.shape, F32)

    def attend(g, mask):
        kg = k_ref[:, g * hd:(g + 1) * hd].astype(BF16)
        vg = v_ref[:, g * hd:(g + 1) * hd].astype(BF16)
        s = jnp.where(mask, _dot_nt(qs_scr[g], kg), NEG_INF)
        m_prev = m_scr[g]
        m_new = jnp.maximum(m_prev, jnp.max(s, axis=-1, keepdims=True))
        m_safe = jnp.where(m_new == NEG_INF, 0.0, m_new)
        alpha = jnp.exp(m_prev - m_safe)
        pr = jnp.exp(s - m_safe)
        l_scr[g] = alpha * l_scr[g] + jnp.sum(pr, axis=-1, keepdims=True)
        acc_scr[g] = alpha * acc_scr[g] + _dot(pr.astype(BF16), vg)
        m_scr[g] = m_new

    @pl.when(kj < qi)
    def _():
        for g in range(ngrp):
            sel = sel_scr[g]
            lane = lax.broadcasted_iota(jnp.int32, sel.shape, 1)
            picked = jnp.sum(jnp.where(lane == kj, sel, 0.0), axis=-1, keepdims=True) > 0.5
            attend(g, picked)

    @pl.when(kj == qi)
    def _():
        qpos = jnp.bitwise_and(lax.broadcasted_iota(jnp.int32, (rows, MOBA_BLOCK), 0), tq - 1)
        kpos = lax.broadcasted_iota(jnp.int32, (rows, MOBA_BLOCK), 1)
        causal = kpos <= qpos
        outs = []
        for g in range(ngrp):
            attend(g, causal)
            og = acc_scr[g] / l_scr[g]
            outs += [og[hh * tq:(hh + 1) * tq, :] for hh in range(grp)]
        o_ref[...] = jnp.concatenate(outs, axis=-1)


def _moba_prompt(q, k, v, means, *, nb_batch, t_len, ngrp, grp, hd):
    tq = MOBA_BLOCK
    nq = t_len // tq
    nb = t_len // MOBA_BLOCK
    pairs = [(i, j) for i in range(nq) for j in range(i + 1)]
    qi_tab = jnp.asarray(np.array([pq for pq, _ in pairs], np.int32))
    kj_tab = jnp.asarray(np.array([pk for _, pk in pairs], np.int32))
    rows = grp * tq
    w = ngrp * grp * hd
    kern = functools.partial(_moba_prompt_kernel, tq=tq, ngrp=ngrp, grp=grp, hd=hd,
                             scale=float(hd) ** -0.5)
    grid_spec = pltpu.PrefetchScalarGridSpec(
        num_scalar_prefetch=2,
        grid=(nb_batch, len(pairs)),
        in_specs=[
            pl.BlockSpec((tq, w), lambda b, p, qt, kt: (b * nq + qt[p], 0)),
            pl.BlockSpec((MOBA_BLOCK, ngrp * hd), lambda b, p, qt, kt: (b * nb + kt[p], 0)),
            pl.BlockSpec((MOBA_BLOCK, ngrp * hd), lambda b, p, qt, kt: (b * nb + kt[p], 0)),
            pl.BlockSpec((nb, ngrp * hd), lambda b, p, qt, kt: (b, 0)),
        ],
        out_specs=pl.BlockSpec((tq, w), lambda b, p, qt, kt: (b * nq + qt[p], 0)),
        scratch_shapes=[pltpu.VMEM((ngrp, rows, hd), BF16),
                        pltpu.VMEM((ngrp, rows, nb), F32),
                        pltpu.VMEM((ngrp, rows, 1), F32),
                        pltpu.VMEM((ngrp, rows, 1), F32),
                        pltpu.VMEM((ngrp, rows, hd), F32)],
    )
    return pl.pallas_call(
        kern,
        grid_spec=grid_spec,
        out_shape=jax.ShapeDtypeStruct(q.shape, F32),
        compiler_params=_cparams(("parallel", "arbitrary")),
        name="moba_prompt",
    )(qi_tab, kj_tab, q, k, v, means)


def _sample_partial_kernel(pt_ref, q_ref, *refs, npg, ngrp, grp, hd, scale):
    k_pages = refs[:npg]
    v_pages = refs[npg:2 * npg]
    mean_ref, m_ref, l_ref, po_ref, qs_scr = refs[2 * npg:]
    @pl.when(pl.program_id(1) == 0)
    def _():
        for g in range(ngrp):
            qg = jnp.concatenate(
                [q_ref[:, (g * grp + hh) * hd:(g * grp + hh + 1) * hd] for hh in range(grp)], axis=0)
            qs_scr[g] = (qg * scale).astype(BF16)

    pages_per_block = MOBA_BLOCK // k_pages[0].shape[0]
    for blk in range(npg // pages_per_block):
        sl = slice(blk * pages_per_block, (blk + 1) * pages_per_block)
        kb = jnp.concatenate([r[...] for r in k_pages[sl]], axis=0)
        vb = jnp.concatenate([r[...] for r in v_pages[sl]], axis=0)
        mean_ref[blk:blk + 1, :] = jnp.sum(kb, axis=0, keepdims=True) * (1.0 / MOBA_BLOCK)
        kbb = kb.astype(BF16)
        vt = vb.T.astype(BF16)
        ms, ls, os = [], [], []
        for g in range(ngrp):
            st = _dot_nt(kbb[:, g * hd:(g + 1) * hd], qs_scr[g])
            m = jnp.max(st, axis=0, keepdims=True)
            pr = jnp.exp(st - m)
            ms.append(m)
            ls.append(jnp.sum(pr, axis=0, keepdims=True))
            os.append(_dot(vt[g * hd:(g + 1) * hd, :], pr.astype(BF16)))
        m_ref[blk:blk + 1, :] = jnp.concatenate(ms, axis=1)
        l_ref[blk:blk + 1, :] = jnp.concatenate(ls, axis=1)
        po_ref[blk] = jnp.concatenate(os, axis=1)


def _sample_partials(q, cache_k, cache_v, page_table, *, npg, ngrp, grp, hd):
    db, n_pages = page_table.shape
    n_pool, page, kvw = cache_k.shape
    t_new = q.shape[0] // db
    ppb = MOBA_BLOCK // page
    bps = npg // ppb
    nsteps = n_pages // npg
    nrow = ngrp * grp * t_new
    kern = functools.partial(_sample_partial_kernel, npg=npg, ngrp=ngrp, grp=grp, hd=hd,
                             scale=float(hd) ** -0.5)

    def page_spec(i):
        return pl.BlockSpec((None, page, kvw),
                            lambda b, s, pt: (pt[b * n_pages + s * npg + i], 0, 0))

    grid_spec = pltpu.PrefetchScalarGridSpec(
        num_scalar_prefetch=1,
        grid=(db, nsteps),
        in_specs=[pl.BlockSpec((t_new, q.shape[1]), lambda b, s, pt: (b, 0))]
        + [page_spec(i) for i in range(npg)] + [page_spec(i) for i in range(npg)],
        out_specs=[pl.BlockSpec((None, None, bps, kvw), lambda b, s, pt: (b, s, 0, 0)),
                   pl.BlockSpec((None, None, bps, nrow), lambda b, s, pt: (b, s, 0, 0)),
                   pl.BlockSpec((None, None, bps, nrow), lambda b, s, pt: (b, s, 0, 0)),
                   pl.BlockSpec((None, None, bps, hd, nrow), lambda b, s, pt: (b, s, 0, 0, 0))],
        scratch_shapes=[pltpu.VMEM((ngrp, grp * t_new, hd), BF16)],
    )
    return pl.pallas_call(
        kern,
        grid_spec=grid_spec,
        out_shape=[jax.ShapeDtypeStruct((db, nsteps, bps, kvw), F32),
                   jax.ShapeDtypeStruct((db, nsteps, bps, nrow), F32),
                   jax.ShapeDtypeStruct((db, nsteps, bps, nrow), F32),
                   jax.ShapeDtypeStruct((db, nsteps, bps, hd, nrow), F32)],
        compiler_params=_cparams(("parallel", "arbitrary")),
        name="moba_sample_partials",
    )(page_table.reshape(-1), q, *([cache_k] * npg), *([cache_v] * npg))


def _sample_combine_kernel(q_ref, kn_ref, vn_ref, mean_ref, m_ref, l_ref, po_ref, o_ref, w_scr,
                           *, ngrp, grp, hd, scale):
    t_new = q_ref.shape[0]
    nblk = mean_ref.shape[0]
    gr = grp * t_new
    gates, qss = [], []
    for g in range(ngrp):
        qg = jnp.concatenate(
            [q_ref[:, (g * grp + hh) * hd:(g * grp + hh + 1) * hd] for hh in range(grp)], axis=0)
        qss.append((qg * scale).astype(BF16))
        gates.append(_dot_nt(mean_ref[:, g * hd:(g + 1) * hd], qg, HI))
    sel = _top_select(jnp.concatenate(gates, axis=1), nblk, 0) > 0.5
    m_all = m_ref[...]
    m_past = jnp.max(jnp.where(sel, m_all, NEG_INF), axis=0, keepdims=True)
    kpos = lax.broadcasted_iota(jnp.int32, (t_new, gr), 0)
    qpos = jnp.bitwise_and(lax.broadcasted_iota(jnp.int32, (t_new, gr), 1), t_new - 1)
    causal = kpos <= qpos
    kn = kn_ref[...].astype(BF16)
    vnt = vn_ref[...].T.astype(BF16)
    s_own = [jnp.where(causal, _dot_nt(kn[:, g * hd:(g + 1) * hd], qss[g]), NEG_INF)
             for g in range(ngrp)]
    m_own = jnp.concatenate([jnp.max(s, axis=0, keepdims=True) for s in s_own], axis=1)
    m_fin = jnp.maximum(m_past, m_own)
    w_scr[...] = jnp.where(sel, jnp.exp(m_all - m_fin), 0.0)
    p_own = [jnp.exp(s_own[g] - m_fin[:, g * gr:(g + 1) * gr]) for g in range(ngrp)]
    l_fin = (jnp.sum(w_scr[...] * l_ref[...], axis=0, keepdims=True)
             + jnp.concatenate([jnp.sum(pg, axis=0, keepdims=True) for pg in p_own], axis=1))
    acc = jnp.concatenate(
        [_dot(vnt[g * hd:(g + 1) * hd, :], p_own[g].astype(BF16)) for g in range(ngrp)], axis=1)

    def body(j, a):
        return a + w_scr[pl.ds(j, 1), :] * po_ref[j]

    acc = lax.fori_loop(0, nblk, body, acc)
    out_t = (acc / l_fin).T
    o_ref[...] = jnp.concatenate(
        [out_t[r * t_new:(r + 1) * t_new, :] for r in range(ngrp * grp)], axis=1)


def _sample_combine(q, k_new, v_new, means, m_part, l_part, po, *, db, ngrp, grp, hd):
    t_new = q.shape[0] // db
    nblk = means.shape[1]
    nrow = ngrp * grp * t_new
    kvw = ngrp * hd
    kern = functools.partial(_sample_combine_kernel, ngrp=ngrp, grp=grp, hd=hd,
                             scale=float(hd) ** -0.5)
    return pl.pallas_call(
        kern,
        grid=(db,),
        in_specs=[pl.BlockSpec((t_new, q.shape[1]), lambda b: (b, 0)),
                  pl.BlockSpec((t_new, kvw), lambda b: (b, 0)),
                  pl.BlockSpec((t_new, kvw), lambda b: (b, 0)),
                  pl.BlockSpec((None, nblk, kvw), lambda b: (b, 0, 0)),
                  pl.BlockSpec((None, nblk, nrow), lambda b: (b, 0, 0)),
                  pl.BlockSpec((None, nblk, nrow), lambda b: (b, 0, 0)),
                  pl.BlockSpec((None, nblk, hd, nrow), lambda b: (b, 0, 0, 0))],
        out_specs=pl.BlockSpec((t_new, q.shape[1]), lambda b: (b, 0)),
        out_shape=jax.ShapeDtypeStruct(q.shape, F32),
        scratch_shapes=[pltpu.VMEM((nblk, nrow), F32)],
        compiler_params=_cparams(("parallel",)),
        name="moba_sample_combine",
    )(q, k_new, v_new, means, m_part, l_part, po)


def _rope_tables(pos, hd, reps):
    half = hd // 2
    inv_freq = ROPE_THETA ** (-jnp.arange(half, dtype=F32) / half)
    ang = pos.astype(F32)[:, None] * inv_freq[None, :]
    cos = jnp.cos(ang)
    sin = jnp.sin(ang)
    per_vreg = LANES // hd
    cos_t = jnp.tile(jnp.concatenate([cos, cos], axis=1), (reps, per_vreg))
    sin_t = jnp.tile(jnp.concatenate([-sin, sin], axis=1), (reps, per_vreg))
    return cos_t, sin_t


def _trunk(x, pos, conv_state8, delta_state, attend, wts, *, nseq, t_len, tm, prep_tt,
           delta_ct, delta_c, dims):
    nh, dk, conv_dim, n_a, depth, alpha, bh, kvh, hd = dims
    rows = nseq * t_len
    new_conv, new_delta = [], []
    k_sh = v_sh = None
    for layer in range(depth):
        if layer < n_a:
            proj = _matmul(x, wts["a_w_in"][layer], tm, wts["a_w_in"][layer].shape[1] // 3, "in_proj")
            qkv, gb = _gdn_prep(proj, conv_state8[layer], wts["a_conv"][layer], wts["a_log"][layer],
                                wts["a_dtb"][layer], nseq=nseq, t_len=t_len, tt=prep_tt,
                                conv_dim=conv_dim, nh=nh, dk=dk)
            o, s_new = _delta_rule(qkv, gb, delta_state[layer], nseq=nseq, t_len=t_len,
                                   ct=delta_ct, c=delta_c, nh=nh, dk=dk)
            cw = wts["a_conv"][layer].shape[0]
            new_conv.append(proj.reshape(nseq, t_len, -1)[:, t_len - (cw - 1):, :conv_dim])
            new_delta.append(s_new)
            h = _out_proj(o, x, wts["a_w_out"][layer], wts["ln_g"][layer][0], wts["ln_b"][layer][0],
                          tm=tm, alpha=alpha, z_src=proj, z_blk=conv_dim // (nh * dk),
                          norm_w=wts["a_norm"][layer], nh=nh, dv=dk)
        else:
            j = layer - n_a
            w_all = wts["b_w_qkv"][j]
            cos_t, sin_t = _rope_tables(pos, hd, (tm // t_len) if tm > t_len else 1)
            q, k_l, v_l = _qkv_rope(x, w_all, cos_t, sin_t, tm=tm, nq=bh * hd, nk=kvh * hd,
                                    half=hd // 2)
            if layer == n_a:
                k_sh, v_sh = k_l, v_l
            att = attend(q, k_sh, v_sh)
            h = _out_proj(att, x, wts["b_w_o"][j], wts["ln_g"][layer][0], wts["ln_b"][layer][0],
                          tm=tm, alpha=alpha)
        x = _ffn(h, wts["ffn_wg"][layer], wts["ffn_wu"][layer], wts["ffn_wd"][layer],
                 wts["ln_g"][layer][1], wts["ln_b"][layer][1], tm=tm, tf=256, alpha=alpha)
    del rows
    return x, jnp.stack(new_conv), jnp.stack(new_delta), k_sh, v_sh


def kernel(x_prompt, x_sample, state_conv, state_delta, cache_k, cache_v, page_table, ln_g, ln_b,
           a_w_in, a_conv, a_log_decay, a_dt_bias, a_norm, a_w_out, kv_w_k, kv_w_v, b_w_q, b_w_o,
           ffn_w_gate, ffn_w_up, ffn_w_down):
    bp, tp, d = x_prompt.shape
    db, ts, _ = x_sample.shape
    depth = ln_g.shape[0]
    n_a = a_w_in.shape[0]
    nh = a_log_decay.shape[1]
    dk = state_delta.shape[-2]
    conv_dim = a_conv.shape[-1]
    cw = a_conv.shape[1]
    n_pool, page, kvh, hd = cache_k.shape
    bh = b_w_q.shape[-1] // hd
    grp = bh // kvh
    n_pages = page_table.shape[1]
    past = n_pages * page
    alpha = (2.0 * depth) ** 0.25
    assert past % MOBA_BLOCK == 0 and ts <= MOBA_BLOCK and ts >= cw - 1 and tp >= cw - 1
    assert state_delta.shape[-1] == dk and conv_dim == 3 * nh * dk

    in_w = a_w_in.shape[-1]
    in_pad = -(-in_w // (3 * LANES)) * (3 * LANES)
    pad_lanes = lambda v: jnp.pad(v, ((0, 0), (0, LANES - v.shape[-1])))[:, None, :]
    wts = {
        "a_w_in": jnp.pad(a_w_in, ((0, 0), (0, 0), (0, in_pad - in_w))).astype(BF16),
        "a_conv": a_conv,
        "a_log": pad_lanes(a_log_decay),
        "a_dtb": pad_lanes(a_dt_bias),
        "a_norm": a_norm[:, None, :],
        "a_w_out": a_w_out.astype(BF16),
        "b_w_qkv": jnp.concatenate(
            [b_w_q, jnp.broadcast_to(kv_w_k, (b_w_q.shape[0],) + kv_w_k.shape),
             jnp.broadcast_to(kv_w_v, (b_w_q.shape[0],) + kv_w_v.shape)], axis=-1).astype(BF16),
        "b_w_o": b_w_o.astype(BF16),
        "ffn_wg": ffn_w_gate.astype(BF16),
        "ffn_wu": ffn_w_up.astype(BF16),
        "ffn_wd": ffn_w_down.astype(BF16),
        "ln_g": ln_g[:, :, None, :],
        "ln_b": ln_b[:, :, None, :],
    }
    dims = (nh, dk, conv_dim, n_a, depth, alpha, bh, kvh, hd)

    nb = tp // MOBA_BLOCK

    def attend_prompt(q, k, v):
        total = bp * nb
        step = SUBLANES if total % SUBLANES == 0 else total
        means = _block_means(k, nblk_step=step)
        return _moba_prompt(q, k, v, means, nb_batch=bp, t_len=tp, ngrp=kvh, grp=grp, hd=hd)

    tm_p = 512 if (bp * tp) % 512 == 0 else 256
    y_p, p_conv, p_delta, p_k, p_v = _trunk(
        x_prompt.reshape(bp * tp, d), jnp.arange(tp, dtype=jnp.int32),
        jnp.zeros((n_a, bp, SUBLANES, conv_dim), F32), jnp.zeros((n_a, bp, nh, dk, dk), F32),
        attend_prompt, wts, nseq=bp, t_len=tp, tm=tm_p, prep_tt=256, delta_ct=256,
        delta_c=DELTA_CHUNK, dims=dims)

    ck = cache_k.reshape(n_pool, page, kvh * hd)
    cv = cache_v.reshape(n_pool, page, kvh * hd)
    npg = 8 if n_pages % 8 == 0 else (MOBA_BLOCK // page)

    def attend_sample(q, k_new, v_new):
        means, m_part, l_part, po = _sample_partials(q, ck, cv, page_table, npg=npg, ngrp=kvh,
                                                     grp=grp, hd=hd)
        nblk = past // MOBA_BLOCK
        nrow = bh * ts
        return _sample_combine(q, k_new, v_new, means.reshape(db, nblk, kvh * hd),
                               m_part.reshape(db, nblk, nrow), l_part.reshape(db, nblk, nrow),
                               po.reshape(db, nblk, hd, nrow), db=db, ngrp=kvh, grp=grp, hd=hd)

    conv8 = jnp.pad(state_conv, ((0, 0), (0, 0), (SUBLANES - (cw - 1), 0), (0, 0)))
    y_s, s_conv, s_delta, s_k, s_v = _trunk(
        x_sample.reshape(db * ts, d), past + jnp.arange(ts, dtype=jnp.int32), conv8, state_delta,
        attend_sample, wts, nseq=db, t_len=ts, tm=db * ts, prep_tt=ts, delta_ct=ts, delta_c=ts,
        dims=dims)

    return (y_p.reshape(bp, tp, d), y_s.reshape(db, ts, d), p_conv, p_delta,
            p_k.reshape(bp, tp, kvh, hd), p_v.reshape(bp, tp, kvh, hd), s_conv, s_delta,
            s_k.reshape(db, ts, kvh, hd), s_v.reshape(db, ts, kvh, hd))
```

```python
import functools

import numpy as np
import jax
import jax.numpy as jnp
from jax import lax
from jax.experimental import pallas as pl
from jax.experimental.pallas import tpu as pltpu

F32 = jnp.float32
BF16 = jnp.bfloat16
HI = lax.Precision.HIGHEST

LANES = 128
SUBLANES = 8
VMEM_LIMIT_MB = 56

MOBA_BLOCK = 256
MOBA_TOPK = 3
DELTA_CHUNK = 64
ROPE_THETA = 10000.0
LN_EPS = 1e-5
RMS_EPS = 1e-6
L2_EPS = 1e-6
NEG_INF = float("-inf")


def _cparams(sem):
    return pltpu.CompilerParams(dimension_semantics=sem,
                                vmem_limit_bytes=VMEM_LIMIT_MB * 1024 * 1024)


def _dot(a, b, prec=None):
    return jnp.dot(a, b, precision=prec, preferred_element_type=F32)


def _dot_nt(a, b, prec=None):
    return lax.dot_general(a, b, (((1,), (1,)), ((), ())), precision=prec,
                           preferred_element_type=F32)


def _dot_tn(a, b, prec=None):
    return lax.dot_general(a, b, (((0,), (0,)), ((), ())), precision=prec,
                           preferred_element_type=F32)


def _sigmoid(x):
    return 1.0 / (1.0 + jnp.exp(-x))


def _layer_norm(v, g, b):
    mu = jnp.mean(v, axis=-1, keepdims=True)
    d = v - mu
    var = jnp.mean(d * d, axis=-1, keepdims=True)
    return d * lax.rsqrt(var + LN_EPS) * g + b


def _mm_kernel(x_ref, w_ref, o_ref):
    o_ref[...] = _dot(x_ref[...].astype(BF16), w_ref[...])


def _matmul(x, w, tm, tn, name):
    r, k = x.shape
    n = w.shape[1]
    return pl.pallas_call(
        _mm_kernel,
        grid=(r // tm, n // tn),
        in_specs=[pl.BlockSpec((tm, k), lambda i, j: (i, 0)),
                  pl.BlockSpec((k, tn), lambda i, j: (0, j))],
        out_specs=pl.BlockSpec((tm, tn), lambda i, j: (i, j)),
        out_shape=jax.ShapeDtypeStruct((r, n), F32),
        compiler_params=_cparams(("parallel", "parallel")),
        name=name,
    )(x, w)


def _prep_kernel(cur_ref, prev_ref, st_ref, ab_ref, cw_ref, alog_ref, dtb_ref,
                 qkv_ref, gb_ref, xp_ref, *, tt, conv_w, n_q, n_qk, nh, q_scale):
    t = pl.program_id(1)
    hist = SUBLANES

    @pl.when(t == 0)
    def _():
        xp_ref[0:hist, :] = st_ref[...]

    @pl.when(t != 0)
    def _():
        xp_ref[0:hist, :] = prev_ref[...]

    xp_ref[hist:hist + tt, :] = cur_ref[...]
    base = hist - (conv_w - 1)
    for c in range(cur_ref.shape[1] // LANES):
        cs = slice(c * LANES, (c + 1) * LANES)
        y = xp_ref[base:base + tt, cs] * cw_ref[0:1, cs]
        for i in range(1, conv_w):
            y = y + xp_ref[base + i:base + i + tt, cs] * cw_ref[i:i + 1, cs]
        y = y * _sigmoid(y)
        if c < n_qk:
            y = y * lax.rsqrt(jnp.sum(y * y, axis=-1, keepdims=True) + L2_EPS)
            if c < n_q:
                y = y * q_scale
        qkv_ref[:, cs] = y
    ab = ab_ref[...]
    lane = lax.broadcasted_iota(jnp.int32, ab.shape, 1)
    sp = ab + dtb_ref[...]
    softplus = jnp.maximum(sp, 0.0) + jnp.log1p(jnp.exp(-jnp.abs(sp)))
    g = -jnp.exp(alog_ref[...]) * softplus
    gb_ref[...] = jnp.where(lane < nh, g, _sigmoid(ab))


def _gdn_prep(proj, state8, conv_w, alog, dtb, *, nseq, t_len, tt, conv_dim, nh, dk):
    rows = nseq * t_len
    nt = t_len // tt
    cw = conv_w.shape[0]
    ab_blk = (conv_dim + nh * dk) // LANES
    kern = functools.partial(_prep_kernel, tt=tt, conv_w=cw, n_q=nh, n_qk=2 * nh, nh=nh,
                             q_scale=float(dk) ** -0.5)
    tpb = tt // SUBLANES
    return pl.pallas_call(
        kern,
        grid=(nseq, nt),
        in_specs=[
            pl.BlockSpec((tt, conv_dim), lambda s, t: (s * nt + t, 0)),
            pl.BlockSpec((SUBLANES, conv_dim),
                         lambda s, t: (jnp.maximum((s * nt + t) * tpb - 1, 0), 0)),
            pl.BlockSpec((None, SUBLANES, conv_dim), lambda s, t: (s, 0, 0)),
            pl.BlockSpec((tt, LANES), lambda s, t: (s * nt + t, ab_blk)),
            pl.BlockSpec((cw, conv_dim), lambda s, t: (0, 0)),
            pl.BlockSpec((1, LANES), lambda s, t: (0, 0)),
            pl.BlockSpec((1, LANES), lambda s, t: (0, 0)),
        ],
        out_specs=[pl.BlockSpec((tt, conv_dim), lambda s, t: (s * nt + t, 0)),
                   pl.BlockSpec((tt, LANES), lambda s, t: (s * nt + t, 0))],
        out_shape=[jax.ShapeDtypeStruct((rows, conv_dim), F32),
                   jax.ShapeDtypeStruct((rows, LANES), F32)],
        scratch_shapes=[pltpu.VMEM((tt + SUBLANES, conv_dim), F32)],
        compiler_params=_cparams(("parallel", "arbitrary")),
        name="gdn_prep",
    )(proj, proj, state8, proj, conv_w, alog, dtb)


def _delta_kernel(q_ref, k_ref, v_ref, gb_ref, s0_ref, o_ref, s_ref, st_scr, *, ct, c, nh, dk):
    t = pl.program_id(1)

    @pl.when(t == 0)
    def _():
        st_scr[...] = s0_ref[...]

    row = lax.broadcasted_iota(jnp.int32, (c, c), 0)
    col = lax.broadcasted_iota(jnp.int32, (c, c), 1)
    causal = row >= col
    strict = row > col
    tril = causal.astype(F32)
    eye = (row == col).astype(F32)
    n_sq = max(int(np.ceil(np.log2(c))) - 1, 0)

    def chunk(r0):
        rs = pl.ds(r0, c)
        gbc = gb_ref[rs, :]
        gc = _dot(tril, gbc, HI)
        gt = gc.T
        eg = jnp.exp(gc)
        heads = range(nh)
        qs, ks, qkm, xs, rhs, gcols, glasts = [], [], [], [], [], [], []
        for h in heads:
            cs = slice(h * dk, (h + 1) * dk)
            qh = q_ref[rs, cs]
            kh = k_ref[rs, cs]
            vh = v_ref[rs, cs]
            gcol = gc[:, h:h + 1]
            grow = gt[h:h + 1, :]
            bcol = gbc[:, nh + h:nh + h + 1]
            egcol = eg[:, h:h + 1]
            decay = jnp.where(causal, jnp.exp(jnp.where(causal, gcol - grow, 0.0)), 0.0)
            kb = kh.astype(BF16)
            kk = _dot_nt(kb, kb)
            qk = _dot_nt(qh.astype(BF16), kb)
            xs.append(-jnp.where(strict, bcol * kk * decay, 0.0))
            qkm.append(jnp.where(causal, qk * decay, 0.0).astype(BF16))
            rhs.append(jnp.concatenate([vh * bcol, kh * (bcol * egcol)], axis=-1))
            qs.append((qh * egcol).astype(BF16))
            ks.append(kh)
            gcols.append(gcol)
            glasts.append(grow[:, c - 1:c])
        tms = [eye + x for x in xs]
        ps = xs
        for _ in range(n_sq):
            ps = [_dot(p, p, HI) for p in ps]
            tms = [tm + _dot(tm, p, HI) for tm, p in zip(tms, ps)]
        sols = [_dot(tm, r, HI) for tm, r in zip(tms, rhs)]
        for h in heads:
            cs = slice(h * dk, (h + 1) * dk)
            u = sols[h][:, :dk]
            w = sols[h][:, dk:]
            s = st_scr[h]
            sb = s.astype(BF16)
            v_new = u - _dot(w.astype(BF16), sb)
            vb = v_new.astype(BF16)
            o_ref[rs, cs] = _dot(qs[h], sb) + _dot(qkm[h], vb)
            kd = (ks[h] * jnp.exp(glasts[h] - gcols[h])).astype(BF16)
            st_scr[h] = s * jnp.exp(glasts[h]) + _dot_tn(kd, vb)

    n_chunks = ct // c
    if n_chunks == 1:
        chunk(0)
    else:
        def body(ci, carry):
            chunk(pl.multiple_of(ci * c, c))
            return carry
        lax.fori_loop(0, n_chunks, body, 0)

    @pl.when(t == pl.num_programs(1) - 1)
    def _():
        s_ref[...] = st_scr[...]


def _delta_rule(qkv, gb, s0, *, nseq, t_len, ct, c, nh, dk):
    rows = nseq * t_len
    nt = t_len // ct
    w = nh * dk
    kern = functools.partial(_delta_kernel, ct=ct, c=c, nh=nh, dk=dk)
    return pl.pallas_call(
        kern,
        grid=(nseq, nt),
        in_specs=[
            pl.BlockSpec((ct, w), lambda s, t: (s * nt + t, 0)),
            pl.BlockSpec((ct, w), lambda s, t: (s * nt + t, 1)),
            pl.BlockSpec((ct, w), lambda s, t: (s * nt + t, 2)),
            pl.BlockSpec((ct, LANES), lambda s, t: (s * nt + t, 0)),
            pl.BlockSpec((None, nh, dk, dk), lambda s, t: (s, 0, 0, 0)),
        ],
        out_specs=[pl.BlockSpec((ct, w), lambda s, t: (s * nt + t, 0)),
                   pl.BlockSpec((None, nh, dk, dk), lambda s, t: (s, 0, 0, 0))],
        out_shape=[jax.ShapeDtypeStruct((rows, w), F32),
                   jax.ShapeDtypeStruct((nseq, nh, dk, dk), F32)],
        scratch_shapes=[pltpu.VMEM((nh, dk, dk), F32)],
        compiler_params=_cparams(("parallel", "arbitrary")),
        name="delta_rule",
    )(qkv, qkv, qkv, gb, s0)


def _outproj_kernel(*refs, gated, nh, dv, alpha):
    if gated:
        o_ref, z_ref, nw_ref, x_ref, w_ref, g_ref, b_ref, out_ref, a_scr = refs
        for h in range(nh):
            cs = slice(h * dv, (h + 1) * dv)
            oh = o_ref[:, cs]
            zh = z_ref[:, cs]
            oh = oh * lax.rsqrt(jnp.mean(oh * oh, axis=-1, keepdims=True) + RMS_EPS) * nw_ref[...]
            a_scr[:, cs] = (oh * (zh * _sigmoid(zh))).astype(BF16)
        a = a_scr[...]
    else:
        o_ref, x_ref, w_ref, g_ref, b_ref, out_ref = refs
        a = o_ref[...].astype(BF16)
    v = alpha * x_ref[...] + _dot(a, w_ref[...])
    out_ref[...] = _layer_norm(v, g_ref[...], b_ref[...])


def _out_proj(o, x, w, g, b, *, tm, alpha, z_src=None, z_blk=0, norm_w=None, nh=1, dv=1):
    rows, d_in = o.shape
    d = x.shape[1]
    gated = z_src is not None
    kern = functools.partial(_outproj_kernel, gated=gated, nh=nh, dv=dv, alpha=alpha)
    row_spec = lambda width: pl.BlockSpec((tm, width), lambda i: (i, 0))
    const = lambda shape: pl.BlockSpec(shape, lambda i: (0, 0))
    in_specs = [row_spec(d_in)]
    args = [o]
    scratch = []
    if gated:
        in_specs += [pl.BlockSpec((tm, d_in), lambda i: (i, z_blk)), const((1, dv))]
        args += [z_src, norm_w]
        scratch = [pltpu.VMEM((tm, d_in), BF16)]
    in_specs += [row_spec(d), const((d_in, d)), const((1, d)), const((1, d))]
    args += [x, w, g, b]
    return pl.pallas_call(
        kern,
        grid=(rows // tm,),
        in_specs=in_specs,
        out_specs=row_spec(d),
        out_shape=jax.ShapeDtypeStruct((rows, d), F32),
        scratch_shapes=scratch,
        compiler_params=_cparams(("parallel",)),
        name="out_proj_ln",
    )(*args)


def _ffn_kernel(h_ref, wg_ref, wu_ref, wd_ref, g_ref, b_ref, out_ref, acc_ref, *, tf, alpha):
    hb = h_ref[...].astype(BF16)
    for c in range(wg_ref.shape[1] // tf):
        fs = slice(c * tf, (c + 1) * tf)
        gate = _dot(hb, wg_ref[:, fs])
        up = _dot(hb, wu_ref[:, fs])
        act = (gate * _sigmoid(gate) * up).astype(BF16)
        down = _dot(act, wd_ref[fs, :])
        if c == 0:
            acc_ref[...] = down
        else:
            acc_ref[...] += down
    v = alpha * h_ref[...] + acc_ref[...]
    out_ref[...] = _layer_norm(v, g_ref[...], b_ref[...])


def _ffn(h, wg, wu, wd, g, b, *, tm, tf, alpha):
    rows, d = h.shape
    f = wg.shape[1]
    kern = functools.partial(_ffn_kernel, tf=tf, alpha=alpha)
    resident = lambda shape: pl.BlockSpec(shape, lambda i: (0, 0), pipeline_mode=pl.Buffered(1))
    return pl.pallas_call(
        kern,
        grid=(rows // tm,),
        in_specs=[pl.BlockSpec((tm, d), lambda i: (i, 0)),
                  resident((d, f)), resident((d, f)), resident((f, d)),
                  pl.BlockSpec((1, d), lambda i: (0, 0)),
                  pl.BlockSpec((1, d), lambda i: (0, 0))],
        out_specs=pl.BlockSpec((tm, d), lambda i: (i, 0)),
        out_shape=jax.ShapeDtypeStruct((rows, d), F32),
        scratch_shapes=[pltpu.VMEM((tm, d), F32)],
        compiler_params=_cparams(("parallel",)),
        name="ffn_ln",
    )(h, wg, wu, wd, g, b)


def _qkv_kernel(x_ref, w_ref, cos_ref, sin_ref, q_ref, k_ref, v_ref, *vt_ref, nq, nk, half):
    y = _dot(x_ref[...].astype(BF16), w_ref[...])
    if vt_ref:
        vt_ref[0][...] = y[:, nq + nk:].T
    cos = cos_ref[...]
    sin = sin_ref[...]
    lane = lax.broadcasted_iota(jnp.int32, cos.shape, 1)
    first = jnp.bitwise_and(lane, 2 * half - 1) < half

    def rope(xg):
        partner = jnp.where(first, pltpu.roll(xg, LANES - half, 1), pltpu.roll(xg, half, 1))
        return xg * cos + partner * sin

    for c in range(nq // LANES):
        q_ref[:, c * LANES:(c + 1) * LANES] = rope(y[:, c * LANES:(c + 1) * LANES])
    for c in range(nk // LANES):
        k_ref[:, c * LANES:(c + 1) * LANES] = rope(y[:, nq + c * LANES:nq + (c + 1) * LANES])
    v_ref[...] = y[:, nq + nk:]


def _qkv_rope(x, w, cos, sin, *, tm, nq, nk, half, vt_seq=None):
    rows, d = x.shape
    n_tab = cos.shape[0] // tm
    kern = functools.partial(_qkv_kernel, nq=nq, nk=nk, half=half)
    out_specs = [pl.BlockSpec((tm, nq), lambda i: (i, 0)),
                 pl.BlockSpec((tm, nk), lambda i: (i, 0)),
                 pl.BlockSpec((tm, nk), lambda i: (i, 0))]
    out_shape = [jax.ShapeDtypeStruct((rows, nq), F32),
                 jax.ShapeDtypeStruct((rows, nk), F32),
                 jax.ShapeDtypeStruct((rows, nk), F32)]
    if vt_seq is not None:
        nt = vt_seq // tm
        out_specs.append(pl.BlockSpec((None, nk, tm), lambda i: (i // nt, 0, i % nt)))
        out_shape.append(jax.ShapeDtypeStruct((rows // vt_seq, nk, vt_seq), F32))
    return pl.pallas_call(
        kern,
        grid=(rows // tm,),
        in_specs=[pl.BlockSpec((tm, d), lambda i: (i, 0)),
                  pl.BlockSpec((d, nq + 2 * nk), lambda i: (0, 0)),
                  pl.BlockSpec((tm, LANES), lambda i: (i % n_tab, 0)),
                  pl.BlockSpec((tm, LANES), lambda i: (i % n_tab, 0))],
        out_specs=out_specs,
        out_shape=out_shape,
        compiler_params=_cparams(("parallel",)),
        name="qkv_rope",
    )(x, w, cos, sin)


def _kmeans_kernel(k_ref, m_ref, *, nblk):
    kb = k_ref[...].reshape(nblk, MOBA_BLOCK, k_ref.shape[1])
    m_ref[...] = jnp.sum(kb, axis=1) * (1.0 / MOBA_BLOCK)


def _block_means(k, *, nblk_step):
    rows, w = k.shape
    total = rows // MOBA_BLOCK
    return pl.pallas_call(
        functools.partial(_kmeans_kernel, nblk=nblk_step),
        grid=(total // nblk_step,),
        in_specs=[pl.BlockSpec((nblk_step * MOBA_BLOCK, w), lambda i: (i, 0))],
        out_specs=pl.BlockSpec((nblk_step, w), lambda i: (i, 0)),
        out_shape=jax.ShapeDtypeStruct((total, w), F32),
        compiler_params=_cparams(("parallel",)),
        name="block_means",
    )(k)


def _top_select(gate, n_past, axis):
    n = gate.shape[axis]
    idx = lax.broadcasted_iota(jnp.int32, gate.shape, axis).astype(F32)
    gm = jnp.where(idx < n_past, gate, NEG_INF)
    sel = jnp.zeros(gate.shape, F32)
    for r in range(MOBA_TOPK):
        mx = jnp.max(gm, axis=axis, keepdims=True)
        first = jnp.min(jnp.where(gm == mx, idx, float(n)), axis=axis, keepdims=True)
        pick = idx == first
        keep = jnp.where(r < n_past, 1.0, 0.0).astype(F32)
        sel = jnp.maximum(sel, jnp.where(pick, keep, 0.0))
        gm = jnp.where(pick, NEG_INF, gm)
    return sel


def _moba_prompt_kernel(qi_ref, kj_ref, q_ref, k_ref, vt_ref, mean_ref, o_ref,
                        qs_scr, sel_scr, m_scr, l_scr, acc_scr, *, tq, ngrp, grp, hd, scale):
    p = pl.program_id(1)
    qi = qi_ref[p]
    kj = kj_ref[p]
    rows = grp * tq

    @pl.when(kj == 0)
    def _():
        qt = q_ref[...].T
        for g in range(ngrp):
            qg = jnp.concatenate(
                [qt[(g * grp + hh) * hd:(g * grp + hh + 1) * hd, :] for hh in range(grp)], axis=1)
            qs_scr[g] = (qg * scale).astype(BF16)
            gate = _dot(mean_ref[:, g * hd:(g + 1) * hd], qg, HI)
            sel_scr[g] = _top_select(gate, qi, 0)
        m_scr[...] = jnp.full(m_scr.shape, NEG_INF, F32)
        l_scr[...] = jnp.zeros(l_scr.shape, F32)
        acc_scr[...] = jnp.zeros(acc_scr.shape, F32)

    ones_rows = jnp.ones((2 * SUBLANES, MOBA_BLOCK), BF16)

    def attend(g, diagonal):
        kg = k_ref[:, g * hd:(g + 1) * hd].astype(BF16)
        vg = jnp.concatenate([vt_ref[g * hd:(g + 1) * hd, :].astype(BF16), ones_rows], axis=0)
        sel_row = None if diagonal else sel_scr[g, pl.ds(kj, 1), :]
        chunks = [slice(c * LANES, (c + 1) * LANES) for c in range(rows // LANES)]
        ss = [_dot(kg, qs_scr[g, :, ls]) for ls in chunks]
        if diagonal:
            kpos = lax.broadcasted_iota(jnp.int32, ss[0].shape, 0)
            lane = lax.broadcasted_iota(jnp.int32, ss[0].shape, 1)
            ss = [jnp.where(kpos <= jnp.bitwise_and(lane + ls.start, tq - 1), s, NEG_INF)
                  for s, ls in zip(ss, chunks)]
        m_prevs = [m_scr[g, :, ls] for ls in chunks]
        m_news = [jnp.maximum(mp, jnp.max(s, axis=0, keepdims=True)) for mp, s in zip(m_prevs, ss)]
        ps = [jnp.exp(s - mn).astype(BF16) for s, mn in zip(ss, m_news)]
        pvs = [_dot(vg, p) for p in ps]
        for ls, mp, mn, pv in zip(chunks, m_prevs, m_news, pvs):
            alpha = jnp.exp(mp - mn)
            acc = alpha * acc_scr[g, :, ls] + pv[:hd, :]
            l = alpha * l_scr[g, :, ls] + pv[hd:hd + 1, :]
            if diagonal:
                acc_scr[g, :, ls] = acc
                l_scr[g, :, ls] = l
            else:
                picked = sel_row[:, ls] > 0.5
                acc_scr[g, :, ls] = jnp.where(picked, acc, acc_scr[g, :, ls])
                l_scr[g, :, ls] = jnp.where(picked, l, l_scr[g, :, ls])
                m_scr[g, :, ls] = jnp.where(picked, mn, mp)

    @pl.when(kj < qi)
    def _():
        for g in range(ngrp):
            attend(g, False)

    @pl.when(kj == qi)
    def _():
        outs = []
        for g in range(ngrp):
            attend(g, True)
            og = acc_scr[g] / l_scr[g]
            outs += [og[:, hh * tq:(hh + 1) * tq] for hh in range(grp)]
        o_ref[...] = jnp.concatenate(outs, axis=0).T


def _moba_prompt(q, k, vt, means, *, nb_batch, t_len, ngrp, grp, hd):
    tq = MOBA_BLOCK
    nq = t_len // tq
    nb = t_len // MOBA_BLOCK
    pairs = [(i, j) for i in range(nq) for j in range(i + 1)]
    qi_tab = jnp.asarray(np.array([pq for pq, _ in pairs], np.int32))
    kj_tab = jnp.asarray(np.array([pk for _, pk in pairs], np.int32))
    rows = grp * tq
    w = ngrp * grp * hd
    kern = functools.partial(_moba_prompt_kernel, tq=tq, ngrp=ngrp, grp=grp, hd=hd,
                             scale=float(hd) ** -0.5)
    grid_spec = pltpu.PrefetchScalarGridSpec(
        num_scalar_prefetch=2,
        grid=(nb_batch, len(pairs)),
        in_specs=[
            pl.BlockSpec((tq, w), lambda b, p, qt, kt: (b * nq + qt[p], 0)),
            pl.BlockSpec((MOBA_BLOCK, ngrp * hd), lambda b, p, qt, kt: (b * nb + kt[p], 0)),
            pl.BlockSpec((None, ngrp * hd, MOBA_BLOCK), lambda b, p, qt, kt: (b, 0, kt[p])),
            pl.BlockSpec((nb, ngrp * hd), lambda b, p, qt, kt: (b, 0)),
        ],
        out_specs=pl.BlockSpec((tq, w), lambda b, p, qt, kt: (b * nq + qt[p], 0)),
        scratch_shapes=[pltpu.VMEM((ngrp, hd, rows), BF16),
                        pltpu.VMEM((ngrp, nb, rows), F32),
                        pltpu.VMEM((ngrp, 1, rows), F32),
                        pltpu.VMEM((ngrp, 1, rows), F32),
                        pltpu.VMEM((ngrp, hd, rows), F32)],
    )
    return pl.pallas_call(
        kern,
        grid_spec=grid_spec,
        out_shape=jax.ShapeDtypeStruct(q.shape, F32),
        compiler_params=_cparams(("parallel", "arbitrary")),
        name="moba_prompt",
    )(qi_tab, kj_tab, q, k, vt, means)


def _sample_partial_kernel(pt_ref, q_ref, *refs, npg, ngrp, grp, hd, scale):
    k_pages = refs[:npg]
    v_pages = refs[npg:2 * npg]
    mean_ref, m_ref, l_ref, po_ref, qs_scr = refs[2 * npg:]
    step = pl.program_id(1)
    t_new = q_ref.shape[0]
    gr = grp * t_new

    @pl.when(step == 0)
    def _():
        for g in range(ngrp):
            qg = jnp.concatenate(
                [q_ref[:, (g * grp + hh) * hd:(g * grp + hh + 1) * hd] for hh in range(grp)], axis=0)
            qs_scr[g] = (qg * scale).astype(BF16)
        mean_ref[...] = jnp.zeros(mean_ref.shape, F32)
        m_ref[...] = jnp.zeros(m_ref.shape, F32)
        l_ref[...] = jnp.zeros(l_ref.shape, F32)

    ppb = MOBA_BLOCK // k_pages[0].shape[-1]
    bps = npg // ppb
    lane = lax.broadcasted_iota(jnp.int32, m_ref.shape, 1)
    lane_hd = lax.broadcasted_iota(jnp.int32, (hd, LANES), 1)
    pair = []
    for blk in range(bps):
        sl = slice(blk * ppb, (blk + 1) * ppb)
        bidx = step * bps + blk
        scores = []
        for g in range(ngrp):
            kt = jnp.concatenate([r[g] for r in k_pages[sl]], axis=1)
            mean_col = jnp.sum(kt, axis=1, keepdims=True) * (1.0 / MOBA_BLOCK)
            rs = slice(g * hd, (g + 1) * hd)
            mean_ref[rs, :] = jnp.where(lane_hd == bidx, mean_col, mean_ref[rs, :])
            scores.append(_dot(qs_scr[g], kt.astype(BF16)))
        sc = jnp.concatenate(scores, axis=0)
        m = jnp.max(sc, axis=1, keepdims=True)
        pr = jnp.exp(sc - m)
        l = jnp.sum(pr, axis=1, keepdims=True)
        prb = pr.astype(BF16)
        outs = []
        for g in range(ngrp):
            vt = jnp.concatenate([r[g] for r in v_pages[sl]], axis=1).astype(BF16)
            outs.append(_dot_nt(prb[g * gr:(g + 1) * gr, :], vt))
        m_ref[...] = jnp.where(lane == bidx, m, m_ref[...])
        l_ref[...] = jnp.where(lane == bidx, l, l_ref[...])
        pair.append(jnp.concatenate(outs, axis=0))
        if len(pair) == LANES // hd:
            po_ref[blk // len(pair)] = jnp.concatenate(pair, axis=1)
            pair = []


def _sample_partials(q, cache_k, cache_v, page_table, *, npg, ngrp, grp, hd):
    db, n_pages = page_table.shape
    page = cache_k.shape[-1]
    kvw = ngrp * hd
    t_new = q.shape[0] // db
    ppb = MOBA_BLOCK // page
    bps = npg // ppb
    ppl = LANES // hd
    nsteps = n_pages // npg
    nblk = n_pages // ppb
    nrow = ngrp * grp * t_new
    assert nblk <= LANES and bps % ppl == 0
    kern = functools.partial(_sample_partial_kernel, npg=npg, ngrp=ngrp, grp=grp, hd=hd,
                             scale=float(hd) ** -0.5)

    def page_spec(i):
        return pl.BlockSpec((None, ngrp, hd, page),
                            lambda b, s, pt: (pt[b * n_pages + s * npg + i], 0, 0, 0))

    grid_spec = pltpu.PrefetchScalarGridSpec(
        num_scalar_prefetch=1,
        grid=(db, nsteps),
        in_specs=[pl.BlockSpec((t_new, q.shape[1]), lambda b, s, pt: (b, 0))]
        + [page_spec(i) for i in range(npg)] + [page_spec(i) for i in range(npg)],
        out_specs=[pl.BlockSpec((None, kvw, LANES), lambda b, s, pt: (b, 0, 0)),
                   pl.BlockSpec((None, nrow, LANES), lambda b, s, pt: (b, 0, 0)),
                   pl.BlockSpec((None, nrow, LANES), lambda b, s, pt: (b, 0, 0)),
                   pl.BlockSpec((None, bps // ppl, nrow, LANES), lambda b, s, pt: (b, s, 0, 0))],
        scratch_shapes=[pltpu.VMEM((ngrp, grp * t_new, hd), BF16)],
    )
    return pl.pallas_call(
        kern,
        grid_spec=grid_spec,
        out_shape=[jax.ShapeDtypeStruct((db, kvw, LANES), F32),
                   jax.ShapeDtypeStruct((db, nrow, LANES), F32),
                   jax.ShapeDtypeStruct((db, nrow, LANES), F32),
                   jax.ShapeDtypeStruct((db, nblk // ppl, nrow, LANES), F32)],
        compiler_params=_cparams(("parallel", "arbitrary")),
        name="moba_sample_partials",
    )(page_table.reshape(-1), q, *([cache_k] * npg), *([cache_v] * npg))


def _sample_combine_kernel(q_ref, kn_ref, vn_ref, mean_ref, m_ref, l_ref, po_ref, o_ref,
                           *, nblk, ngrp, grp, hd, scale):
    t_new = q_ref.shape[0]
    gr = grp * t_new
    nrow = ngrp * gr
    gates, qss = [], []
    for g in range(ngrp):
        qg = jnp.concatenate(
            [q_ref[:, (g * grp + hh) * hd:(g * grp + hh + 1) * hd] for hh in range(grp)], axis=0)
        qss.append((qg * scale).astype(BF16))
        gates.append(_dot(qg, mean_ref[g * hd:(g + 1) * hd, :], HI))
    sel = _top_select(jnp.concatenate(gates, axis=0), nblk, 1) > 0.5
    m_all = m_ref[...]
    m_past = jnp.max(jnp.where(sel, m_all, NEG_INF), axis=1, keepdims=True)
    kpos = lax.broadcasted_iota(jnp.int32, (nrow, t_new), 1)
    qpos = jnp.bitwise_and(lax.broadcasted_iota(jnp.int32, (nrow, t_new), 0), t_new - 1)
    kn = kn_ref[...].astype(BF16)
    vn = vn_ref[...].astype(BF16)
    s_own = jnp.concatenate(
        [_dot_nt(qss[g], kn[:, g * hd:(g + 1) * hd]) for g in range(ngrp)], axis=0)
    s_own = jnp.where(kpos <= qpos, s_own, NEG_INF)
    m_fin = jnp.maximum(m_past, jnp.max(s_own, axis=1, keepdims=True))
    w = jnp.where(sel, jnp.exp(m_all - m_fin), 0.0)
    p_own = jnp.exp(s_own - m_fin)
    l_fin = (jnp.sum(w * l_ref[...], axis=1, keepdims=True)
             + jnp.sum(p_own, axis=1, keepdims=True))
    pb = p_own.astype(BF16)
    o_own = jnp.concatenate(
        [_dot(pb[g * gr:(g + 1) * gr, :], vn[:, g * hd:(g + 1) * hd]) for g in range(ngrp)], axis=0)
    ppl = LANES // hd
    lane = lax.broadcasted_iota(jnp.int32, (nrow, LANES), 1)
    acc = jnp.zeros((nrow, LANES), F32)
    for j in range(nblk // ppl):
        wj = w[:, j * ppl:j * ppl + 1]
        for i in range(1, ppl):
            wj = jnp.where(lane < i * hd, wj, w[:, j * ppl + i:j * ppl + i + 1])
        acc = acc + wj * po_ref[j]
    o = o_own
    for i in range(ppl):
        o = o + acc[:, i * hd:(i + 1) * hd]
    o = o / l_fin
    o_ref[...] = jnp.concatenate(
        [o[r * t_new:(r + 1) * t_new, :] for r in range(ngrp * grp)], axis=1)


def _sample_combine(q, k_new, v_new, means_t, m_part, l_part, po, *, db, nblk, ngrp, grp, hd):
    t_new = q.shape[0] // db
    nrow = ngrp * grp * t_new
    kvw = ngrp * hd
    kern = functools.partial(_sample_combine_kernel, nblk=nblk, ngrp=ngrp, grp=grp, hd=hd,
                             scale=float(hd) ** -0.5)
    return pl.pallas_call(
        kern,
        grid=(db,),
        in_specs=[pl.BlockSpec((t_new, q.shape[1]), lambda b: (b, 0)),
                  pl.BlockSpec((t_new, kvw), lambda b: (b, 0)),
                  pl.BlockSpec((t_new, kvw), lambda b: (b, 0)),
                  pl.BlockSpec((None, kvw, LANES), lambda b: (b, 0, 0)),
                  pl.BlockSpec((None, nrow, LANES), lambda b: (b, 0, 0)),
                  pl.BlockSpec((None, nrow, LANES), lambda b: (b, 0, 0)),
                  pl.BlockSpec((None,) + po.shape[1:], lambda b: (b, 0, 0, 0))],
        out_specs=pl.BlockSpec((t_new, q.shape[1]), lambda b: (b, 0)),
        out_shape=jax.ShapeDtypeStruct(q.shape, F32),
        compiler_params=_cparams(("parallel",)),
        name="moba_sample_combine",
    )(q, k_new, v_new, means_t, m_part, l_part, po)


def _rope_tables(pos, hd, reps):
    half = hd // 2
    inv_freq = ROPE_THETA ** (-jnp.arange(half, dtype=F32) / half)
    ang = pos.astype(F32)[:, None] * inv_freq[None, :]
    cos = jnp.cos(ang)
    sin = jnp.sin(ang)
    per_vreg = LANES // hd
    cos_t = jnp.tile(jnp.concatenate([cos, cos], axis=1), (reps, per_vreg))
    sin_t = jnp.tile(jnp.concatenate([-sin, sin], axis=1), (reps, per_vreg))
    return cos_t, sin_t


def _trunk(x, pos, conv_state8, delta_state, attend, wts, *, nseq, t_len, tm, prep_tt,
           delta_ct, delta_c, dims, want_vt=False):
    nh, dk, conv_dim, n_a, depth, alpha, bh, kvh, hd = dims
    new_conv, new_delta = [], []
    k_sh = v_sh = vt_sh = None
    for layer in range(depth):
        if layer < n_a:
            proj = _matmul(x, wts["a_w_in"][layer], tm, wts["a_w_in"][layer].shape[1] // 3, "in_proj")
            qkv, gb = _gdn_prep(proj, conv_state8[layer], wts["a_conv"][layer], wts["a_log"][layer],
                                wts["a_dtb"][layer], nseq=nseq, t_len=t_len, tt=prep_tt,
                                conv_dim=conv_dim, nh=nh, dk=dk)
            o, s_new = _delta_rule(qkv, gb, delta_state[layer], nseq=nseq, t_len=t_len,
                                   ct=delta_ct, c=delta_c, nh=nh, dk=dk)
            cw = wts["a_conv"][layer].shape[0]
            new_conv.append(proj.reshape(nseq, t_len, -1)[:, t_len - (cw - 1):, :conv_dim])
            new_delta.append(s_new)
            h = _out_proj(o, x, wts["a_w_out"][layer], wts["ln_g"][layer][0], wts["ln_b"][layer][0],
                          tm=tm, alpha=alpha, z_src=proj, z_blk=conv_dim // (nh * dk),
                          norm_w=wts["a_norm"][layer], nh=nh, dv=dk)
        else:
            j = layer - n_a
            w_all = wts["b_w_qkv"][j]
            cos_t, sin_t = _rope_tables(pos, hd, (tm // t_len) if tm > t_len else 1)
            q, k_l, v_l, *vt_l = _qkv_rope(x, w_all, cos_t, sin_t, tm=tm, nq=bh * hd, nk=kvh * hd,
                                           half=hd // 2, vt_seq=t_len if want_vt else None)
            if layer == n_a:
                k_sh, v_sh, vt_sh = k_l, v_l, vt_l
            att = attend(q, k_sh, v_sh, *vt_sh)
            h = _out_proj(att, x, wts["b_w_o"][j], wts["ln_g"][layer][0], wts["ln_b"][layer][0],
                          tm=tm, alpha=alpha)
        x = _ffn(h, wts["ffn_wg"][layer], wts["ffn_wu"][layer], wts["ffn_wd"][layer],
                 wts["ln_g"][layer][1], wts["ln_b"][layer][1], tm=tm, tf=256, alpha=alpha)
    return x, jnp.stack(new_conv), jnp.stack(new_delta), k_sh, v_sh


def kernel(x_prompt, x_sample, state_conv, state_delta, cache_k, cache_v, page_table, ln_g, ln_b,
           a_w_in, a_conv, a_log_decay, a_dt_bias, a_norm, a_w_out, kv_w_k, kv_w_v, b_w_q, b_w_o,
           ffn_w_gate, ffn_w_up, ffn_w_down):
    bp, tp, d = x_prompt.shape
    db, ts, _ = x_sample.shape
    depth = ln_g.shape[0]
    n_a = a_w_in.shape[0]
    nh = a_log_decay.shape[1]
    dk = state_delta.shape[-2]
    conv_dim = a_conv.shape[-1]
    cw = a_conv.shape[1]
    n_pool, page, kvh, hd = cache_k.shape
    bh = b_w_q.shape[-1] // hd
    grp = bh // kvh
    n_pages = page_table.shape[1]
    past = n_pages * page
    alpha = (2.0 * depth) ** 0.25
    assert past % MOBA_BLOCK == 0 and ts <= MOBA_BLOCK and ts >= cw - 1 and tp >= cw - 1
    assert state_delta.shape[-1] == dk and conv_dim == 3 * nh * dk

    in_w = a_w_in.shape[-1]
    in_pad = -(-in_w // (3 * LANES)) * (3 * LANES)
    pad_lanes = lambda v: jnp.pad(v, ((0, 0), (0, LANES - v.shape[-1])))[:, None, :]
    wts = {
        "a_w_in": jnp.pad(a_w_in, ((0, 0), (0, 0), (0, in_pad - in_w))).astype(BF16),
        "a_conv": a_conv,
        "a_log": pad_lanes(a_log_decay),
        "a_dtb": pad_lanes(a_dt_bias),
        "a_norm": a_norm[:, None, :],
        "a_w_out": a_w_out.astype(BF16),
        "b_w_qkv": jnp.concatenate(
            [b_w_q, jnp.broadcast_to(kv_w_k, (b_w_q.shape[0],) + kv_w_k.shape),
             jnp.broadcast_to(kv_w_v, (b_w_q.shape[0],) + kv_w_v.shape)], axis=-1).astype(BF16),
        "b_w_o": b_w_o.astype(BF16),
        "ffn_wg": ffn_w_gate.astype(BF16),
        "ffn_wu": ffn_w_up.astype(BF16),
        "ffn_wd": ffn_w_down.astype(BF16),
        "ln_g": ln_g[:, :, None, :],
        "ln_b": ln_b[:, :, None, :],
    }
    dims = (nh, dk, conv_dim, n_a, depth, alpha, bh, kvh, hd)

    nb = tp // MOBA_BLOCK

    def attend_prompt(q, k, v, vt):
        total = bp * nb
        step = SUBLANES if total % SUBLANES == 0 else total
        means = _block_means(k, nblk_step=step)
        return _moba_prompt(q, k, vt, means, nb_batch=bp, t_len=tp, ngrp=kvh, grp=grp, hd=hd)

    tm_p = 512 if (bp * tp) % 512 == 0 else 256
    y_p, p_conv, p_delta, p_k, p_v = _trunk(
        x_prompt.reshape(bp * tp, d), jnp.arange(tp, dtype=jnp.int32),
        jnp.zeros((n_a, bp, SUBLANES, conv_dim), F32), jnp.zeros((n_a, bp, nh, dk, dk), F32),
        attend_prompt, wts, nseq=bp, t_len=tp, tm=tm_p, prep_tt=256, delta_ct=256,
        delta_c=DELTA_CHUNK, dims=dims, want_vt=True)

    ck = jnp.transpose(cache_k, (0, 2, 3, 1))
    cv = jnp.transpose(cache_v, (0, 2, 3, 1))
    npg = 8 if n_pages % 8 == 0 else 2 * (MOBA_BLOCK // page)

    def attend_sample(q, k_new, v_new):
        means_t, m_part, l_part, po = _sample_partials(q, ck, cv, page_table, npg=npg, ngrp=kvh,
                                                       grp=grp, hd=hd)
        return _sample_combine(q, k_new, v_new, means_t, m_part, l_part, po, db=db,
                               nblk=past // MOBA_BLOCK, ngrp=kvh, grp=grp, hd=hd)

    conv8 = jnp.pad(state_conv, ((0, 0), (0, 0), (SUBLANES - (cw - 1), 0), (0, 0)))
    y_s, s_conv, s_delta, s_k, s_v = _trunk(
        x_sample.reshape(db * ts, d), past + jnp.arange(ts, dtype=jnp.int32), conv8, state_delta,
        attend_sample, wts, nseq=db, t_len=ts, tm=db * ts, prep_tt=ts, delta_ct=ts, delta_c=ts,
        dims=dims)

    return (y_p.reshape(bp, tp, d), y_s.reshape(db, ts, d), p_conv, p_delta,
            p_k.reshape(bp, tp, kvh, hd), p_v.reshape(bp, tp, kvh, hd), s_conv, s_delta,
            s_k.reshape(db, ts, kvh, hd), s_v.reshape(db, ts, kvh, hd))
```

```python
import functools

import numpy as np
import jax
import jax.numpy as jnp
from jax import lax
from jax.experimental import pallas as pl
from jax.experimental.pallas import tpu as pltpu

F32 = jnp.float32
BF16 = jnp.bfloat16
HI = lax.Precision.HIGHEST

LANES = 128
SUBLANES = 8
VMEM_LIMIT_MB = 56

MOBA_BLOCK = 256
MOBA_TOPK = 3
CHUNK_BATCH = 32
DELTA_CHUNK = 64
ROPE_THETA = 10000.0
LN_EPS = 1e-5
RMS_EPS = 1e-6
L2_EPS = 1e-6
NEG_INF = float("-inf")
LOG2E = 1.4426950408889634


def _cparams(sem):
    return pltpu.CompilerParams(dimension_semantics=sem,
                                vmem_limit_bytes=VMEM_LIMIT_MB * 1024 * 1024)


def _dot(a, b, prec=None):
    return jnp.dot(a, b, precision=prec, preferred_element_type=F32)


def _dot_nt(a, b, prec=None):
    return lax.dot_general(a, b, (((1,), (1,)), ((), ())), precision=prec,
                           preferred_element_type=F32)


def _dot_tn(a, b, prec=None):
    return lax.dot_general(a, b, (((0,), (0,)), ((), ())), precision=prec,
                           preferred_element_type=F32)


def _split(a):
    hi = a.astype(BF16)
    return hi, (a - hi.astype(F32)).astype(BF16)


def _dot3(a, b):
    return _dot(a[0], b[0]) + (_dot(a[0], b[1]) + _dot(a[1], b[0]))


def _sigmoid(x):
    return 1.0 / (1.0 + jnp.exp(-x))


def _layer_norm(v, g, b):
    mu = jnp.mean(v, axis=-1, keepdims=True)
    d = v - mu
    var = jnp.mean(d * d, axis=-1, keepdims=True)
    return d * lax.rsqrt(var + LN_EPS) * g + b


def _mm_kernel(x_ref, w_ref, o_ref):
    o_ref[...] = _dot(x_ref[...].astype(BF16), w_ref[...])


def _matmul(x, w, tm, tn, name):
    r, k = x.shape
    n = w.shape[1]
    return pl.pallas_call(
        _mm_kernel,
        grid=(r // tm, n // tn),
        in_specs=[pl.BlockSpec((tm, k), lambda i, j: (i, 0)),
                  pl.BlockSpec((k, tn), lambda i, j: (0, j))],
        out_specs=pl.BlockSpec((tm, tn), lambda i, j: (i, j)),
        out_shape=jax.ShapeDtypeStruct((r, n), F32),
        compiler_params=_cparams(("parallel", "parallel")),
        name=name,
    )(x, w)


def _prep_kernel(cur_ref, prev_ref, st_ref, ab_ref, cw_ref, alog_ref, dtb_ref,
                 qkv_ref, gb_ref, xp_ref, *, tt, conv_w, n_q, n_qk, nh, q_scale):
    t = pl.program_id(1)
    hist = SUBLANES

    @pl.when(t == 0)
    def _():
        xp_ref[0:hist, :] = st_ref[...]

    @pl.when(t != 0)
    def _():
        xp_ref[0:hist, :] = prev_ref[...]

    xp_ref[hist:hist + tt, :] = cur_ref[...]
    base = hist - (conv_w - 1)
    for c in range(cur_ref.shape[1] // LANES):
        cs = slice(c * LANES, (c + 1) * LANES)
        y = xp_ref[base:base + tt, cs] * cw_ref[0:1, cs]
        for i in range(1, conv_w):
            y = y + xp_ref[base + i:base + i + tt, cs] * cw_ref[i:i + 1, cs]
        y = y * _sigmoid(y)
        if c < n_qk:
            y = y * lax.rsqrt(jnp.sum(y * y, axis=-1, keepdims=True) + L2_EPS)
            if c < n_q:
                y = y * q_scale
        qkv_ref[:, cs] = y
    ab = ab_ref[...]
    lane = lax.broadcasted_iota(jnp.int32, ab.shape, 1)
    sp = ab + dtb_ref[...]
    softplus = jnp.maximum(sp, 0.0) + jnp.log1p(jnp.exp(-jnp.abs(sp)))
    g = -jnp.exp(alog_ref[...]) * softplus
    gb_ref[...] = jnp.where(lane < nh, g, _sigmoid(ab))


def _gdn_prep(proj, state8, conv_w, alog, dtb, *, nseq, t_len, tt, conv_dim, nh, dk):
    rows = nseq * t_len
    nt = t_len // tt
    cw = conv_w.shape[0]
    ab_blk = (conv_dim + nh * dk) // LANES
    kern = functools.partial(_prep_kernel, tt=tt, conv_w=cw, n_q=nh, n_qk=2 * nh, nh=nh,
                             q_scale=float(dk) ** -0.5)
    tpb = tt // SUBLANES
    return pl.pallas_call(
        kern,
        grid=(nseq, nt),
        in_specs=[
            pl.BlockSpec((tt, conv_dim), lambda s, t: (s * nt + t, 0)),
            pl.BlockSpec((SUBLANES, conv_dim),
                         lambda s, t: (jnp.maximum((s * nt + t) * tpb - 1, 0), 0)),
            pl.BlockSpec((None, SUBLANES, conv_dim), lambda s, t: (s, 0, 0)),
            pl.BlockSpec((tt, LANES), lambda s, t: (s * nt + t, ab_blk)),
            pl.BlockSpec((cw, conv_dim), lambda s, t: (0, 0)),
            pl.BlockSpec((1, LANES), lambda s, t: (0, 0)),
            pl.BlockSpec((1, LANES), lambda s, t: (0, 0)),
        ],
        out_specs=[pl.BlockSpec((tt, conv_dim), lambda s, t: (s * nt + t, 0)),
                   pl.BlockSpec((tt, LANES), lambda s, t: (s * nt + t, 0))],
        out_shape=[jax.ShapeDtypeStruct((rows, conv_dim), F32),
                   jax.ShapeDtypeStruct((rows, LANES), F32)],
        scratch_shapes=[pltpu.VMEM((tt + SUBLANES, conv_dim), F32)],
        compiler_params=_cparams(("parallel", "arbitrary")),
        name="gdn_prep",
    )(proj, proj, state8, proj, conv_w, alog, dtb)


def _delta_kernel(q_ref, k_ref, v_ref, gb_ref, s0_ref, o_ref, s_ref, st_scr, *, ct, c, nh, dk):
    t = pl.program_id(1)

    @pl.when(t == 0)
    def _():
        st_scr[...] = s0_ref[...]

    row = lax.broadcasted_iota(jnp.int32, (c, c), 0)
    col = lax.broadcasted_iota(jnp.int32, (c, c), 1)
    causal = row >= col
    strict = row > col
    tril = causal.astype(F32)
    eye = (row == col).astype(F32)
    n_sq = max(int(np.ceil(np.log2(c))) - 1, 0)

    def chunk(r0):
        rs = pl.ds(r0, c)
        gbc = gb_ref[rs, :]
        gc = _dot(tril, gbc, HI)
        gt = gc.T
        eg = jnp.exp(gc)
        heads = range(nh)
        qs, ks, qkm, xs, rhs, gcols, glasts = [], [], [], [], [], [], []
        for h in heads:
            cs = slice(h * dk, (h + 1) * dk)
            qh = q_ref[rs, cs]
            kh = k_ref[rs, cs]
            vh = v_ref[rs, cs]
            gcol = gc[:, h:h + 1]
            grow = gt[h:h + 1, :]
            bcol = gbc[:, nh + h:nh + h + 1]
            egcol = eg[:, h:h + 1]
            decay = jnp.where(causal, jnp.exp(jnp.where(causal, gcol - grow, 0.0)), 0.0)
            kb = kh.astype(BF16)
            kk = _dot_nt(kb, kb)
            qk = _dot_nt(qh.astype(BF16), kb)
            xs.append(-jnp.where(strict, bcol * kk * decay, 0.0))
            qkm.append(jnp.where(causal, qk * decay, 0.0).astype(BF16))
            rhs.append(jnp.concatenate([vh * bcol, kh * (bcol * egcol)], axis=-1))
            qs.append((qh * egcol).astype(BF16))
            ks.append(kh)
            gcols.append(gcol)
            glasts.append(grow[:, c - 1:c])
        tms = [eye + x for x in xs]
        ps = xs
        for _ in range(n_sq):
            sp = [_split(p) for p in ps]
            ps = [_dot3(s, s) for s in sp]
            sp = [_split(p) for p in ps]
            tms = [tm + _dot3(_split(tm), s) for tm, s in zip(tms, sp)]
        sols = [_dot3(_split(tm), _split(r)) for tm, r in zip(tms, rhs)]
        for h in heads:
            cs = slice(h * dk, (h + 1) * dk)
            u = sols[h][:, :dk]
            w = sols[h][:, dk:]
            s = st_scr[h]
            sb = s.astype(BF16)
            v_new = u - _dot(w.astype(BF16), sb)
            vb = v_new.astype(BF16)
            o_ref[rs, cs] = _dot(qs[h], sb) + _dot(qkm[h], vb)
            kd = (ks[h] * jnp.exp(glasts[h] - gcols[h])).astype(BF16)
            st_scr[h] = s * jnp.exp(glasts[h]) + _dot_tn(kd, vb)

    n_chunks = ct // c
    if n_chunks == 1:
        chunk(0)
    else:
        def body(ci, carry):
            chunk(pl.multiple_of(ci * c, c))
            return carry
        lax.fori_loop(0, n_chunks, body, 0)

    @pl.when(t == pl.num_programs(1) - 1)
    def _():
        s_ref[...] = st_scr[...]


def _delta_rule(qkv, gb, s0, *, nseq, t_len, ct, c, nh, dk):
    rows = nseq * t_len
    nt = t_len // ct
    w = nh * dk
    kern = functools.partial(_delta_kernel, ct=ct, c=c, nh=nh, dk=dk)
    return pl.pallas_call(
        kern,
        grid=(nseq, nt),
        in_specs=[
            pl.BlockSpec((ct, w), lambda s, t: (s * nt + t, 0)),
            pl.BlockSpec((ct, w), lambda s, t: (s * nt + t, 1)),
            pl.BlockSpec((ct, w), lambda s, t: (s * nt + t, 2)),
            pl.BlockSpec((ct, LANES), lambda s, t: (s * nt + t, 0)),
            pl.BlockSpec((None, nh, dk, dk), lambda s, t: (s, 0, 0, 0)),
        ],
        out_specs=[pl.BlockSpec((ct, w), lambda s, t: (s * nt + t, 0)),
                   pl.BlockSpec((None, nh, dk, dk), lambda s, t: (s, 0, 0, 0))],
        out_shape=[jax.ShapeDtypeStruct((rows, w), F32),
                   jax.ShapeDtypeStruct((nseq, nh, dk, dk), F32)],
        scratch_shapes=[pltpu.VMEM((nh, dk, dk), F32)],
        compiler_params=_cparams(("parallel", "arbitrary")),
        name="delta_rule",
    )(qkv, qkv, qkv, gb, s0)


def _outproj_kernel(*refs, gated, nh, dv, alpha):
    if gated:
        o_ref, z_ref, nw_ref, x_ref, w_ref, g_ref, b_ref, out_ref, a_scr = refs
        for h in range(nh):
            cs = slice(h * dv, (h + 1) * dv)
            oh = o_ref[:, cs]
            zh = z_ref[:, cs]
            oh = oh * lax.rsqrt(jnp.mean(oh * oh, axis=-1, keepdims=True) + RMS_EPS) * nw_ref[...]
            a_scr[:, cs] = (oh * (zh * _sigmoid(zh))).astype(BF16)
        a = a_scr[...]
    else:
        o_ref, x_ref, w_ref, g_ref, b_ref, out_ref = refs
        a = o_ref[...].astype(BF16)
    v = alpha * x_ref[...] + _dot(a, w_ref[...])
    out_ref[...] = _layer_norm(v, g_ref[...], b_ref[...])


def _out_proj(o, x, w, g, b, *, tm, alpha, z_src=None, z_blk=0, norm_w=None, nh=1, dv=1):
    rows, d_in = o.shape
    d = x.shape[1]
    gated = z_src is not None
    kern = functools.partial(_outproj_kernel, gated=gated, nh=nh, dv=dv, alpha=alpha)
    row_spec = lambda width: pl.BlockSpec((tm, width), lambda i: (i, 0))
    const = lambda shape: pl.BlockSpec(shape, lambda i: (0, 0))
    in_specs = [row_spec(d_in)]
    args = [o]
    scratch = []
    if gated:
        in_specs += [pl.BlockSpec((tm, d_in), lambda i: (i, z_blk)), const((1, dv))]
        args += [z_src, norm_w]
        scratch = [pltpu.VMEM((tm, d_in), BF16)]
    in_specs += [row_spec(d), const((d_in, d)), const((1, d)), const((1, d))]
    args += [x, w, g, b]
    return pl.pallas_call(
        kern,
        grid=(rows // tm,),
        in_specs=in_specs,
        out_specs=row_spec(d),
        out_shape=jax.ShapeDtypeStruct((rows, d), F32),
        scratch_shapes=scratch,
        compiler_params=_cparams(("parallel",)),
        name="out_proj_ln",
    )(*args)


def _ffn_kernel(h_ref, wg_ref, wu_ref, wd_ref, g_ref, b_ref, out_ref, acc_ref, *, tf, alpha):
    hb = h_ref[...].astype(BF16)
    for c in range(wg_ref.shape[1] // tf):
        fs = slice(c * tf, (c + 1) * tf)
        gate = _dot(hb, wg_ref[:, fs])
        up = _dot(hb, wu_ref[:, fs])
        act = (gate * _sigmoid(gate) * up).astype(BF16)
        down = _dot(act, wd_ref[fs, :])
        if c == 0:
            acc_ref[...] = down
        else:
            acc_ref[...] += down
    v = alpha * h_ref[...] + acc_ref[...]
    out_ref[...] = _layer_norm(v, g_ref[...], b_ref[...])


def _ffn(h, wg, wu, wd, g, b, *, tm, tf, alpha):
    rows, d = h.shape
    f = wg.shape[1]
    kern = functools.partial(_ffn_kernel, tf=tf, alpha=alpha)
    resident = lambda shape: pl.BlockSpec(shape, lambda i: (0, 0), pipeline_mode=pl.Buffered(1))
    return pl.pallas_call(
        kern,
        grid=(rows // tm,),
        in_specs=[pl.BlockSpec((tm, d), lambda i: (i, 0)),
                  resident((d, f)), resident((d, f)), resident((f, d)),
                  pl.BlockSpec((1, d), lambda i: (0, 0)),
                  pl.BlockSpec((1, d), lambda i: (0, 0))],
        out_specs=pl.BlockSpec((tm, d), lambda i: (i, 0)),
        out_shape=jax.ShapeDtypeStruct((rows, d), F32),
        scratch_shapes=[pltpu.VMEM((tm, d), F32)],
        compiler_params=_cparams(("parallel",)),
        name="ffn_ln",
    )(h, wg, wu, wd, g, b)


def _qkv_kernel(x_ref, w_ref, cos_ref, sin_ref, q_ref, k_ref, v_ref, *t_refs, nq, nk, half):
    y = _dot(x_ref[...].astype(BF16), w_ref[...])
    cos = cos_ref[...]
    sin = sin_ref[...]
    lane = lax.broadcasted_iota(jnp.int32, cos.shape, 1)
    first = jnp.bitwise_and(lane, 2 * half - 1) < half

    def rope(xg):
        partner = jnp.where(first, pltpu.roll(xg, LANES - half, 1), pltpu.roll(xg, half, 1))
        return xg * cos + partner * sin

    for c in range(nq // LANES):
        q_ref[:, c * LANES:(c + 1) * LANES] = rope(y[:, c * LANES:(c + 1) * LANES])
    k = jnp.concatenate([rope(y[:, nq + c * LANES:nq + (c + 1) * LANES])
                         for c in range(nk // LANES)], axis=1)
    k_ref[...] = k
    v_ref[...] = y[:, nq + nk:]
    if t_refs:
        kt_ref, vt_ref = t_refs
        kt_ref[...] = k.T
        vt_ref[...] = y[:, nq + nk:].T


def _qkv_rope(x, w, cos, sin, *, tm, nq, nk, half, t_seq=None):
    rows, d = x.shape
    n_tab = cos.shape[0] // tm
    kern = functools.partial(_qkv_kernel, nq=nq, nk=nk, half=half)
    out_specs = [pl.BlockSpec((tm, nq), lambda i: (i, 0)),
                 pl.BlockSpec((tm, nk), lambda i: (i, 0)),
                 pl.BlockSpec((tm, nk), lambda i: (i, 0))]
    out_shape = [jax.ShapeDtypeStruct((rows, nq), F32),
                 jax.ShapeDtypeStruct((rows, nk), F32),
                 jax.ShapeDtypeStruct((rows, nk), F32)]
    if t_seq is not None:
        nt = t_seq // tm
        out_specs += [pl.BlockSpec((None, nk, tm), lambda i: (i // nt, 0, i % nt))] * 2
        out_shape += [jax.ShapeDtypeStruct((rows // t_seq, nk, t_seq), F32)] * 2
    return pl.pallas_call(
        kern,
        grid=(rows // tm,),
        in_specs=[pl.BlockSpec((tm, d), lambda i: (i, 0)),
                  pl.BlockSpec((d, nq + 2 * nk), lambda i: (0, 0)),
                  pl.BlockSpec((tm, LANES), lambda i: (i % n_tab, 0)),
                  pl.BlockSpec((tm, LANES), lambda i: (i % n_tab, 0))],
        out_specs=out_specs,
        out_shape=out_shape,
        compiler_params=_cparams(("parallel",)),
        name="qkv_rope",
    )(x, w, cos, sin)


def _kmeans_kernel(k_ref, m_ref, *, nblk):
    kb = k_ref[...].reshape(nblk, MOBA_BLOCK, k_ref.shape[1])
    m_ref[...] = jnp.sum(kb, axis=1) * (1.0 / MOBA_BLOCK)


def _block_means(k, *, nblk_step):
    rows, w = k.shape
    total = rows // MOBA_BLOCK
    return pl.pallas_call(
        functools.partial(_kmeans_kernel, nblk=nblk_step),
        grid=(total // nblk_step,),
        in_specs=[pl.BlockSpec((nblk_step * MOBA_BLOCK, w), lambda i: (i, 0))],
        out_specs=pl.BlockSpec((nblk_step, w), lambda i: (i, 0)),
        out_shape=jax.ShapeDtypeStruct((total, w), F32),
        compiler_params=_cparams(("parallel",)),
        name="block_means",
    )(k)


def _top_select(gate, n_past, axis):
    n = gate.shape[axis]
    idx = lax.broadcasted_iota(jnp.int32, gate.shape, axis).astype(F32)
    gm = jnp.where(idx < n_past, gate, NEG_INF)
    sel = jnp.zeros(gate.shape, F32)
    for r in range(MOBA_TOPK):
        mx = jnp.max(gm, axis=axis, keepdims=True)
        first = jnp.min(jnp.where(gm == mx, idx, float(n)), axis=axis, keepdims=True)
        pick = idx == first
        keep = jnp.where(r < n_past, 1.0, 0.0).astype(F32)
        sel = jnp.maximum(sel, jnp.where(pick, keep, 0.0))
        gm = jnp.where(pick, NEG_INF, gm)
    return sel


def _moba_prompt_kernel(qi_ref, kj_ref, q_ref, k_ref, vt_ref, mean_ref, o_ref,
                        qs_scr, sel_scr, m_scr, l_scr, acc_scr, *, tq, ngrp, grp, hd, scale):
    p = pl.program_id(1)
    qi = qi_ref[p]
    kj = kj_ref[p]
    rows = grp * tq

    @pl.when(kj == 0)
    def _():
        qt = q_ref[...].T
        for g in range(ngrp):
            qg = jnp.concatenate(
                [qt[(g * grp + hh) * hd:(g * grp + hh + 1) * hd, :] for hh in range(grp)], axis=1)
            qs_scr[g] = (qg * (scale * LOG2E)).astype(BF16)
            gate = _dot(mean_ref[:, g * hd:(g + 1) * hd], qg, HI)
            sel_scr[g] = _top_select(gate, qi, 0)
        m_scr[...] = jnp.full(m_scr.shape, NEG_INF, F32)
        l_scr[...] = jnp.zeros(l_scr.shape, F32)
        acc_scr[...] = jnp.zeros(acc_scr.shape, F32)

    ones_rows = jnp.ones((2 * SUBLANES, MOBA_BLOCK), BF16)

    def attend(diagonal):
        kgs = [k_ref[:, g * hd:(g + 1) * hd].astype(BF16) for g in range(ngrp)]
        vgs = [jnp.concatenate([vt_ref[g * hd:(g + 1) * hd, :].astype(BF16), ones_rows], axis=0)
               for g in range(ngrp)]
        sel_rows = None if diagonal else [sel_scr[g, pl.ds(kj, 1), :] for g in range(ngrp)]
        units = [(g, slice(c * LANES, (c + 1) * LANES))
                 for g in range(ngrp) for c in range(rows // LANES)]
        for u0 in range(0, len(units), CHUNK_BATCH):
            attend_units(diagonal, kgs, vgs, sel_rows, units[u0:u0 + CHUNK_BATCH])

    def attend_units(diagonal, kgs, vgs, sel_rows, units):
        ss = [_dot(kgs[g], qs_scr[g, :, ls]) for g, ls in units]
        if diagonal:
            kpos = lax.broadcasted_iota(jnp.int32, ss[0].shape, 0)
            lane = lax.broadcasted_iota(jnp.int32, ss[0].shape, 1)
            ss = [jnp.where(kpos <= jnp.bitwise_and(lane + ls.start, tq - 1), s, NEG_INF)
                  for s, (_, ls) in zip(ss, units)]
        m_prevs = [m_scr[g, :, ls] for g, ls in units]
        m_news = [jnp.maximum(mp, jnp.max(s, axis=0, keepdims=True)) for mp, s in zip(m_prevs, ss)]
        ps = [jnp.exp2(s - mn).astype(BF16) for s, mn in zip(ss, m_news)]
        pvs = [_dot(vgs[g], p) for (g, _), p in zip(units, ps)]
        for (g, ls), mp, mn, pv in zip(units, m_prevs, m_news, pvs):
            alpha = jnp.exp2(mp - mn)
            acc = alpha * acc_scr[g, :, ls] + pv[:hd, :]
            l = alpha * l_scr[g, :, ls] + pv[hd:hd + 1, :]
            if diagonal:
                acc_scr[g, :, ls] = acc
                l_scr[g, :, ls] = l
            else:
                picked = sel_rows[g][:, ls] > 0.5
                acc_scr[g, :, ls] = jnp.where(picked, acc, acc_scr[g, :, ls])
                l_scr[g, :, ls] = jnp.where(picked, l, l_scr[g, :, ls])
                m_scr[g, :, ls] = jnp.where(picked, mn, mp)

    @pl.when(kj < qi)
    def _():
        attend(False)

    @pl.when(kj == qi)
    def _():
        attend(True)
        outs = []
        for g in range(ngrp):
            og = acc_scr[g] / l_scr[g]
            outs += [og[:, hh * tq:(hh + 1) * tq] for hh in range(grp)]
        o_ref[...] = jnp.concatenate(outs, axis=0).T


def _moba_prompt(q, k, vt, means, *, nb_batch, t_len, ngrp, grp, hd):
    tq = MOBA_BLOCK
    nq = t_len // tq
    nb = t_len // MOBA_BLOCK
    pairs = [(i, j) for i in range(nq) for j in range(i + 1)]
    qi_tab = jnp.asarray(np.array([pq for pq, _ in pairs], np.int32))
    kj_tab = jnp.asarray(np.array([pk for _, pk in pairs], np.int32))
    rows = grp * tq
    w = ngrp * grp * hd
    kern = functools.partial(_moba_prompt_kernel, tq=tq, ngrp=ngrp, grp=grp, hd=hd,
                             scale=float(hd) ** -0.5)
    grid_spec = pltpu.PrefetchScalarGridSpec(
        num_scalar_prefetch=2,
        grid=(nb_batch, len(pairs)),
        in_specs=[
            pl.BlockSpec((tq, w), lambda b, p, qt, kt: (b * nq + qt[p], 0)),
            pl.BlockSpec((MOBA_BLOCK, ngrp * hd), lambda b, p, qt, kt: (b * nb + kt[p], 0)),
            pl.BlockSpec((None, ngrp * hd, MOBA_BLOCK), lambda b, p, qt, kt: (b, 0, kt[p])),
            pl.BlockSpec((nb, ngrp * hd), lambda b, p, qt, kt: (b, 0)),
        ],
        out_specs=pl.BlockSpec((tq, w), lambda b, p, qt, kt: (b * nq + qt[p], 0)),
        scratch_shapes=[pltpu.VMEM((ngrp, hd, rows), BF16),
                        pltpu.VMEM((ngrp, nb, rows), F32),
                        pltpu.VMEM((ngrp, 1, rows), F32),
                        pltpu.VMEM((ngrp, 1, rows), F32),
                        pltpu.VMEM((ngrp, hd, rows), F32)],
    )
    return pl.pallas_call(
        kern,
        grid_spec=grid_spec,
        out_shape=jax.ShapeDtypeStruct(q.shape, F32),
        compiler_params=_cparams(("parallel", "arbitrary")),
        name="moba_prompt",
    )(qi_tab, kj_tab, q, k, vt, means)


def _sample_partial_kernel(pt_ref, q_ref, *refs, npg, ngrp, grp, hd, scale):
    k_pages = refs[:npg]
    v_pages = refs[npg:2 * npg]
    mean_ref, m_ref, l_ref, po_ref, qs_scr = refs[2 * npg:]
    step = pl.program_id(1)
    t_new = q_ref.shape[0]
    gr = grp * t_new

    @pl.when(step == 0)
    def _():
        for g in range(ngrp):
            qg = jnp.concatenate(
                [q_ref[:, (g * grp + hh) * hd:(g * grp + hh + 1) * hd] for hh in range(grp)], axis=0)
            qs_scr[g] = (qg * scale).astype(BF16)
        mean_ref[...] = jnp.zeros(mean_ref.shape, F32)
        m_ref[...] = jnp.zeros(m_ref.shape, F32)
        l_ref[...] = jnp.zeros(l_ref.shape, F32)

    ppb = MOBA_BLOCK // k_pages[0].shape[-1]
    bps = npg // ppb
    lane = lax.broadcasted_iota(jnp.int32, m_ref.shape, 1)
    lane_hd = lax.broadcasted_iota(jnp.int32, (hd, LANES), 1)
    blocks = [slice(blk * ppb, (blk + 1) * ppb) for blk in range(bps)]
    kts = [[jnp.concatenate([r[g] for r in k_pages[sl]], axis=1) for g in range(ngrp)]
           for sl in blocks]
    scs = [jnp.concatenate([_dot(qs_scr[g], kt[g].astype(BF16)) for g in range(ngrp)], axis=0)
           for kt in kts]
    ms = [jnp.max(sc, axis=1, keepdims=True) for sc in scs]
    prs = [jnp.exp(sc - m) for sc, m in zip(scs, ms)]
    ls = [jnp.sum(pr, axis=1, keepdims=True) for pr in prs]
    outs = []
    for sl, pr in zip(blocks, prs):
        prb = pr.astype(BF16)
        outs.append(jnp.concatenate(
            [_dot_nt(prb[g * gr:(g + 1) * gr, :],
                     jnp.concatenate([r[g] for r in v_pages[sl]], axis=1).astype(BF16))
             for g in range(ngrp)], axis=0))
    m_tile = m_ref[...]
    l_tile = l_ref[...]
    for blk in range(bps):
        bidx = step * bps + blk
        m_tile = jnp.where(lane == bidx, ms[blk], m_tile)
        l_tile = jnp.where(lane == bidx, ls[blk], l_tile)
        for g in range(ngrp):
            mean_col = jnp.sum(kts[blk][g], axis=1, keepdims=True) * (1.0 / MOBA_BLOCK)
            rs = slice(g * hd, (g + 1) * hd)
            mean_ref[rs, :] = jnp.where(lane_hd == bidx, mean_col, mean_ref[rs, :])
    m_ref[...] = m_tile
    l_ref[...] = l_tile
    ppl = LANES // hd
    for j in range(bps // ppl):
        po_ref[j] = jnp.concatenate(outs[j * ppl:(j + 1) * ppl], axis=1)


def _sample_partials(q, cache_k, cache_v, page_table, *, npg, ngrp, grp, hd):
    db, n_pages = page_table.shape
    page = cache_k.shape[-1]
    kvw = ngrp * hd
    t_new = q.shape[0] // db
    ppb = MOBA_BLOCK // page
    bps = npg // ppb
    ppl = LANES // hd
    nsteps = n_pages // npg
    nblk = n_pages // ppb
    nrow = ngrp * grp * t_new
    assert nblk <= LANES and bps % ppl == 0
    kern = functools.partial(_sample_partial_kernel, npg=npg, ngrp=ngrp, grp=grp, hd=hd,
                             scale=float(hd) ** -0.5)

    def page_spec(i):
        return pl.BlockSpec((None, ngrp, hd, page),
                            lambda b, s, pt: (pt[b * n_pages + s * npg + i], 0, 0, 0))

    grid_spec = pltpu.PrefetchScalarGridSpec(
        num_scalar_prefetch=1,
        grid=(db, nsteps),
        in_specs=[pl.BlockSpec((t_new, q.shape[1]), lambda b, s, pt: (b, 0))]
        + [page_spec(i) for i in range(npg)] + [page_spec(i) for i in range(npg)],
        out_specs=[pl.BlockSpec((None, kvw, LANES), lambda b, s, pt: (b, 0, 0)),
                   pl.BlockSpec((None, nrow, LANES), lambda b, s, pt: (b, 0, 0)),
                   pl.BlockSpec((None, nrow, LANES), lambda b, s, pt: (b, 0, 0)),
                   pl.BlockSpec((None, bps // ppl, nrow, LANES), lambda b, s, pt: (b, s, 0, 0))],
        scratch_shapes=[pltpu.VMEM((ngrp, grp * t_new, hd), BF16)],
    )
    return pl.pallas_call(
        kern,
        grid_spec=grid_spec,
        out_shape=[jax.ShapeDtypeStruct((db, kvw, LANES), F32),
                   jax.ShapeDtypeStruct((db, nrow, LANES), F32),
                   jax.ShapeDtypeStruct((db, nrow, LANES), F32),
                   jax.ShapeDtypeStruct((db, nblk // ppl, nrow, LANES), F32)],
        compiler_params=_cparams(("parallel", "arbitrary")),
        name="moba_sample_partials",
    )(page_table.reshape(-1), q, *([cache_k] * npg), *([cache_v] * npg))


def _sample_combine_kernel(q_ref, kn_ref, vn_ref, mean_ref, m_ref, l_ref, po_ref, o_ref,
                           *, nblk, ngrp, grp, hd, scale):
    t_new = q_ref.shape[0]
    gr = grp * t_new
    nrow = ngrp * gr
    gates, qss = [], []
    for g in range(ngrp):
        qg = jnp.concatenate(
            [q_ref[:, (g * grp + hh) * hd:(g * grp + hh + 1) * hd] for hh in range(grp)], axis=0)
        qss.append((qg * scale).astype(BF16))
        gates.append(_dot(qg, mean_ref[g * hd:(g + 1) * hd, :], HI))
    sel = _top_select(jnp.concatenate(gates, axis=0), nblk, 1) > 0.5
    m_all = m_ref[...]
    m_past = jnp.max(jnp.where(sel, m_all, NEG_INF), axis=1, keepdims=True)
    kpos = lax.broadcasted_iota(jnp.int32, (nrow, t_new), 1)
    qpos = jnp.bitwise_and(lax.broadcasted_iota(jnp.int32, (nrow, t_new), 0), t_new - 1)
    kn = kn_ref[...].astype(BF16)
    vn = vn_ref[...].astype(BF16)
    s_own = jnp.concatenate(
        [_dot_nt(qss[g], kn[:, g * hd:(g + 1) * hd]) for g in range(ngrp)], axis=0)
    s_own = jnp.where(kpos <= qpos, s_own, NEG_INF)
    m_fin = jnp.maximum(m_past, jnp.max(s_own, axis=1, keepdims=True))
    w = jnp.where(sel, jnp.exp(m_all - m_fin), 0.0)
    p_own = jnp.exp(s_own - m_fin)
    l_fin = (jnp.sum(w * l_ref[...], axis=1, keepdims=True)
             + jnp.sum(p_own, axis=1, keepdims=True))
    pb = p_own.astype(BF16)
    o_own = jnp.concatenate(
        [_dot(pb[g * gr:(g + 1) * gr, :], vn[:, g * hd:(g + 1) * hd]) for g in range(ngrp)], axis=0)
    ppl = LANES // hd
    lane = lax.broadcasted_iota(jnp.int32, (nrow, LANES), 1)
    acc = jnp.zeros((nrow, LANES), F32)
    for j in range(nblk // ppl):
        wj = w[:, j * ppl:j * ppl + 1]
        for i in range(1, ppl):
            wj = jnp.where(lane < i * hd, wj, w[:, j * ppl + i:j * ppl + i + 1])
        acc = acc + wj * po_ref[j]
    o = o_own
    for i in range(ppl):
        o = o + acc[:, i * hd:(i + 1) * hd]
    o = o / l_fin
    o_ref[...] = jnp.concatenate(
        [o[r * t_new:(r + 1) * t_new, :] for r in range(ngrp * grp)], axis=1)


def _sample_combine(q, k_new, v_new, means_t, m_part, l_part, po, *, db, nblk, ngrp, grp, hd):
    t_new = q.shape[0] // db
    nrow = ngrp * grp * t_new
    kvw = ngrp * hd
    kern = functools.partial(_sample_combine_kernel, nblk=nblk, ngrp=ngrp, grp=grp, hd=hd,
                             scale=float(hd) ** -0.5)
    return pl.pallas_call(
        kern,
        grid=(db,),
        in_specs=[pl.BlockSpec((t_new, q.shape[1]), lambda b: (b, 0)),
                  pl.BlockSpec((t_new, kvw), lambda b: (b, 0)),
                  pl.BlockSpec((t_new, kvw), lambda b: (b, 0)),
                  pl.BlockSpec((None, kvw, LANES), lambda b: (b, 0, 0)),
                  pl.BlockSpec((None, nrow, LANES), lambda b: (b, 0, 0)),
                  pl.BlockSpec((None, nrow, LANES), lambda b: (b, 0, 0)),
                  pl.BlockSpec((None,) + po.shape[1:], lambda b: (b, 0, 0, 0))],
        out_specs=pl.BlockSpec((t_new, q.shape[1]), lambda b: (b, 0)),
        out_shape=jax.ShapeDtypeStruct(q.shape, F32),
        compiler_params=_cparams(("parallel",)),
        name="moba_sample_combine",
    )(q, k_new, v_new, means_t, m_part, l_part, po)


def _rope_tables(pos, hd, reps):
    half = hd // 2
    inv_freq = ROPE_THETA ** (-jnp.arange(half, dtype=F32) / half)
    ang = pos.astype(F32)[:, None] * inv_freq[None, :]
    cos = jnp.cos(ang)
    sin = jnp.sin(ang)
    per_vreg = LANES // hd
    cos_t = jnp.tile(jnp.concatenate([cos, cos], axis=1), (reps, per_vreg))
    sin_t = jnp.tile(jnp.concatenate([-sin, sin], axis=1), (reps, per_vreg))
    return cos_t, sin_t


def _trunk(x, pos, conv_state8, delta_state, attend, wts, *, nseq, t_len, tm, prep_tt,
           delta_ct, delta_c, dims, want_t=False):
    nh, dk, conv_dim, n_a, depth, alpha, bh, kvh, hd = dims
    rows = nseq * t_len
    new_conv, new_delta = [], []
    k_sh = v_sh = t_sh = None
    for layer in range(depth):
        if layer < n_a:
            tm_in = 2 * tm if rows % (2 * tm) == 0 else tm
            proj = _matmul(x, wts["a_w_in"][layer], tm_in, wts["a_w_in"][layer].shape[1] // 3,
                           "in_proj")
            qkv, gb = _gdn_prep(proj, conv_state8[layer], wts["a_conv"][layer], wts["a_log"][layer],
                                wts["a_dtb"][layer], nseq=nseq, t_len=t_len, tt=prep_tt,
                                conv_dim=conv_dim, nh=nh, dk=dk)
            o, s_new = _delta_rule(qkv, gb, delta_state[layer], nseq=nseq, t_len=t_len,
                                   ct=delta_ct, c=delta_c, nh=nh, dk=dk)
            cw = wts["a_conv"][layer].shape[0]
            new_conv.append(proj.reshape(nseq, t_len, -1)[:, t_len - (cw - 1):, :conv_dim])
            new_delta.append(s_new)
            h = _out_proj(o, x, wts["a_w_out"][layer], wts["ln_g"][layer][0], wts["ln_b"][layer][0],
                          tm=tm, alpha=alpha, z_src=proj, z_blk=conv_dim // (nh * dk),
                          norm_w=wts["a_norm"][layer], nh=nh, dv=dk)
        else:
            j = layer - n_a
            w_all = wts["b_w_qkv"][j]
            cos_t, sin_t = _rope_tables(pos, hd, (tm // t_len) if tm > t_len else 1)
            q, k_l, v_l, *t_l = _qkv_rope(x, w_all, cos_t, sin_t, tm=tm, nq=bh * hd, nk=kvh * hd,
                                          half=hd // 2, t_seq=t_len if want_t else None)
            if layer == n_a:
                k_sh, v_sh, t_sh = k_l, v_l, t_l
            att = attend(q, k_sh, v_sh, *t_sh)
            h = _out_proj(att, x, wts["b_w_o"][j], wts["ln_g"][layer][0], wts["ln_b"][layer][0],
                          tm=tm, alpha=alpha)
        x = _ffn(h, wts["ffn_wg"][layer], wts["ffn_wu"][layer], wts["ffn_wd"][layer],
                 wts["ln_g"][layer][1], wts["ln_b"][layer][1], tm=tm, tf=256, alpha=alpha)
    return x, jnp.stack(new_conv), jnp.stack(new_delta), k_sh, v_sh, t_sh


def kernel(x_prompt, x_sample, state_conv, state_delta, cache_k, cache_v, page_table, ln_g, ln_b,
           a_w_in, a_conv, a_log_decay, a_dt_bias, a_norm, a_w_out, kv_w_k, kv_w_v, b_w_q, b_w_o,
           ffn_w_gate, ffn_w_up, ffn_w_down):
    bp, tp, d = x_prompt.shape
    db, ts, _ = x_sample.shape
    depth = ln_g.shape[0]
    n_a = a_w_in.shape[0]
    nh = a_log_decay.shape[1]
    dk = state_delta.shape[-2]
    conv_dim = a_conv.shape[-1]
    cw = a_conv.shape[1]
    n_pool, page, kvh, hd = cache_k.shape
    bh = b_w_q.shape[-1] // hd
    grp = bh // kvh
    n_pages = page_table.shape[1]
    past = n_pages * page
    alpha = (2.0 * depth) ** 0.25
    assert past % MOBA_BLOCK == 0 and ts <= MOBA_BLOCK and ts >= cw - 1 and tp >= cw - 1
    assert state_delta.shape[-1] == dk and conv_dim == 3 * nh * dk

    in_w = a_w_in.shape[-1]
    in_pad = -(-in_w // (3 * LANES)) * (3 * LANES)
    pad_lanes = lambda v: jnp.pad(v, ((0, 0), (0, LANES - v.shape[-1])))[:, None, :]
    wts = {
        "a_w_in": jnp.pad(a_w_in, ((0, 0), (0, 0), (0, in_pad - in_w))).astype(BF16),
        "a_conv": a_conv,
        "a_log": pad_lanes(a_log_decay),
        "a_dtb": pad_lanes(a_dt_bias),
        "a_norm": a_norm[:, None, :],
        "a_w_out": a_w_out.astype(BF16),
        "b_w_qkv": jnp.concatenate(
            [b_w_q, jnp.broadcast_to(kv_w_k, (b_w_q.shape[0],) + kv_w_k.shape),
             jnp.broadcast_to(kv_w_v, (b_w_q.shape[0],) + kv_w_v.shape)], axis=-1).astype(BF16),
        "b_w_o": b_w_o.astype(BF16),
        "ffn_wg": ffn_w_gate.astype(BF16),
        "ffn_wu": ffn_w_up.astype(BF16),
        "ffn_wd": ffn_w_down.astype(BF16),
        "ln_g": ln_g[:, :, None, :],
        "ln_b": ln_b[:, :, None, :],
    }
    dims = (nh, dk, conv_dim, n_a, depth, alpha, bh, kvh, hd)

    nb = tp // MOBA_BLOCK

    def attend_prompt(q, k, v, kt, vt):
        total = bp * nb
        step = SUBLANES if total % SUBLANES == 0 else total
        means = _block_means(k, nblk_step=step)
        return _moba_prompt(q, k, vt, means, nb_batch=bp, t_len=tp, ngrp=kvh, grp=grp, hd=hd)

    tm_p = 512 if (bp * tp) % 512 == 0 else 256
    y_p, p_conv, p_delta, _, _, (p_kt, p_vt) = _trunk(
        x_prompt.reshape(bp * tp, d), jnp.arange(tp, dtype=jnp.int32),
        jnp.zeros((n_a, bp, SUBLANES, conv_dim), F32), jnp.zeros((n_a, bp, nh, dk, dk), F32),
        attend_prompt, wts, nseq=bp, t_len=tp, tm=tm_p, prep_tt=256, delta_ct=256,
        delta_c=DELTA_CHUNK, dims=dims, want_t=True)
    p_k = jnp.transpose(p_kt.reshape(bp, kvh, hd, tp), (0, 3, 1, 2))
    p_v = jnp.transpose(p_vt.reshape(bp, kvh, hd, tp), (0, 3, 1, 2))

    ck = jnp.transpose(cache_k, (0, 2, 3, 1))
    cv = jnp.transpose(cache_v, (0, 2, 3, 1))
    npg = next(n for n in (16, 8, 2 * (MOBA_BLOCK // page)) if n_pages % n == 0)

    def attend_sample(q, k_new, v_new):
        means_t, m_part, l_part, po = _sample_partials(q, ck, cv, page_table, npg=npg, ngrp=kvh,
                                                       grp=grp, hd=hd)
        return _sample_combine(q, k_new, v_new, means_t, m_part, l_part, po, db=db,
                               nblk=past // MOBA_BLOCK, ngrp=kvh, grp=grp, hd=hd)

    conv8 = jnp.pad(state_conv, ((0, 0), (0, 0), (SUBLANES - (cw - 1), 0), (0, 0)))
    y_s, s_conv, s_delta, s_k, s_v, _ = _trunk(
        x_sample.reshape(db * ts, d), past + jnp.arange(ts, dtype=jnp.int32), conv8, state_delta,
        attend_sample, wts, nseq=db, t_len=ts, tm=db * ts, prep_tt=ts, delta_ct=ts, delta_c=ts,
        dims=dims)

    return (y_p.reshape(bp, tp, d), y_s.reshape(db, ts, d), p_conv, p_delta, p_k, p_v,
            s_conv, s_delta, s_k.reshape(db, ts, kvh, hd), s_v.reshape(db, ts, kvh, hd))
```

```python
import functools

import numpy as np
import jax
import jax.numpy as jnp
from jax import lax
from jax.experimental import pallas as pl
from jax.experimental.pallas import tpu as pltpu

F32 = jnp.float32
BF16 = jnp.bfloat16
HI = lax.Precision.HIGHEST

LANES = 128
SUBLANES = 8
VMEM_LIMIT_MB = 56

MOBA_BLOCK = 256
MOBA_TOPK = 3
CHUNK_BATCH = 32
DELTA_CHUNK = 64
DELTA_CHUNKS_PER_ITER = 4
ROPE_THETA = 10000.0
LN_EPS = 1e-5
RMS_EPS = 1e-6
L2_EPS = 1e-6
NEG_INF = float("-inf")
POS_INF = float("inf")
LOG2E = 1.4426950408889634


def _cparams(sem):
    return pltpu.CompilerParams(dimension_semantics=sem,
                                vmem_limit_bytes=VMEM_LIMIT_MB * 1024 * 1024)


def _dot(a, b, prec=None):
    return jnp.dot(a, b, precision=prec, preferred_element_type=F32)


def _dot_nt(a, b, prec=None):
    return lax.dot_general(a, b, (((1,), (1,)), ((), ())), precision=prec,
                           preferred_element_type=F32)


def _dot_tn(a, b, prec=None):
    return lax.dot_general(a, b, (((0,), (0,)), ((), ())), precision=prec,
                           preferred_element_type=F32)


def _split(a):
    hi = a.astype(BF16)
    return hi, (a - hi.astype(F32)).astype(BF16)


def _dot3(a, b):
    return _dot(a[0], b[0]) + (_dot(a[0], b[1]) + _dot(a[1], b[0]))


def _sigmoid(x):
    return 1.0 / (1.0 + jnp.exp(-x))


def _layer_norm(v, g, b):
    mu = jnp.mean(v, axis=-1, keepdims=True)
    d = v - mu
    var = jnp.mean(d * d, axis=-1, keepdims=True)
    return d * lax.rsqrt(var + LN_EPS) * g + b


def _mm_kernel(x_ref, w_ref, o_ref):
    o_ref[...] = _dot(x_ref[...].astype(BF16), w_ref[...])


def _matmul(x, w, tm, tn, name):
    r, k = x.shape
    n = w.shape[1]
    return pl.pallas_call(
        _mm_kernel,
        grid=(r // tm, n // tn),
        in_specs=[pl.BlockSpec((tm, k), lambda i, j: (i, 0)),
                  pl.BlockSpec((k, tn), lambda i, j: (0, j))],
        out_specs=pl.BlockSpec((tm, tn), lambda i, j: (i, j)),
        out_shape=jax.ShapeDtypeStruct((r, n), F32),
        compiler_params=_cparams(("parallel", "parallel")),
        name=name,
    )(x, w)


def _prep_kernel(cur_ref, prev_ref, st_ref, ab_ref, cw_ref, alog_ref, dtb_ref,
                 qkv_ref, gb_ref, xp_ref, *, tt, conv_w, n_q, n_qk, nh, q_scale):
    t = pl.program_id(1)
    hist = SUBLANES

    @pl.when(t == 0)
    def _():
        xp_ref[0:hist, :] = st_ref[...]

    @pl.when(t != 0)
    def _():
        xp_ref[0:hist, :] = prev_ref[...]

    xp_ref[hist:hist + tt, :] = cur_ref[...]
    base = hist - (conv_w - 1)
    for c in range(cur_ref.shape[1] // LANES):
        cs = slice(c * LANES, (c + 1) * LANES)
        y = xp_ref[base:base + tt, cs] * cw_ref[0:1, cs]
        for i in range(1, conv_w):
            y = y + xp_ref[base + i:base + i + tt, cs] * cw_ref[i:i + 1, cs]
        y = y * _sigmoid(y)
        if c < n_qk:
            y = y * lax.rsqrt(jnp.sum(y * y, axis=-1, keepdims=True) + L2_EPS)
            if c < n_q:
                y = y * q_scale
        qkv_ref[:, cs] = y
    ab = ab_ref[...]
    lane = lax.broadcasted_iota(jnp.int32, ab.shape, 1)
    sp = ab + dtb_ref[...]
    softplus = jnp.maximum(sp, 0.0) + jnp.log1p(jnp.exp(-jnp.abs(sp)))
    g = -jnp.exp(alog_ref[...]) * softplus
    gb_ref[...] = jnp.where(lane < nh, g, _sigmoid(ab))


def _gdn_prep(proj, state8, conv_w, alog, dtb, *, nseq, t_len, tt, conv_dim, nh, dk):
    rows = nseq * t_len
    nt = t_len // tt
    cw = conv_w.shape[0]
    ab_blk = (conv_dim + nh * dk) // LANES
    kern = functools.partial(_prep_kernel, tt=tt, conv_w=cw, n_q=nh, n_qk=2 * nh, nh=nh,
                             q_scale=float(dk) ** -0.5)
    tpb = tt // SUBLANES
    return pl.pallas_call(
        kern,
        grid=(nseq, nt),
        in_specs=[
            pl.BlockSpec((tt, conv_dim), lambda s, t: (s * nt + t, 0)),
            pl.BlockSpec((SUBLANES, conv_dim),
                         lambda s, t: (jnp.maximum((s * nt + t) * tpb - 1, 0), 0)),
            pl.BlockSpec((None, SUBLANES, conv_dim), lambda s, t: (s, 0, 0)),
            pl.BlockSpec((tt, LANES), lambda s, t: (s * nt + t, ab_blk)),
            pl.BlockSpec((cw, conv_dim), lambda s, t: (0, 0)),
            pl.BlockSpec((1, LANES), lambda s, t: (0, 0)),
            pl.BlockSpec((1, LANES), lambda s, t: (0, 0)),
        ],
        out_specs=[pl.BlockSpec((tt, conv_dim), lambda s, t: (s * nt + t, 0)),
                   pl.BlockSpec((tt, LANES), lambda s, t: (s * nt + t, 0))],
        out_shape=[jax.ShapeDtypeStruct((rows, conv_dim), F32),
                   jax.ShapeDtypeStruct((rows, LANES), F32)],
        scratch_shapes=[pltpu.VMEM((tt + SUBLANES, conv_dim), F32)],
        compiler_params=_cparams(("parallel", "arbitrary")),
        name="gdn_prep",
    )(proj, proj, state8, proj, conv_w, alog, dtb)


def _delta_kernel(q_ref, k_ref, v_ref, gb_ref, s0_ref, o_ref, s_ref, st_scr, *, ct, c, nh, dk):
    t = pl.program_id(1)

    @pl.when(t == 0)
    def _():
        st_scr[...] = s0_ref[...]

    row = lax.broadcasted_iota(jnp.int32, (c, c), 0)
    col = lax.broadcasted_iota(jnp.int32, (c, c), 1)
    causal = row >= col
    strict = row > col
    tril = causal.astype(F32)
    eye = (row == col).astype(F32)
    n_sq = max(int(np.ceil(np.log2(c))) - 1, 0)

    def prepare(rs):
        gbc = gb_ref[rs, :]
        gc = _dot(tril, gbc, HI)
        gt = gc.T
        eg = jnp.exp(gc)
        units = []
        for h in range(nh):
            cs = slice(h * dk, (h + 1) * dk)
            qh = q_ref[rs, cs]
            kh = k_ref[rs, cs]
            vh = v_ref[rs, cs]
            gcol = gc[:, h:h + 1]
            grow = gt[h:h + 1, :]
            bcol = gbc[:, nh + h:nh + h + 1]
            egcol = eg[:, h:h + 1]
            decay = jnp.where(causal, jnp.exp(jnp.where(causal, gcol - grow, 0.0)), 0.0)
            kb = kh.astype(BF16)
            kk = _dot_nt(kb, kb)
            qk = _dot_nt(qh.astype(BF16), kb)
            glast = grow[:, c - 1:c]
            units.append(dict(
                rs=rs, h=h,
                x=-jnp.where(strict, bcol * kk * decay, 0.0),
                qkm=jnp.where(causal, qk * decay, 0.0).astype(BF16),
                rhs=jnp.concatenate([vh * bcol, kh * (bcol * egcol)], axis=-1),
                qd=(qh * egcol).astype(BF16),
                kd=(kh * jnp.exp(glast - gcol)).astype(BF16),
                glast=jnp.exp(glast)))
        return units

    def solve(units):
        ps = [u["x"] for u in units]
        tms = [eye + p for p in ps]
        for _ in range(n_sq):
            sp = [_split(p) for p in ps]
            ps = [_dot3(s, s) for s in sp]
            sp = [_split(p) for p in ps]
            tms = [tm + _dot3(_split(tm), s) for tm, s in zip(tms, sp)]
        return [_dot3(_split(tm), _split(u["rhs"])) for tm, u in zip(tms, units)]

    def update(u, sol):
        h = u["h"]
        s = st_scr[h]
        sb = s.astype(BF16)
        v_new = sol[:, :dk] - _dot(sol[:, dk:].astype(BF16), sb)
        vb = v_new.astype(BF16)
        o_ref[u["rs"], h * dk:(h + 1) * dk] = _dot(u["qd"], sb) + _dot(u["qkm"], vb)
        st_scr[h] = s * u["glast"] + _dot_tn(u["kd"], vb)

    def chunks(r0, n):
        units = []
        for i in range(n):
            units += prepare(pl.ds(r0 + i * c, c))
        for u, sol in zip(units, solve(units)):
            update(u, sol)

    n_chunks = ct // c
    per_iter = DELTA_CHUNKS_PER_ITER if n_chunks % DELTA_CHUNKS_PER_ITER == 0 else 1
    if n_chunks == per_iter:
        chunks(0, per_iter)
    else:
        def body(ci, carry):
            chunks(pl.multiple_of(ci * (per_iter * c), per_iter * c), per_iter)
            return carry
        lax.fori_loop(0, n_chunks // per_iter, body, 0)

    @pl.when(t == pl.num_programs(1) - 1)
    def _():
        s_ref[...] = st_scr[...]


def _delta_rule(qkv, gb, s0, *, nseq, t_len, ct, c, nh, dk):
    rows = nseq * t_len
    nt = t_len // ct
    w = nh * dk
    kern = functools.partial(_delta_kernel, ct=ct, c=c, nh=nh, dk=dk)
    return pl.pallas_call(
        kern,
        grid=(nseq, nt),
        in_specs=[
            pl.BlockSpec((ct, w), lambda s, t: (s * nt + t, 0)),
            pl.BlockSpec((ct, w), lambda s, t: (s * nt + t, 1)),
            pl.BlockSpec((ct, w), lambda s, t: (s * nt + t, 2)),
            pl.BlockSpec((ct, LANES), lambda s, t: (s * nt + t, 0)),
            pl.BlockSpec((None, nh, dk, dk), lambda s, t: (s, 0, 0, 0)),
        ],
        out_specs=[pl.BlockSpec((ct, w), lambda s, t: (s * nt + t, 0)),
                   pl.BlockSpec((None, nh, dk, dk), lambda s, t: (s, 0, 0, 0))],
        out_shape=[jax.ShapeDtypeStruct((rows, w), F32),
                   jax.ShapeDtypeStruct((nseq, nh, dk, dk), F32)],
        scratch_shapes=[pltpu.VMEM((nh, dk, dk), F32)],
        compiler_params=_cparams(("parallel", "arbitrary")),
        name="delta_rule",
    )(qkv, qkv, qkv, gb, s0)


def _outproj_kernel(*refs, gated, nh, dv, alpha):
    if gated:
        o_ref, z_ref, nw_ref, x_ref, w_ref, g_ref, b_ref, out_ref, a_scr = refs
        for h in range(nh):
            cs = slice(h * dv, (h + 1) * dv)
            oh = o_ref[:, cs]
            zh = z_ref[:, cs]
            oh = oh * lax.rsqrt(jnp.mean(oh * oh, axis=-1, keepdims=True) + RMS_EPS) * nw_ref[...]
            a_scr[:, cs] = (oh * (zh * _sigmoid(zh))).astype(BF16)
        a = a_scr[...]
    else:
        o_ref, x_ref, w_ref, g_ref, b_ref, out_ref = refs
        a = o_ref[...].astype(BF16)
    v = alpha * x_ref[...] + _dot(a, w_ref[...])
    out_ref[...] = _layer_norm(v, g_ref[...], b_ref[...])


def _out_proj(o, x, w, g, b, *, tm, alpha, z_src=None, z_blk=0, norm_w=None, nh=1, dv=1):
    rows, d_in = o.shape
    d = x.shape[1]
    gated = z_src is not None
    kern = functools.partial(_outproj_kernel, gated=gated, nh=nh, dv=dv, alpha=alpha)
    row_spec = lambda width: pl.BlockSpec((tm, width), lambda i: (i, 0))
    const = lambda shape: pl.BlockSpec(shape, lambda i: (0, 0))
    in_specs = [row_spec(d_in)]
    args = [o]
    scratch = []
    if gated:
        in_specs += [pl.BlockSpec((tm, d_in), lambda i: (i, z_blk)), const((1, dv))]
        args += [z_src, norm_w]
        scratch = [pltpu.VMEM((tm, d_in), BF16)]
    in_specs += [row_spec(d), const((d_in, d)), const((1, d)), const((1, d))]
    args += [x, w, g, b]
    return pl.pallas_call(
        kern,
        grid=(rows // tm,),
        in_specs=in_specs,
        out_specs=row_spec(d),
        out_shape=jax.ShapeDtypeStruct((rows, d), F32),
        scratch_shapes=scratch,
        compiler_params=_cparams(("parallel",)),
        name="out_proj_ln",
    )(*args)


def _ffn_kernel(h_ref, wg_ref, wu_ref, wd_ref, g_ref, b_ref, out_ref, acc_ref, *, tf, alpha):
    hb = h_ref[...].astype(BF16)
    for c in range(wg_ref.shape[1] // tf):
        fs = slice(c * tf, (c + 1) * tf)
        gate = _dot(hb, wg_ref[:, fs])
        up = _dot(hb, wu_ref[:, fs])
        act = (gate * _sigmoid(gate) * up).astype(BF16)
        down = _dot(act, wd_ref[fs, :])
        if c == 0:
            acc_ref[...] = down
        else:
            acc_ref[...] += down
    v = alpha * h_ref[...] + acc_ref[...]
    out_ref[...] = _layer_norm(v, g_ref[...], b_ref[...])


def _ffn(h, wg, wu, wd, g, b, *, tm, tf, alpha):
    rows, d = h.shape
    f = wg.shape[1]
    kern = functools.partial(_ffn_kernel, tf=tf, alpha=alpha)
    resident = lambda shape: pl.BlockSpec(shape, lambda i: (0, 0), pipeline_mode=pl.Buffered(1))
    return pl.pallas_call(
        kern,
        grid=(rows // tm,),
        in_specs=[pl.BlockSpec((tm, d), lambda i: (i, 0)),
                  resident((d, f)), resident((d, f)), resident((f, d)),
                  pl.BlockSpec((1, d), lambda i: (0, 0)),
                  pl.BlockSpec((1, d), lambda i: (0, 0))],
        out_specs=pl.BlockSpec((tm, d), lambda i: (i, 0)),
        out_shape=jax.ShapeDtypeStruct((rows, d), F32),
        scratch_shapes=[pltpu.VMEM((tm, d), F32)],
        compiler_params=_cparams(("parallel",)),
        name="ffn_ln",
    )(h, wg, wu, wd, g, b)


def _qkv_kernel(x_ref, w_ref, cos_ref, sin_ref, q_ref, k_ref, v_ref, *t_refs, nq, nk, half):
    y = _dot(x_ref[...].astype(BF16), w_ref[...])
    cos = cos_ref[...]
    sin = sin_ref[...]
    lane = lax.broadcasted_iota(jnp.int32, cos.shape, 1)
    first = jnp.bitwise_and(lane, 2 * half - 1) < half

    def rope(xg):
        partner = jnp.where(first, pltpu.roll(xg, LANES - half, 1), pltpu.roll(xg, half, 1))
        return xg * cos + partner * sin

    for c in range(nq // LANES):
        q_ref[:, c * LANES:(c + 1) * LANES] = rope(y[:, c * LANES:(c + 1) * LANES])
    k = jnp.concatenate([rope(y[:, nq + c * LANES:nq + (c + 1) * LANES])
                         for c in range(nk // LANES)], axis=1)
    k_ref[...] = k
    v_ref[...] = y[:, nq + nk:]
    if t_refs:
        kt_ref, vt_ref = t_refs
        kt_ref[...] = k.T
        vt_ref[...] = y[:, nq + nk:].T


def _qkv_rope(x, w, cos, sin, *, tm, nq, nk, half, t_seq=None):
    rows, d = x.shape
    n_tab = cos.shape[0] // tm
    kern = functools.partial(_qkv_kernel, nq=nq, nk=nk, half=half)
    out_specs = [pl.BlockSpec((tm, nq), lambda i: (i, 0)),
                 pl.BlockSpec((tm, nk), lambda i: (i, 0)),
                 pl.BlockSpec((tm, nk), lambda i: (i, 0))]
    out_shape = [jax.ShapeDtypeStruct((rows, nq), F32),
                 jax.ShapeDtypeStruct((rows, nk), F32),
                 jax.ShapeDtypeStruct((rows, nk), F32)]
    if t_seq is not None:
        nt = t_seq // tm
        out_specs += [pl.BlockSpec((None, nk, tm), lambda i: (i // nt, 0, i % nt))] * 2
        out_shape += [jax.ShapeDtypeStruct((rows // t_seq, nk, t_seq), F32)] * 2
    return pl.pallas_call(
        kern,
        grid=(rows // tm,),
        in_specs=[pl.BlockSpec((tm, d), lambda i: (i, 0)),
                  pl.BlockSpec((d, nq + 2 * nk), lambda i: (0, 0)),
                  pl.BlockSpec((tm, LANES), lambda i: (i % n_tab, 0)),
                  pl.BlockSpec((tm, LANES), lambda i: (i % n_tab, 0))],
        out_specs=out_specs,
        out_shape=out_shape,
        compiler_params=_cparams(("parallel",)),
        name="qkv_rope",
    )(x, w, cos, sin)


def _kmeans_kernel(k_ref, m_ref, *, nblk):
    kb = k_ref[...].reshape(nblk, MOBA_BLOCK, k_ref.shape[1])
    m_ref[...] = jnp.sum(kb, axis=1) * (1.0 / MOBA_BLOCK)


def _block_means(k, *, nblk_step):
    rows, w = k.shape
    total = rows // MOBA_BLOCK
    return pl.pallas_call(
        functools.partial(_kmeans_kernel, nblk=nblk_step),
        grid=(total // nblk_step,),
        in_specs=[pl.BlockSpec((nblk_step * MOBA_BLOCK, w), lambda i: (i, 0))],
        out_specs=pl.BlockSpec((nblk_step, w), lambda i: (i, 0)),
        out_shape=jax.ShapeDtypeStruct((total, w), F32),
        compiler_params=_cparams(("parallel",)),
        name="block_means",
    )(k)


def _top_select(gate, n_past, axis):
    n = gate.shape[axis]
    idx = lax.broadcasted_iota(jnp.int32, gate.shape, axis).astype(F32)
    gm = jnp.where(idx < n_past, gate, NEG_INF)
    sel = jnp.zeros(gate.shape, F32)
    for r in range(MOBA_TOPK):
        mx = jnp.max(gm, axis=axis, keepdims=True)
        first = jnp.min(jnp.where(gm == mx, idx, float(n)), axis=axis, keepdims=True)
        pick = idx == first
        keep = jnp.where(r < n_past, 1.0, 0.0).astype(F32)
        sel = jnp.maximum(sel, jnp.where(pick, keep, 0.0))
        gm = jnp.where(pick, NEG_INF, gm)
    return sel


def _moba_prompt_kernel(qi_ref, ka_ref, kb_ref, last_ref, q_ref, ka, kb, vta, vtb, mean_ref, o_ref,
                        qs_scr, sel_scr, m_scr, l_scr, acc_scr, *, tq, ngrp, grp, hd, scale):
    p = pl.program_id(1)
    qi = qi_ref[p]
    kja = ka_ref[p]
    kjb = kb_ref[p]
    rows = grp * tq

    @pl.when(kja == qi)
    def _():
        qt = q_ref[...].T
        for g in range(ngrp):
            qg = jnp.concatenate(
                [qt[(g * grp + hh) * hd:(g * grp + hh + 1) * hd, :] for hh in range(grp)], axis=1)
            qs_scr[g] = (qg * (scale * LOG2E)).astype(BF16)
            gate = _dot(mean_ref[:, g * hd:(g + 1) * hd], qg, HI)
            sel_scr[g] = _top_select(gate, qi, 0)
        m_scr[...] = jnp.full(m_scr.shape, NEG_INF, F32)
        l_scr[...] = jnp.zeros(l_scr.shape, F32)
        acc_scr[...] = jnp.zeros(acc_scr.shape, F32)

    ones_rows = jnp.ones((2 * SUBLANES, MOBA_BLOCK), BF16)

    def attend(k_ref, vt_ref, kj, diagonal):
        kgs = [k_ref[:, g * hd:(g + 1) * hd].astype(BF16) for g in range(ngrp)]
        vgs = [jnp.concatenate([vt_ref[g * hd:(g + 1) * hd, :].astype(BF16), ones_rows], axis=0)
               for g in range(ngrp)]
        sel_rows = None if diagonal else [sel_scr[g, pl.ds(kj, 1), :] for g in range(ngrp)]
        units = [(g, slice(c * LANES, (c + 1) * LANES))
                 for g in range(ngrp) for c in range(rows // LANES)]
        for u0 in range(0, len(units), CHUNK_BATCH):
            attend_units(diagonal, kgs, vgs, sel_rows, units[u0:u0 + CHUNK_BATCH])

    def attend_units(diagonal, kgs, vgs, sel_rows, units):
        ss = [_dot(kgs[g], qs_scr[g, :, ls]) for g, ls in units]
        m_prevs = [m_scr[g, :, ls] for g, ls in units]
        if diagonal:
            kpos = lax.broadcasted_iota(jnp.int32, ss[0].shape, 0)
            lane = lax.broadcasted_iota(jnp.int32, ss[0].shape, 1)
            ss = [jnp.where(kpos <= jnp.bitwise_and(lane + ls.start, tq - 1), s, NEG_INF)
                  for s, (_, ls) in zip(ss, units)]
            m_news = [jnp.maximum(mp, jnp.max(s, axis=0, keepdims=True)) for mp, s in zip(m_prevs, ss)]
            m_refs = m_news
            alphas = [jnp.exp2(mp - mn) for mp, mn in zip(m_prevs, m_news)]
        else:
            picked = [sel_rows[g][:, ls] > 0.5 for g, ls in units]
            m_cand = [jnp.maximum(mp, jnp.max(s, axis=0, keepdims=True)) for mp, s in zip(m_prevs, ss)]
            m_news = [jnp.where(pk, mc, mp) for pk, mc, mp in zip(picked, m_cand, m_prevs)]
            m_refs = [jnp.where(pk, mc, POS_INF) for pk, mc in zip(picked, m_cand)]
            alphas = [jnp.exp2(mp - mn) for mp, mn in zip(m_prevs, m_news)]
        ps = [jnp.exp2(s - mr).astype(BF16) for s, mr in zip(ss, m_refs)]
        pvs = [_dot(vgs[g], p) for (g, _), p in zip(units, ps)]
        for (g, ls), al, mn, pv in zip(units, alphas, m_news, pvs):
            acc_scr[g, :, ls] = al * acc_scr[g, :, ls] + pv[:hd, :]
            l_scr[g, :, ls] = al * l_scr[g, :, ls] + pv[hd:hd + 1, :]
            m_scr[g, :, ls] = mn

    @pl.when(kja == qi)
    def _():
        attend(ka, vta, kja, True)

    @pl.when(kja < qi)
    def _():
        attend(ka, vta, kja, False)

    @pl.when(kjb < qi)
    def _():
        attend(kb, vtb, kjb, False)

    @pl.when(last_ref[p] == 1)
    def _():
        outs = []
        for g in range(ngrp):
            og = acc_scr[g] / l_scr[g]
            outs += [og[:, hh * tq:(hh + 1) * tq] for hh in range(grp)]
        o_ref[...] = jnp.concatenate(outs, axis=0).T


def _moba_prompt(q, k, vt, means, *, nb_batch, t_len, ngrp, grp, hd):
    tq = MOBA_BLOCK
    nq = t_len // tq
    nb = t_len // MOBA_BLOCK
    steps = []
    for i in range(nq):
        steps.append((i, i, i, int(i == 0)))
        for j in range(0, i, 2):
            steps.append((i, j, min(j + 1, i), int(j + 2 >= i)))
    tabs = [jnp.asarray(np.array([st[c] for st in steps], np.int32)) for c in range(4)]
    rows = grp * tq
    w = ngrp * grp * hd
    kvw = ngrp * hd
    kern = functools.partial(_moba_prompt_kernel, tq=tq, ngrp=ngrp, grp=grp, hd=hd,
                             scale=float(hd) ** -0.5)
    grid_spec = pltpu.PrefetchScalarGridSpec(
        num_scalar_prefetch=4,
        grid=(nb_batch, len(steps)),
        in_specs=[
            pl.BlockSpec((tq, w), lambda b, p, qt, ka, kb, la: (b * nq + qt[p], 0)),
            pl.BlockSpec((MOBA_BLOCK, kvw), lambda b, p, qt, ka, kb, la: (b * nb + ka[p], 0)),
            pl.BlockSpec((MOBA_BLOCK, kvw), lambda b, p, qt, ka, kb, la: (b * nb + kb[p], 0)),
            pl.BlockSpec((None, kvw, MOBA_BLOCK), lambda b, p, qt, ka, kb, la: (b, 0, ka[p])),
            pl.BlockSpec((None, kvw, MOBA_BLOCK), lambda b, p, qt, ka, kb, la: (b, 0, kb[p])),
            pl.BlockSpec((nb, kvw), lambda b, p, qt, ka, kb, la: (b, 0)),
        ],
        out_specs=pl.BlockSpec((tq, w), lambda b, p, qt, ka, kb, la: (b * nq + qt[p], 0)),
        scratch_shapes=[pltpu.VMEM((ngrp, hd, rows), BF16),
                        pltpu.VMEM((ngrp, nb, rows), F32),
                        pltpu.VMEM((ngrp, 1, rows), F32),
                        pltpu.VMEM((ngrp, 1, rows), F32),
                        pltpu.VMEM((ngrp, hd, rows), F32)],
    )
    return pl.pallas_call(
        kern,
        grid_spec=grid_spec,
        out_shape=jax.ShapeDtypeStruct(q.shape, F32),
        compiler_params=_cparams(("parallel", "arbitrary")),
        name="moba_prompt",
    )(*tabs, q, k, k, vt, vt, means)


def _sample_partial_kernel(pt_ref, q_ref, *refs, npg, ngrp, grp, hd, scale):
    k_pages = refs[:npg]
    v_pages = refs[npg:2 * npg]
    mean_ref, m_ref, l_ref, po_ref, qs_scr = refs[2 * npg:]
    step = pl.program_id(1)
    t_new = q_ref.shape[0]
    gr = grp * t_new

    @pl.when(step == 0)
    def _():
        for g in range(ngrp):
            qg = jnp.concatenate(
                [q_ref[:, (g * grp + hh) * hd:(g * grp + hh + 1) * hd] for hh in range(grp)], axis=0)
            qs_scr[g] = (qg * scale).astype(BF16)
        mean_ref[...] = jnp.zeros(mean_ref.shape, F32)
        m_ref[...] = jnp.zeros(m_ref.shape, F32)
        l_ref[...] = jnp.zeros(l_ref.shape, F32)

    ppb = MOBA_BLOCK // k_pages[0].shape[-1]
    bps = npg // ppb
    lane = lax.broadcasted_iota(jnp.int32, m_ref.shape, 1)
    lane_hd = lax.broadcasted_iota(jnp.int32, (hd, LANES), 1)
    blocks = [slice(blk * ppb, (blk + 1) * ppb) for blk in range(bps)]
    kts = [[jnp.concatenate([r[g] for r in k_pages[sl]], axis=1) for g in range(ngrp)]
           for sl in blocks]
    scs = [jnp.concatenate([_dot(qs_scr[g], kt[g].astype(BF16)) for g in range(ngrp)], axis=0)
           for kt in kts]
    ms = [jnp.max(sc, axis=1, keepdims=True) for sc in scs]
    prs = [jnp.exp(sc - m) for sc, m in zip(scs, ms)]
    ls = [jnp.sum(pr, axis=1, keepdims=True) for pr in prs]
    outs = []
    for sl, pr in zip(blocks, prs):
        prb = pr.astype(BF16)
        outs.append(jnp.concatenate(
            [_dot_nt(prb[g * gr:(g + 1) * gr, :],
                     jnp.concatenate([r[g] for r in v_pages[sl]], axis=1).astype(BF16))
             for g in range(ngrp)], axis=0))
    m_tile = m_ref[...]
    l_tile = l_ref[...]
    for blk in range(bps):
        bidx = step * bps + blk
        m_tile = jnp.where(lane == bidx, ms[blk], m_tile)
        l_tile = jnp.where(lane == bidx, ls[blk], l_tile)
        for g in range(ngrp):
            mean_col = jnp.sum(kts[blk][g], axis=1, keepdims=True) * (1.0 / MOBA_BLOCK)
            rs = slice(g * hd, (g + 1) * hd)
            mean_ref[rs, :] = jnp.where(lane_hd == bidx, mean_col, mean_ref[rs, :])
    m_ref[...] = m_tile
    l_ref[...] = l_tile
    ppl = LANES // hd
    for j in range(bps // ppl):
        po_ref[j] = jnp.concatenate(outs[j * ppl:(j + 1) * ppl], axis=1)


def _sample_partials(q, cache_k, cache_v, page_table, *, npg, ngrp, grp, hd):
    db, n_pages = page_table.shape
    page = cache_k.shape[-1]
    kvw = ngrp * hd
    t_new = q.shape[0] // db
    ppb = MOBA_BLOCK // page
    bps = npg // ppb
    ppl = LANES // hd
    nsteps = n_pages // npg
    nblk = n_pages // ppb
    nrow = ngrp * grp * t_new
    assert nblk <= LANES and bps % ppl == 0
    kern = functools.partial(_sample_partial_kernel, npg=npg, ngrp=ngrp, grp=grp, hd=hd,
                             scale=float(hd) ** -0.5)

    def page_spec(i):
        return pl.BlockSpec((None, ngrp, hd, page),
                            lambda b, s, pt: (pt[b * n_pages + s * npg + i], 0, 0, 0))

    grid_spec = pltpu.PrefetchScalarGridSpec(
        num_scalar_prefetch=1,
        grid=(db, nsteps),
        in_specs=[pl.BlockSpec((t_new, q.shape[1]), lambda b, s, pt: (b, 0))]
        + [page_spec(i) for i in range(npg)] + [page_spec(i) for i in range(npg)],
        out_specs=[pl.BlockSpec((None, kvw, LANES), lambda b, s, pt: (b, 0, 0)),
                   pl.BlockSpec((None, nrow, LANES), lambda b, s, pt: (b, 0, 0)),
                   pl.BlockSpec((None, nrow, LANES), lambda b, s, pt: (b, 0, 0)),
                   pl.BlockSpec((None, bps // ppl, nrow, LANES), lambda b, s, pt: (b, s, 0, 0))],
        scratch_shapes=[pltpu.VMEM((ngrp, grp * t_new, hd), BF16)],
    )
    return pl.pallas_call(
        kern,
        grid_spec=grid_spec,
        out_shape=[jax.ShapeDtypeStruct((db, kvw, LANES), F32),
                   jax.ShapeDtypeStruct((db, nrow, LANES), F32),
                   jax.ShapeDtypeStruct((db, nrow, LANES), F32),
                   jax.ShapeDtypeStruct((db, nblk // ppl, nrow, LANES), F32)],
        compiler_params=_cparams(("parallel", "arbitrary")),
        name="moba_sample_partials",
    )(page_table.reshape(-1), q, *([cache_k] * npg), *([cache_v] * npg))


def _sample_combine_kernel(q_ref, kn_ref, vn_ref, mean_ref, m_ref, l_ref, po_ref, o_ref,
                           *, nblk, ngrp, grp, hd, scale):
    t_new = q_ref.shape[0]
    gr = grp * t_new
    nrow = ngrp * gr
    gates, qss = [], []
    for g in range(ngrp):
        qg = jnp.concatenate(
            [q_ref[:, (g * grp + hh) * hd:(g * grp + hh + 1) * hd] for hh in range(grp)], axis=0)
        qss.append((qg * scale).astype(BF16))
        gates.append(_dot(qg, mean_ref[g * hd:(g + 1) * hd, :], HI))
    sel = _top_select(jnp.concatenate(gates, axis=0), nblk, 1) > 0.5
    m_all = m_ref[...]
    m_past = jnp.max(jnp.where(sel, m_all, NEG_INF), axis=1, keepdims=True)
    kpos = lax.broadcasted_iota(jnp.int32, (nrow, t_new), 1)
    qpos = jnp.bitwise_and(lax.broadcasted_iota(jnp.int32, (nrow, t_new), 0), t_new - 1)
    kn = kn_ref[...].astype(BF16)
    vn = vn_ref[...].astype(BF16)
    s_own = jnp.concatenate(
        [_dot_nt(qss[g], kn[:, g * hd:(g + 1) * hd]) for g in range(ngrp)], axis=0)
    s_own = jnp.where(kpos <= qpos, s_own, NEG_INF)
    m_fin = jnp.maximum(m_past, jnp.max(s_own, axis=1, keepdims=True))
    w = jnp.where(sel, jnp.exp(m_all - m_fin), 0.0)
    p_own = jnp.exp(s_own - m_fin)
    l_fin = (jnp.sum(w * l_ref[...], axis=1, keepdims=True)
             + jnp.sum(p_own, axis=1, keepdims=True))
    pb = p_own.astype(BF16)
    o_own = jnp.concatenate(
        [_dot(pb[g * gr:(g + 1) * gr, :], vn[:, g * hd:(g + 1) * hd]) for g in range(ngrp)], axis=0)
    ppl = LANES // hd
    lane = lax.broadcasted_iota(jnp.int32, (nrow, LANES), 1)
    acc = jnp.zeros((nrow, LANES), F32)
    for j in range(nblk // ppl):
        wj = w[:, j * ppl:j * ppl + 1]
        for i in range(1, ppl):
            wj = jnp.where(lane < i * hd, wj, w[:, j * ppl + i:j * ppl + i + 1])
        acc = acc + wj * po_ref[j]
    o = o_own
    for i in range(ppl):
        o = o + acc[:, i * hd:(i + 1) * hd]
    o = o / l_fin
    o_ref[...] = jnp.concatenate(
        [o[r * t_new:(r + 1) * t_new, :] for r in range(ngrp * grp)], axis=1)


def _sample_combine(q, k_new, v_new, means_t, m_part, l_part, po, *, db, nblk, ngrp, grp, hd):
    t_new = q.shape[0] // db
    nrow = ngrp * grp * t_new
    kvw = ngrp * hd
    kern = functools.partial(_sample_combine_kernel, nblk=nblk, ngrp=ngrp, grp=grp, hd=hd,
                             scale=float(hd) ** -0.5)
    return pl.pallas_call(
        kern,
        grid=(db,),
        in_specs=[pl.BlockSpec((t_new, q.shape[1]), lambda b: (b, 0)),
                  pl.BlockSpec((t_new, kvw), lambda b: (b, 0)),
                  pl.BlockSpec((t_new, kvw), lambda b: (b, 0)),
                  pl.BlockSpec((None, kvw, LANES), lambda b: (b, 0, 0)),
                  pl.BlockSpec((None, nrow, LANES), lambda b: (b, 0, 0)),
                  pl.BlockSpec((None, nrow, LANES), lambda b: (b, 0, 0)),
                  pl.BlockSpec((None,) + po.shape[1:], lambda b: (b, 0, 0, 0))],
        out_specs=pl.BlockSpec((t_new, q.shape[1]), lambda b: (b, 0)),
        out_shape=jax.ShapeDtypeStruct(q.shape, F32),
        compiler_params=_cparams(("parallel",)),
        name="moba_sample_combine",
    )(q, k_new, v_new, means_t, m_part, l_part, po)


def _rope_tables(pos, hd, reps):
    half = hd // 2
    inv_freq = ROPE_THETA ** (-jnp.arange(half, dtype=F32) / half)
    ang = pos.astype(F32)[:, None] * inv_freq[None, :]
    cos = jnp.cos(ang)
    sin = jnp.sin(ang)
    per_vreg = LANES // hd
    cos_t = jnp.tile(jnp.concatenate([cos, cos], axis=1), (reps, per_vreg))
    sin_t = jnp.tile(jnp.concatenate([-sin, sin], axis=1), (reps, per_vreg))
    return cos_t, sin_t


def _trunk(x, pos, conv_state8, delta_state, attend, wts, *, nseq, t_len, tm, prep_tt,
           delta_ct, delta_c, dims, want_t=False):
    nh, dk, conv_dim, n_a, depth, alpha, bh, kvh, hd = dims
    rows = nseq * t_len
    new_conv, new_delta = [], []
    k_sh = v_sh = t_sh = None
    for layer in range(depth):
        if layer < n_a:
            tm_in = 2 * tm if rows % (2 * tm) == 0 else tm
            proj = _matmul(x, wts["a_w_in"][layer], tm_in, wts["a_w_in"][layer].shape[1] // 3,
                           "in_proj")
            qkv, gb = _gdn_prep(proj, conv_state8[layer], wts["a_conv"][layer], wts["a_log"][layer],
                                wts["a_dtb"][layer], nseq=nseq, t_len=t_len, tt=prep_tt,
                                conv_dim=conv_dim, nh=nh, dk=dk)
            o, s_new = _delta_rule(qkv, gb, delta_state[layer], nseq=nseq, t_len=t_len,
                                   ct=delta_ct, c=delta_c, nh=nh, dk=dk)
            cw = wts["a_conv"][layer].shape[0]
            new_conv.append(proj.reshape(nseq, t_len, -1)[:, t_len - (cw - 1):, :conv_dim])
            new_delta.append(s_new)
            h = _out_proj(o, x, wts["a_w_out"][layer], wts["ln_g"][layer][0], wts["ln_b"][layer][0],
                          tm=tm, alpha=alpha, z_src=proj, z_blk=conv_dim // (nh * dk),
                          norm_w=wts["a_norm"][layer], nh=nh, dv=dk)
        else:
            j = layer - n_a
            w_all = wts["b_w_qkv"][j]
            cos_t, sin_t = _rope_tables(pos, hd, (tm // t_len) if tm > t_len else 1)
            q, k_l, v_l, *t_l = _qkv_rope(x, w_all, cos_t, sin_t, tm=tm, nq=bh * hd, nk=kvh * hd,
                                          half=hd // 2, t_seq=t_len if want_t else None)
            if layer == n_a:
                k_sh, v_sh, t_sh = k_l, v_l, t_l
            att = attend(q, k_sh, v_sh, *t_sh)
            h = _out_proj(att, x, wts["b_w_o"][j], wts["ln_g"][layer][0], wts["ln_b"][layer][0],
                          tm=tm, alpha=alpha)
        x = _ffn(h, wts["ffn_wg"][layer], wts["ffn_wu"][layer], wts["ffn_wd"][layer],
                 wts["ln_g"][layer][1], wts["ln_b"][layer][1], tm=tm, tf=256, alpha=alpha)
    return x, jnp.stack(new_conv), jnp.stack(new_delta), k_sh, v_sh, t_sh


def kernel(x_prompt, x_sample, state_conv, state_delta, cache_k, cache_v, page_table, ln_g, ln_b,
           a_w_in, a_conv, a_log_decay, a_dt_bias, a_norm, a_w_out, kv_w_k, kv_w_v, b_w_q, b_w_o,
           ffn_w_gate, ffn_w_up, ffn_w_down):
    bp, tp, d = x_prompt.shape
    db, ts, _ = x_sample.shape
    depth = ln_g.shape[0]
    n_a = a_w_in.shape[0]
    nh = a_log_decay.shape[1]
    dk = state_delta.shape[-2]
    conv_dim = a_conv.shape[-1]
    cw = a_conv.shape[1]
    n_pool, page, kvh, hd = cache_k.shape
    bh = b_w_q.shape[-1] // hd
    grp = bh // kvh
    n_pages = page_table.shape[1]
    past = n_pages * page
    alpha = (2.0 * depth) ** 0.25
    assert past % MOBA_BLOCK == 0 and ts <= MOBA_BLOCK and ts >= cw - 1 and tp >= cw - 1
    assert state_delta.shape[-1] == dk and conv_dim == 3 * nh * dk

    in_w = a_w_in.shape[-1]
    in_pad = -(-in_w // (3 * LANES)) * (3 * LANES)
    pad_lanes = lambda v: jnp.pad(v, ((0, 0), (0, LANES - v.shape[-1])))[:, None, :]
    wts = {
        "a_w_in": jnp.pad(a_w_in, ((0, 0), (0, 0), (0, in_pad - in_w))).astype(BF16),
        "a_conv": a_conv,
        "a_log": pad_lanes(a_log_decay),
        "a_dtb": pad_lanes(a_dt_bias),
        "a_norm": a_norm[:, None, :],
        "a_w_out": a_w_out.astype(BF16),
        "b_w_qkv": jnp.concatenate(
            [b_w_q, jnp.broadcast_to(kv_w_k, (b_w_q.shape[0],) + kv_w_k.shape),
             jnp.broadcast_to(kv_w_v, (b_w_q.shape[0],) + kv_w_v.shape)], axis=-1).astype(BF16),
        "b_w_o": b_w_o.astype(BF16),
        "ffn_wg": ffn_w_gate.astype(BF16),
        "ffn_wu": ffn_w_up.astype(BF16),
        "ffn_wd": ffn_w_down.astype(BF16),
        "ln_g": ln_g[:, :, None, :],
        "ln_b": ln_b[:, :, None, :],
    }
    dims = (nh, dk, conv_dim, n_a, depth, alpha, bh, kvh, hd)

    nb = tp // MOBA_BLOCK

    def attend_prompt(q, k, v, kt, vt):
        total = bp * nb
        step = SUBLANES if total % SUBLANES == 0 else total
        means = _block_means(k, nblk_step=step)
        return _moba_prompt(q, k, vt, means, nb_batch=bp, t_len=tp, ngrp=kvh, grp=grp, hd=hd)

    tm_p = 512 if (bp * tp) % 512 == 0 else 256
    y_p, p_conv, p_delta, _, _, (p_kt, p_vt) = _trunk(
        x_prompt.reshape(bp * tp, d), jnp.arange(tp, dtype=jnp.int32),
        jnp.zeros((n_a, bp, SUBLANES, conv_dim), F32), jnp.zeros((n_a, bp, nh, dk, dk), F32),
        attend_prompt, wts, nseq=bp, t_len=tp, tm=tm_p, prep_tt=tm_p, delta_ct=256,
        delta_c=DELTA_CHUNK, dims=dims, want_t=True)
    p_k = jnp.transpose(p_kt.reshape(bp, kvh, hd, tp), (0, 3, 1, 2))
    p_v = jnp.transpose(p_vt.reshape(bp, kvh, hd, tp), (0, 3, 1, 2))

    ck = jnp.transpose(cache_k, (0, 2, 3, 1))
    cv = jnp.transpose(cache_v, (0, 2, 3, 1))
    npg = next(n for n in (16, 8, 2 * (MOBA_BLOCK // page)) if n_pages % n == 0)

    def attend_sample(q, k_new, v_new):
        means_t, m_part, l_part, po = _sample_partials(q, ck, cv, page_table, npg=npg, ngrp=kvh,
                                                       grp=grp, hd=hd)
        return _sample_combine(q, k_new, v_new, means_t, m_part, l_part, po, db=db,
                               nblk=past // MOBA_BLOCK, ngrp=kvh, grp=grp, hd=hd)

    conv8 = jnp.pad(state_conv, ((0, 0), (0, 0), (SUBLANES - (cw - 1), 0), (0, 0)))
    y_s, s_conv, s_delta, s_k, s_v, _ = _trunk(
        x_sample.reshape(db * ts, d), past + jnp.arange(ts, dtype=jnp.int32), conv8, state_delta,
        attend_sample, wts, nseq=db, t_len=ts, tm=db * ts, prep_tt=ts, delta_ct=ts, delta_c=ts,
        dims=dims)

    return (y_p.reshape(bp, tp, d), y_s.reshape(db, ts, d), p_conv, p_delta, p_k, p_v,
            s_conv, s_delta, s_k.reshape(db, ts, kvh, hd), s_v.reshape(db, ts, kvh, hd))
```

```python
import functools

import numpy as np
import jax
import jax.numpy as jnp
from jax import lax
from jax.experimental import pallas as pl
from jax.experimental.pallas import tpu as pltpu

F32 = jnp.float32
BF16 = jnp.bfloat16
HI = lax.Precision.HIGHEST

LANES = 128
SUBLANES = 8
VMEM_LIMIT_MB = 56

MOBA_BLOCK = 256
MOBA_TOPK = 3
CHUNK_BATCH = 32
DELTA_CHUNK = 64
DELTA_CHUNKS_PER_ITER = 4
ROPE_THETA = 10000.0
LN_EPS = 1e-5
RMS_EPS = 1e-6
L2_EPS = 1e-6
NEG_INF = float("-inf")
POS_INF = float("inf")
LOG2E = 1.4426950408889634


def _cparams(sem):
    return pltpu.CompilerParams(dimension_semantics=sem,
                                vmem_limit_bytes=VMEM_LIMIT_MB * 1024 * 1024)


def _dot(a, b, prec=None):
    return jnp.dot(a, b, precision=prec, preferred_element_type=F32)


def _dot_nt(a, b, prec=None):
    return lax.dot_general(a, b, (((1,), (1,)), ((), ())), precision=prec,
                           preferred_element_type=F32)


def _dot_tn(a, b, prec=None):
    return lax.dot_general(a, b, (((0,), (0,)), ((), ())), precision=prec,
                           preferred_element_type=F32)


def _split(a):
    hi = a.astype(BF16)
    return hi, (a - hi.astype(F32)).astype(BF16)


def _dot3(a, b):
    return _dot(a[0], b[0]) + (_dot(a[0], b[1]) + _dot(a[1], b[0]))


def _sigmoid(x):
    return 1.0 / (1.0 + jnp.exp(-x))


def _layer_norm(v, g, b):
    mu = jnp.mean(v, axis=-1, keepdims=True)
    d = v - mu
    var = jnp.mean(d * d, axis=-1, keepdims=True)
    return d * lax.rsqrt(var + LN_EPS) * g + b


def _mm_kernel(x_ref, w_ref, o_ref):
    o_ref[...] = _dot(x_ref[...].astype(BF16), w_ref[...])


def _matmul(x, w, tm, tn, name):
    r, k = x.shape
    n = w.shape[1]
    return pl.pallas_call(
        _mm_kernel,
        grid=(r // tm, n // tn),
        in_specs=[pl.BlockSpec((tm, k), lambda i, j: (i, 0)),
                  pl.BlockSpec((k, tn), lambda i, j: (0, j))],
        out_specs=pl.BlockSpec((tm, tn), lambda i, j: (i, j)),
        out_shape=jax.ShapeDtypeStruct((r, n), F32),
        compiler_params=_cparams(("parallel", "parallel")),
        name=name,
    )(x, w)


def _conv_norm(xp_ref, cw_ref, qkv_ref, cols, *, tt, conv_w, n_q, n_qk, q_scale):
    base = SUBLANES - (conv_w - 1)
    for c in cols:
        cs = slice(c * LANES, (c + 1) * LANES)
        y = xp_ref[base:base + tt, cs] * cw_ref[0:1, cs]
        for i in range(1, conv_w):
            y = y + xp_ref[base + i:base + i + tt, cs] * cw_ref[i:i + 1, cs]
        y = y * _sigmoid(y)
        if c < n_qk:
            y = y * lax.rsqrt(jnp.sum(y * y, axis=-1, keepdims=True) + L2_EPS)
            if c < n_q:
                y = y * q_scale
        qkv_ref[:, cs] = y


def _decay_beta(ab, alog, dtb, nh):
    lane = lax.broadcasted_iota(jnp.int32, ab.shape, 1)
    sp = ab + dtb
    softplus = jnp.maximum(sp, 0.0) + jnp.log1p(jnp.exp(-jnp.abs(sp)))
    return jnp.where(lane < nh, -jnp.exp(alog) * softplus, _sigmoid(ab))


def _prep_kernel(cur_ref, prev_ref, st_ref, ab_ref, cw_ref, alog_ref, dtb_ref,
                 qkv_ref, gb_ref, xp_ref, *, tt, conv_w, n_q, n_qk, nh, q_scale):
    t = pl.program_id(1)
    hist = SUBLANES

    @pl.when(t == 0)
    def _():
        xp_ref[0:hist, :] = st_ref[...]

    @pl.when(t != 0)
    def _():
        xp_ref[0:hist, :] = prev_ref[...]

    xp_ref[hist:hist + tt, :] = cur_ref[...]
    _conv_norm(xp_ref, cw_ref, qkv_ref, range(cur_ref.shape[1] // LANES), tt=tt, conv_w=conv_w,
               n_q=n_q, n_qk=n_qk, q_scale=q_scale)
    gb_ref[...] = _decay_beta(ab_ref[...], alog_ref[...], dtb_ref[...], nh)


def _gdn_in_kernel(x_ref, w_ref, st_ref, cw_ref, alog_ref, dtb_ref, qkv_ref, gb_ref, z_ref, tail_ref,
                   xp_ref, *, tt, conv_dim, zw, conv_w, n_q, n_qk, nh, q_scale):
    hist = SUBLANES

    @pl.when(pl.program_id(1) == 0)
    def _():
        xp_ref[0:hist, :] = st_ref[...]

    xb = x_ref[...].astype(BF16)
    mxu_w = 2 * LANES
    for c in range(conv_dim // mxu_w):
        cs = slice(c * mxu_w, (c + 1) * mxu_w)
        xp_ref[hist:hist + tt, cs] = _dot(xb, w_ref[:, cs])
        _conv_norm(xp_ref, cw_ref, qkv_ref, range(2 * c, 2 * c + 2), tt=tt, conv_w=conv_w,
                   n_q=n_q, n_qk=n_qk, q_scale=q_scale)
    z_ref[...] = _dot(xb, w_ref[:, conv_dim:conv_dim + zw])
    ab = _dot(xb, w_ref[:, conv_dim + zw:conv_dim + zw + LANES])
    gb_ref[...] = _decay_beta(ab, alog_ref[...], dtb_ref[...], nh)
    tail = xp_ref[tt:tt + hist, :]
    tail_ref[...] = tail
    xp_ref[0:hist, :] = tail


def _gdn_in_fused(x, w, state8, conv_w, alog, dtb, *, nseq, t_len, tt, conv_dim, nh, dk):
    rows, d = x.shape
    nt = t_len // tt
    cw = conv_w.shape[0]
    zw = nh * dk
    kern = functools.partial(_gdn_in_kernel, tt=tt, conv_dim=conv_dim, zw=zw, conv_w=cw, n_q=nh,
                             n_qk=2 * nh, nh=nh, q_scale=float(dk) ** -0.5)
    row = lambda width: pl.BlockSpec((tt, width), lambda s, t: (s * nt + t, 0))
    const = lambda shape: pl.BlockSpec(shape, lambda s, t: (0, 0))
    per_seq = pl.BlockSpec((None, SUBLANES, conv_dim), lambda s, t: (s, 0, 0))
    return pl.pallas_call(
        kern,
        grid=(nseq, nt),
        in_specs=[row(d),
                  pl.BlockSpec(w.shape, lambda s, t: (0, 0), pipeline_mode=pl.Buffered(1)),
                  per_seq, const((cw, conv_dim)), const((1, LANES)), const((1, LANES))],
        out_specs=[row(conv_dim), row(LANES), row(zw), per_seq],
        out_shape=[jax.ShapeDtypeStruct((rows, conv_dim), F32),
                   jax.ShapeDtypeStruct((rows, LANES), F32),
                   jax.ShapeDtypeStruct((rows, zw), F32),
                   jax.ShapeDtypeStruct((nseq, SUBLANES, conv_dim), F32)],
        scratch_shapes=[pltpu.VMEM((tt + SUBLANES, conv_dim), F32)],
        compiler_params=_cparams(("parallel", "arbitrary")),
        name="gdn_in_fused",
    )(x, w, state8, conv_w, alog, dtb)


def _gdn_prep(proj, state8, conv_w, alog, dtb, *, nseq, t_len, tt, conv_dim, nh, dk):
    rows = nseq * t_len
    nt = t_len // tt
    cw = conv_w.shape[0]
    ab_blk = (conv_dim + nh * dk) // LANES
    kern = functools.partial(_prep_kernel, tt=tt, conv_w=cw, n_q=nh, n_qk=2 * nh, nh=nh,
                             q_scale=float(dk) ** -0.5)
    tpb = tt // SUBLANES
    return pl.pallas_call(
        kern,
        grid=(nseq, nt),
        in_specs=[
            pl.BlockSpec((tt, conv_dim), lambda s, t: (s * nt + t, 0)),
            pl.BlockSpec((SUBLANES, conv_dim),
                         lambda s, t: (jnp.maximum((s * nt + t) * tpb - 1, 0), 0)),
            pl.BlockSpec((None, SUBLANES, conv_dim), lambda s, t: (s, 0, 0)),
            pl.BlockSpec((tt, LANES), lambda s, t: (s * nt + t, ab_blk)),
            pl.BlockSpec((cw, conv_dim), lambda s, t: (0, 0)),
            pl.BlockSpec((1, LANES), lambda s, t: (0, 0)),
            pl.BlockSpec((1, LANES), lambda s, t: (0, 0)),
        ],
        out_specs=[pl.BlockSpec((tt, conv_dim), lambda s, t: (s * nt + t, 0)),
                   pl.BlockSpec((tt, LANES), lambda s, t: (s * nt + t, 0))],
        out_shape=[jax.ShapeDtypeStruct((rows, conv_dim), F32),
                   jax.ShapeDtypeStruct((rows, LANES), F32)],
        scratch_shapes=[pltpu.VMEM((tt + SUBLANES, conv_dim), F32)],
        compiler_params=_cparams(("parallel", "arbitrary")),
        name="gdn_prep",
    )(proj, proj, state8, proj, conv_w, alog, dtb)


def _delta_kernel(q_ref, k_ref, v_ref, gb_ref, s0_ref, o_ref, s_ref, st_scr, *, sb, ct, c, nh, dk):
    t = pl.program_id(1)

    @pl.when(t == 0)
    def _():
        st_scr[...] = s0_ref[...]

    row = lax.broadcasted_iota(jnp.int32, (c, c), 0)
    col = lax.broadcasted_iota(jnp.int32, (c, c), 1)
    causal = row >= col
    strict = row > col
    tril = causal.astype(F32)
    eye = (row == col).astype(F32)
    n_sq = max(int(np.ceil(np.log2(c))) - 1, 0)

    def prepare(rs, seq):
        gbc = gb_ref[rs, :]
        gc = _dot(tril, gbc, HI)
        gt = gc.T
        eg = jnp.exp(gc)
        units = []
        for h in range(nh):
            cs = slice(h * dk, (h + 1) * dk)
            qh = q_ref[rs, cs]
            kh = k_ref[rs, cs]
            vh = v_ref[rs, cs]
            gcol = gc[:, h:h + 1]
            grow = gt[h:h + 1, :]
            bcol = gbc[:, nh + h:nh + h + 1]
            egcol = eg[:, h:h + 1]
            decay = jnp.where(causal, jnp.exp(jnp.where(causal, gcol - grow, 0.0)), 0.0)
            kb = kh.astype(BF16)
            kk = _dot_nt(kb, kb)
            qk = _dot_nt(qh.astype(BF16), kb)
            glast = grow[:, c - 1:c]
            units.append(dict(
                rs=rs, h=h, seq=seq,
                x=-jnp.where(strict, bcol * kk * decay, 0.0),
                qkm=jnp.where(causal, qk * decay, 0.0).astype(BF16),
                rhs=jnp.concatenate([vh * bcol, kh * (bcol * egcol)], axis=-1),
                qd=(qh * egcol).astype(BF16),
                kd=(kh * jnp.exp(glast - gcol)).astype(BF16),
                glast=jnp.exp(glast)))
        return units

    def solve(units):
        ps = [u["x"] for u in units]
        tms = [eye + p for p in ps]
        for _ in range(n_sq):
            sp = [_split(p) for p in ps]
            ps = [_dot3(s, s) for s in sp]
            sp = [_split(p) for p in ps]
            tms = [tm + _dot3(_split(tm), s) for tm, s in zip(tms, sp)]
        return [_dot3(_split(tm), _split(u["rhs"])) for tm, u in zip(tms, units)]

    def update(u, sol):
        h, seq = u["h"], u["seq"]
        s = st_scr[seq, h]
        s_b = s.astype(BF16)
        v_new = sol[:, :dk] - _dot(sol[:, dk:].astype(BF16), s_b)
        vb = v_new.astype(BF16)
        o_ref[u["rs"], h * dk:(h + 1) * dk] = _dot(u["qd"], s_b) + _dot(u["qkm"], vb)
        st_scr[seq, h] = s * u["glast"] + _dot_tn(u["kd"], vb)

    def chunks(r0, n):
        units = []
        for seq in range(sb):
            for i in range(n):
                units += prepare(pl.ds(seq * ct + r0 + i * c, c), seq)
        for u, sol in zip(units, solve(units)):
            update(u, sol)

    n_chunks = ct // c
    per_iter = DELTA_CHUNKS_PER_ITER if n_chunks % DELTA_CHUNKS_PER_ITER == 0 else 1
    if n_chunks == per_iter:
        chunks(0, per_iter)
    else:
        def body(ci, carry):
            chunks(pl.multiple_of(ci * (per_iter * c), per_iter * c), per_iter)
            return carry
        lax.fori_loop(0, n_chunks // per_iter, body, 0)

    @pl.when(t == pl.num_programs(1) - 1)
    def _():
        s_ref[...] = st_scr[...]


def _delta_rule(qkv, gb, s0, *, nseq, t_len, ct, c, nh, dk):
    rows = nseq * t_len
    nt = t_len // ct
    w = nh * dk
    sb = DELTA_CHUNKS_PER_ITER if (nt == 1 and ct == c and nseq % DELTA_CHUNKS_PER_ITER == 0) else 1
    kern = functools.partial(_delta_kernel, sb=sb, ct=ct, c=c, nh=nh, dk=dk)
    return pl.pallas_call(
        kern,
        grid=(nseq // sb, nt),
        in_specs=[
            pl.BlockSpec((sb * ct, w), lambda s, t: (s * nt + t, 0)),
            pl.BlockSpec((sb * ct, w), lambda s, t: (s * nt + t, 1)),
            pl.BlockSpec((sb * ct, w), lambda s, t: (s * nt + t, 2)),
            pl.BlockSpec((sb * ct, LANES), lambda s, t: (s * nt + t, 0)),
            pl.BlockSpec((sb, nh, dk, dk), lambda s, t: (s, 0, 0, 0)),
        ],
        out_specs=[pl.BlockSpec((sb * ct, w), lambda s, t: (s * nt + t, 0)),
                   pl.BlockSpec((sb, nh, dk, dk), lambda s, t: (s, 0, 0, 0))],
        out_shape=[jax.ShapeDtypeStruct((rows, w), F32),
                   jax.ShapeDtypeStruct((nseq, nh, dk, dk), F32)],
        scratch_shapes=[pltpu.VMEM((sb, nh, dk, dk), F32)],
        compiler_params=_cparams(("parallel", "arbitrary")),
        name="delta_rule",
    )(qkv, qkv, qkv, gb, s0)


def _outproj_kernel(*refs, gated, nh, dv, alpha):
    if gated:
        o_ref, z_ref, nw_ref, x_ref, w_ref, g_ref, b_ref, out_ref, a_scr = refs
        for h in range(nh):
            cs = slice(h * dv, (h + 1) * dv)
            oh = o_ref[:, cs]
            zh = z_ref[:, cs]
            oh = oh * lax.rsqrt(jnp.mean(oh * oh, axis=-1, keepdims=True) + RMS_EPS) * nw_ref[...]
            a_scr[:, cs] = (oh * (zh * _sigmoid(zh))).astype(BF16)
        a = a_scr[...]
    else:
        o_ref, x_ref, w_ref, g_ref, b_ref, out_ref = refs
        a = o_ref[...].astype(BF16)
    v = alpha * x_ref[...] + _dot(a, w_ref[...])
    out_ref[...] = _layer_norm(v, g_ref[...], b_ref[...])


def _out_proj(o, x, w, g, b, *, tm, alpha, z_src=None, z_blk=0, norm_w=None, nh=1, dv=1):
    rows, d_in = o.shape
    d = x.shape[1]
    gated = z_src is not None
    kern = functools.partial(_outproj_kernel, gated=gated, nh=nh, dv=dv, alpha=alpha)
    row_spec = lambda width: pl.BlockSpec((tm, width), lambda i: (i, 0))
    const = lambda shape: pl.BlockSpec(shape, lambda i: (0, 0))
    in_specs = [row_spec(d_in)]
    args = [o]
    scratch = []
    if gated:
        in_specs += [pl.BlockSpec((tm, d_in), lambda i: (i, z_blk)), const((1, dv))]
        args += [z_src, norm_w]
        scratch = [pltpu.VMEM((tm, d_in), BF16)]
    in_specs += [row_spec(d), const((d_in, d)), const((1, d)), const((1, d))]
    args += [x, w, g, b]
    return pl.pallas_call(
        kern,
        grid=(rows // tm,),
        in_specs=in_specs,
        out_specs=row_spec(d),
        out_shape=jax.ShapeDtypeStruct((rows, d), F32),
        scratch_shapes=scratch,
        compiler_params=_cparams(("parallel",)),
        name="out_proj_ln",
    )(*args)


def _ffn_kernel(h_ref, wg_ref, wu_ref, wd_ref, g_ref, b_ref, out_ref, acc_ref, *, tf, alpha):
    hb = h_ref[...].astype(BF16)
    for c in range(wg_ref.shape[1] // tf):
        fs = slice(c * tf, (c + 1) * tf)
        gate = _dot(hb, wg_ref[:, fs])
        up = _dot(hb, wu_ref[:, fs])
        act = (gate * _sigmoid(gate) * up).astype(BF16)
        down = _dot(act, wd_ref[fs, :])
        if c == 0:
            acc_ref[...] = down
        else:
            acc_ref[...] += down
    v = alpha * h_ref[...] + acc_ref[...]
    out_ref[...] = _layer_norm(v, g_ref[...], b_ref[...])


def _ffn(h, wg, wu, wd, g, b, *, tm, tf, alpha):
    rows, d = h.shape
    f = wg.shape[1]
    kern = functools.partial(_ffn_kernel, tf=tf, alpha=alpha)
    resident = lambda shape: pl.BlockSpec(shape, lambda i: (0, 0), pipeline_mode=pl.Buffered(1))
    return pl.pallas_call(
        kern,
        grid=(rows // tm,),
        in_specs=[pl.BlockSpec((tm, d), lambda i: (i, 0)),
                  resident((d, f)), resident((d, f)), resident((f, d)),
                  pl.BlockSpec((1, d), lambda i: (0, 0)),
                  pl.BlockSpec((1, d), lambda i: (0, 0))],
        out_specs=pl.BlockSpec((tm, d), lambda i: (i, 0)),
        out_shape=jax.ShapeDtypeStruct((rows, d), F32),
        scratch_shapes=[pltpu.VMEM((tm, d), F32)],
        compiler_params=_cparams(("parallel",)),
        name="ffn_ln",
    )(h, wg, wu, wd, g, b)


def _qkv_kernel(x_ref, w_ref, cos_ref, sin_ref, q_ref, k_ref, v_ref, *t_refs, nq, nk, half):
    y = _dot(x_ref[...].astype(BF16), w_ref[...])
    cos = cos_ref[...]
    sin = sin_ref[...]
    lane = lax.broadcasted_iota(jnp.int32, cos.shape, 1)
    first = jnp.bitwise_and(lane, 2 * half - 1) < half

    def rope(xg):
        partner = jnp.where(first, pltpu.roll(xg, LANES - half, 1), pltpu.roll(xg, half, 1))
        return xg * cos + partner * sin

    for c in range(nq // LANES):
        q_ref[:, c * LANES:(c + 1) * LANES] = rope(y[:, c * LANES:(c + 1) * LANES])
    k = jnp.concatenate([rope(y[:, nq + c * LANES:nq + (c + 1) * LANES])
                         for c in range(nk // LANES)], axis=1)
    k_ref[...] = k
    v_ref[...] = y[:, nq + nk:]
    if t_refs:
        kt_ref, vt_ref = t_refs
        kt_ref[...] = k.T
        vt_ref[...] = y[:, nq + nk:].T


def _qkv_rope(x, w, cos, sin, *, tm, nq, nk, half, t_seq=None):
    rows, d = x.shape
    n_tab = cos.shape[0] // tm
    kern = functools.partial(_qkv_kernel, nq=nq, nk=nk, half=half)
    out_specs = [pl.BlockSpec((tm, nq), lambda i: (i, 0)),
                 pl.BlockSpec((tm, nk), lambda i: (i, 0)),
                 pl.BlockSpec((tm, nk), lambda i: (i, 0))]
    out_shape = [jax.ShapeDtypeStruct((rows, nq), F32),
                 jax.ShapeDtypeStruct((rows, nk), F32),
                 jax.ShapeDtypeStruct((rows, nk), F32)]
    if t_seq is not None:
        nt = t_seq // tm
        out_specs += [pl.BlockSpec((None, nk, tm), lambda i: (i // nt, 0, i % nt))] * 2
        out_shape += [jax.ShapeDtypeStruct((rows // t_seq, nk, t_seq), F32)] * 2
    return pl.pallas_call(
        kern,
        grid=(rows // tm,),
        in_specs=[pl.BlockSpec((tm, d), lambda i: (i, 0)),
                  pl.BlockSpec((d, nq + 2 * nk), lambda i: (0, 0)),
                  pl.BlockSpec((tm, LANES), lambda i: (i % n_tab, 0)),
                  pl.BlockSpec((tm, LANES), lambda i: (i % n_tab, 0))],
        out_specs=out_specs,
        out_shape=out_shape,
        compiler_params=_cparams(("parallel",)),
        name="qkv_rope",
    )(x, w, cos, sin)


def _kmeans_kernel(k_ref, m_ref, *, nblk):
    kb = k_ref[...].reshape(nblk, MOBA_BLOCK, k_ref.shape[1])
    m_ref[...] = jnp.sum(kb, axis=1) * (1.0 / MOBA_BLOCK)


def _block_means(k, *, nblk_step):
    rows, w = k.shape
    total = rows // MOBA_BLOCK
    return pl.pallas_call(
        functools.partial(_kmeans_kernel, nblk=nblk_step),
        grid=(total // nblk_step,),
        in_specs=[pl.BlockSpec((nblk_step * MOBA_BLOCK, w), lambda i: (i, 0))],
        out_specs=pl.BlockSpec((nblk_step, w), lambda i: (i, 0)),
        out_shape=jax.ShapeDtypeStruct((total, w), F32),
        compiler_params=_cparams(("parallel",)),
        name="block_means",
    )(k)


def _top_select(gate, n_past, axis):
    n = gate.shape[axis]
    idx = lax.broadcasted_iota(jnp.int32, gate.shape, axis).astype(F32)
    gm = jnp.where(idx < n_past, gate, NEG_INF)
    sel = jnp.zeros(gate.shape, F32)
    for r in range(MOBA_TOPK):
        mx = jnp.max(gm, axis=axis, keepdims=True)
        first = jnp.min(jnp.where(gm == mx, idx, float(n)), axis=axis, keepdims=True)
        pick = idx == first
        keep = jnp.where(r < n_past, 1.0, 0.0).astype(F32)
        sel = jnp.maximum(sel, jnp.where(pick, keep, 0.0))
        gm = jnp.where(pick, NEG_INF, gm)
    return sel


def _moba_prompt_kernel(qi_ref, ka_ref, kb_ref, last_ref, q_ref, ka, kb, vta, vtb, mean_ref, o_ref,
                        qs_scr, sel_scr, m_scr, l_scr, acc_scr, *, tq, ngrp, grp, hd, scale):
    p = pl.program_id(1)
    qi = qi_ref[p]
    kja = ka_ref[p]
    kjb = kb_ref[p]
    rows = grp * tq

    @pl.when(kja == qi)
    def _():
        qt = q_ref[...].T
        for g in range(ngrp):
            qg = jnp.concatenate(
                [qt[(g * grp + hh) * hd:(g * grp + hh + 1) * hd, :] for hh in range(grp)], axis=1)
            qs_scr[g] = (qg * (scale * LOG2E)).astype(BF16)
            gate = _dot(mean_ref[:, g * hd:(g + 1) * hd], qg, HI)
            sel_scr[g] = _top_select(gate, qi, 0)
        m_scr[...] = jnp.full(m_scr.shape, NEG_INF, F32)
        l_scr[...] = jnp.zeros(l_scr.shape, F32)
        acc_scr[...] = jnp.zeros(acc_scr.shape, F32)

    ones_rows = jnp.ones((2 * SUBLANES, MOBA_BLOCK), BF16)

    def attend(k_ref, vt_ref, kj, diagonal):
        kgs = [k_ref[:, g * hd:(g + 1) * hd].astype(BF16) for g in range(ngrp)]
        vgs = [jnp.concatenate([vt_ref[g * hd:(g + 1) * hd, :].astype(BF16), ones_rows], axis=0)
               for g in range(ngrp)]
        sel_rows = None if diagonal else [sel_scr[g, pl.ds(kj, 1), :] for g in range(ngrp)]
        units = [(g, slice(c * LANES, (c + 1) * LANES))
                 for g in range(ngrp) for c in range(rows // LANES)]
        for u0 in range(0, len(units), CHUNK_BATCH):
            attend_units(diagonal, kgs, vgs, sel_rows, units[u0:u0 + CHUNK_BATCH])

    def attend_units(diagonal, kgs, vgs, sel_rows, units):
        ss = [_dot(kgs[g], qs_scr[g, :, ls]) for g, ls in units]
        m_prevs = [m_scr[g, :, ls] for g, ls in units]
        if diagonal:
            kpos = lax.broadcasted_iota(jnp.int32, ss[0].shape, 0)
            lane = lax.broadcasted_iota(jnp.int32, ss[0].shape, 1)
            ss = [jnp.where(kpos <= jnp.bitwise_and(lane + ls.start, tq - 1), s, NEG_INF)
                  for s, (_, ls) in zip(ss, units)]
            m_news = [jnp.maximum(mp, jnp.max(s, axis=0, keepdims=True)) for mp, s in zip(m_prevs, ss)]
            m_refs = m_news
            alphas = [jnp.exp2(mp - mn) for mp, mn in zip(m_prevs, m_news)]
        else:
            picked = [sel_rows[g][:, ls] > 0.5 for g, ls in units]
            m_cand = [jnp.maximum(mp, jnp.max(s, axis=0, keepdims=True)) for mp, s in zip(m_prevs, ss)]
            m_news = [jnp.where(pk, mc, mp) for pk, mc, mp in zip(picked, m_cand, m_prevs)]
            m_refs = [jnp.where(pk, mc, POS_INF) for pk, mc in zip(picked, m_cand)]
            alphas = [jnp.exp2(mp - mn) for mp, mn in zip(m_prevs, m_news)]
        ps = [jnp.exp2(s - mr).astype(BF16) for s, mr in zip(ss, m_refs)]
        pvs = [_dot(vgs[g], p) for (g, _), p in zip(units, ps)]
        for (g, ls), al, mn, pv in zip(units, alphas, m_news, pvs):
            acc_scr[g, :, ls] = al * acc_scr[g, :, ls] + pv[:hd, :]
            l_scr[g, :, ls] = al * l_scr[g, :, ls] + pv[hd:hd + 1, :]
            m_scr[g, :, ls] = mn

    @pl.when(kja == qi)
    def _():
        attend(ka, vta, kja, True)

    @pl.when(kja < qi)
    def _():
        attend(ka, vta, kja, False)

    @pl.when(kjb < qi)
    def _():
        attend(kb, vtb, kjb, False)

    @pl.when(last_ref[p] == 1)
    def _():
        outs = []
        for g in range(ngrp):
            og = acc_scr[g] / l_scr[g]
            outs += [og[:, hh * tq:(hh + 1) * tq] for hh in range(grp)]
        o_ref[...] = jnp.concatenate(outs, axis=0).T


def _moba_prompt(q, k, vt, means, *, nb_batch, t_len, ngrp, grp, hd):
    tq = MOBA_BLOCK
    nq = t_len // tq
    nb = t_len // MOBA_BLOCK
    steps = []
    for i in range(nq):
        steps.append((i, i, i, int(i == 0)))
        for j in range(0, i, 2):
            steps.append((i, j, min(j + 1, i), int(j + 2 >= i)))
    tabs = [jnp.asarray(np.array([st[c] for st in steps], np.int32)) for c in range(4)]
    rows = grp * tq
    w = ngrp * grp * hd
    kvw = ngrp * hd
    kern = functools.partial(_moba_prompt_kernel, tq=tq, ngrp=ngrp, grp=grp, hd=hd,
                             scale=float(hd) ** -0.5)
    grid_spec = pltpu.PrefetchScalarGridSpec(
        num_scalar_prefetch=4,
        grid=(nb_batch, len(steps)),
        in_specs=[
            pl.BlockSpec((tq, w), lambda b, p, qt, ka, kb, la: (b * nq + qt[p], 0)),
            pl.BlockSpec((MOBA_BLOCK, kvw), lambda b, p, qt, ka, kb, la: (b * nb + ka[p], 0)),
            pl.BlockSpec((MOBA_BLOCK, kvw), lambda b, p, qt, ka, kb, la: (b * nb + kb[p], 0)),
            pl.BlockSpec((None, kvw, MOBA_BLOCK), lambda b, p, qt, ka, kb, la: (b, 0, ka[p])),
            pl.BlockSpec((None, kvw, MOBA_BLOCK), lambda b, p, qt, ka, kb, la: (b, 0, kb[p])),
            pl.BlockSpec((nb, kvw), lambda b, p, qt, ka, kb, la: (b, 0)),
        ],
        out_specs=pl.BlockSpec((tq, w), lambda b, p, qt, ka, kb, la: (b * nq + qt[p], 0)),
        scratch_shapes=[pltpu.VMEM((ngrp, hd, rows), BF16),
                        pltpu.VMEM((ngrp, nb, rows), F32),
                        pltpu.VMEM((ngrp, 1, rows), F32),
                        pltpu.VMEM((ngrp, 1, rows), F32),
                        pltpu.VMEM((ngrp, hd, rows), F32)],
    )
    return pl.pallas_call(
        kern,
        grid_spec=grid_spec,
        out_shape=jax.ShapeDtypeStruct(q.shape, F32),
        compiler_params=_cparams(("parallel", "arbitrary")),
        name="moba_prompt",
    )(*tabs, q, k, k, vt, vt, means)


def _sample_partial_kernel(pt_ref, q_ref, *refs, npg, ngrp, grp, hd, scale):
    k_pages = refs[:npg]
    v_pages = refs[npg:2 * npg]
    mean_ref, m_ref, l_ref, po_ref, qs_scr = refs[2 * npg:]
    step = pl.program_id(1)
    t_new = q_ref.shape[0]
    gr = grp * t_new

    @pl.when(step == 0)
    def _():
        for g in range(ngrp):
            qg = jnp.concatenate(
                [q_ref[:, (g * grp + hh) * hd:(g * grp + hh + 1) * hd] for hh in range(grp)], axis=0)
            qs_scr[g] = (qg * scale).astype(BF16)
        mean_ref[...] = jnp.zeros(mean_ref.shape, F32)
        m_ref[...] = jnp.zeros(m_ref.shape, F32)
        l_ref[...] = jnp.zeros(l_ref.shape, F32)

    ppb = MOBA_BLOCK // k_pages[0].shape[-1]
    bps = npg // ppb
    lane = lax.broadcasted_iota(jnp.int32, m_ref.shape, 1)
    lane_hd = lax.broadcasted_iota(jnp.int32, (hd, LANES), 1)
    blocks = [slice(blk * ppb, (blk + 1) * ppb) for blk in range(bps)]
    kts = [[jnp.concatenate([r[g] for r in k_pages[sl]], axis=1) for g in range(ngrp)]
           for sl in blocks]
    scs = [jnp.concatenate([_dot(qs_scr[g], kt[g].astype(BF16)) for g in range(ngrp)], axis=0)
           for kt in kts]
    ms = [jnp.max(sc, axis=1, keepdims=True) for sc in scs]
    prs = [jnp.exp(sc - m) for sc, m in zip(scs, ms)]
    ls = [jnp.sum(pr, axis=1, keepdims=True) for pr in prs]
    outs = []
    for sl, pr in zip(blocks, prs):
        prb = pr.astype(BF16)
        outs.append(jnp.concatenate(
            [_dot_nt(prb[g * gr:(g + 1) * gr, :],
                     jnp.concatenate([r[g] for r in v_pages[sl]], axis=1).astype(BF16))
             for g in range(ngrp)], axis=0))
    m_tile = m_ref[...]
    l_tile = l_ref[...]
    for blk in range(bps):
        bidx = step * bps + blk
        m_tile = jnp.where(lane == bidx, ms[blk], m_tile)
        l_tile = jnp.where(lane == bidx, ls[blk], l_tile)
        for g in range(ngrp):
            mean_col = jnp.sum(kts[blk][g], axis=1, keepdims=True) * (1.0 / MOBA_BLOCK)
            rs = slice(g * hd, (g + 1) * hd)
            mean_ref[rs, :] = jnp.where(lane_hd == bidx, mean_col, mean_ref[rs, :])
    m_ref[...] = m_tile
    l_ref[...] = l_tile
    ppl = LANES // hd
    for j in range(bps // ppl):
        po_ref[j] = jnp.concatenate(outs[j * ppl:(j + 1) * ppl], axis=1)


def _sample_partials(q, cache_k, cache_v, page_table, *, npg, ngrp, grp, hd):
    db, n_pages = page_table.shape
    page = cache_k.shape[-1]
    kvw = ngrp * hd
    t_new = q.shape[0] // db
    ppb = MOBA_BLOCK // page
    bps = npg // ppb
    ppl = LANES // hd
    nsteps = n_pages // npg
    nblk = n_pages // ppb
    nrow = ngrp * grp * t_new
    assert nblk <= LANES and bps % ppl == 0
    kern = functools.partial(_sample_partial_kernel, npg=npg, ngrp=ngrp, grp=grp, hd=hd,
                             scale=float(hd) ** -0.5)

    def page_spec(i):
        return pl.BlockSpec((None, ngrp, hd, page),
                            lambda b, s, pt: (pt[b * n_pages + s * npg + i], 0, 0, 0))

    grid_spec = pltpu.PrefetchScalarGridSpec(
        num_scalar_prefetch=1,
        grid=(db, nsteps),
        in_specs=[pl.BlockSpec((t_new, q.shape[1]), lambda b, s, pt: (b, 0))]
        + [page_spec(i) for i in range(npg)] + [page_spec(i) for i in range(npg)],
        out_specs=[pl.BlockSpec((None, kvw, LANES), lambda b, s, pt: (b, 0, 0)),
                   pl.BlockSpec((None, nrow, LANES), lambda b, s, pt: (b, 0, 0)),
                   pl.BlockSpec((None, nrow, LANES), lambda b, s, pt: (b, 0, 0)),
                   pl.BlockSpec((None, bps // ppl, nrow, LANES), lambda b, s, pt: (b, s, 0, 0))],
        scratch_shapes=[pltpu.VMEM((ngrp, grp * t_new, hd), BF16)],
    )
    return pl.pallas_call(
        kern,
        grid_spec=grid_spec,
        out_shape=[jax.ShapeDtypeStruct((db, kvw, LANES), F32),
                   jax.ShapeDtypeStruct((db, nrow, LANES), F32),
                   jax.ShapeDtypeStruct((db, nrow, LANES), F32),
                   jax.ShapeDtypeStruct((db, nblk // ppl, nrow, LANES), F32)],
        compiler_params=_cparams(("parallel", "arbitrary")),
        name="moba_sample_partials",
    )(page_table.reshape(-1), q, *([cache_k] * npg), *([cache_v] * npg))


def _sample_combine_kernel(q_ref, kn_ref, vn_ref, mean_ref, m_ref, l_ref, po_ref, o_ref,
                           *, nblk, ngrp, grp, hd, scale):
    t_new = q_ref.shape[0]
    gr = grp * t_new
    nrow = ngrp * gr
    gates, qss = [], []
    for g in range(ngrp):
        qg = jnp.concatenate(
            [q_ref[:, (g * grp + hh) * hd:(g * grp + hh + 1) * hd] for hh in range(grp)], axis=0)
        qss.append((qg * scale).astype(BF16))
        gates.append(_dot(qg, mean_ref[g * hd:(g + 1) * hd, :], HI))
    sel = _top_select(jnp.concatenate(gates, axis=0), nblk, 1) > 0.5
    m_all = m_ref[...]
    m_past = jnp.max(jnp.where(sel, m_all, NEG_INF), axis=1, keepdims=True)
    kpos = lax.broadcasted_iota(jnp.int32, (nrow, t_new), 1)
    qpos = jnp.bitwise_and(lax.broadcasted_iota(jnp.int32, (nrow, t_new), 0), t_new - 1)
    kn = kn_ref[...].astype(BF16)
    vn = vn_ref[...].astype(BF16)
    s_own = jnp.concatenate(
        [_dot_nt(qss[g], kn[:, g * hd:(g + 1) * hd]) for g in range(ngrp)], axis=0)
    s_own = jnp.where(kpos <= qpos, s_own, NEG_INF)
    m_fin = jnp.maximum(m_past, jnp.max(s_own, axis=1, keepdims=True))
    w = jnp.where(sel, jnp.exp(m_all - m_fin), 0.0)
    p_own = jnp.exp(s_own - m_fin)
    l_fin = (jnp.sum(w * l_ref[...], axis=1, keepdims=True)
             + jnp.sum(p_own, axis=1, keepdims=True))
    pb = p_own.astype(BF16)
    o_own = jnp.concatenate(
        [_dot(pb[g * gr:(g + 1) * gr, :], vn[:, g * hd:(g + 1) * hd]) for g in range(ngrp)], axis=0)
    ppl = LANES // hd
    lane = lax.broadcasted_iota(jnp.int32, (nrow, LANES), 1)
    acc = jnp.zeros((nrow, LANES), F32)
    for j in range(nblk // ppl):
        wj = w[:, j * ppl:j * ppl + 1]
        for i in range(1, ppl):
            wj = jnp.where(lane < i * hd, wj, w[:, j * ppl + i:j * ppl + i + 1])
        acc = acc + wj * po_ref[j]
    o = o_own
    for i in range(ppl):
        o = o + acc[:, i * hd:(i + 1) * hd]
    o = o / l_fin
    o_ref[...] = jnp.concatenate(
        [o[r * t_new:(r + 1) * t_new, :] for r in range(ngrp * grp)], axis=1)


def _sample_combine(q, k_new, v_new, means_t, m_part, l_part, po, *, db, nblk, ngrp, grp, hd):
    t_new = q.shape[0] // db
    nrow = ngrp * grp * t_new
    kvw = ngrp * hd
    kern = functools.partial(_sample_combine_kernel, nblk=nblk, ngrp=ngrp, grp=grp, hd=hd,
                             scale=float(hd) ** -0.5)
    return pl.pallas_call(
        kern,
        grid=(db,),
        in_specs=[pl.BlockSpec((t_new, q.shape[1]), lambda b: (b, 0)),
                  pl.BlockSpec((t_new, kvw), lambda b: (b, 0)),
                  pl.BlockSpec((t_new, kvw), lambda b: (b, 0)),
                  pl.BlockSpec((None, kvw, LANES), lambda b: (b, 0, 0)),
                  pl.BlockSpec((None, nrow, LANES), lambda b: (b, 0, 0)),
                  pl.BlockSpec((None, nrow, LANES), lambda b: (b, 0, 0)),
                  pl.BlockSpec((None,) + po.shape[1:], lambda b: (b, 0, 0, 0))],
        out_specs=pl.BlockSpec((t_new, q.shape[1]), lambda b: (b, 0)),
        out_shape=jax.ShapeDtypeStruct(q.shape, F32),
        compiler_params=_cparams(("parallel",)),
        name="moba_sample_combine",
    )(q, k_new, v_new, means_t, m_part, l_part, po)


def _rope_tables(pos, hd, reps):
    half = hd // 2
    inv_freq = ROPE_THETA ** (-jnp.arange(half, dtype=F32) / half)
    ang = pos.astype(F32)[:, None] * inv_freq[None, :]
    cos = jnp.cos(ang)
    sin = jnp.sin(ang)
    per_vreg = LANES // hd
    cos_t = jnp.tile(jnp.concatenate([cos, cos], axis=1), (reps, per_vreg))
    sin_t = jnp.tile(jnp.concatenate([-sin, sin], axis=1), (reps, per_vreg))
    return cos_t, sin_t


def _trunk(x, pos, conv_state8, delta_state, attend, wts, *, nseq, t_len, tm, prep_tt,
           delta_ct, delta_c, dims, want_t=False):
    nh, dk, conv_dim, n_a, depth, alpha, bh, kvh, hd = dims
    rows = nseq * t_len
    new_conv, new_delta = [], []
    k_sh = v_sh = t_sh = None
    for layer in range(depth):
        if layer < n_a:
            cw = wts["a_conv"][layer].shape[0]
            prep_args = (conv_state8[layer], wts["a_conv"][layer], wts["a_log"][layer],
                         wts["a_dtb"][layer])
            prep_kw = dict(nseq=nseq, t_len=t_len, tt=prep_tt, conv_dim=conv_dim, nh=nh, dk=dk)
            if t_len >= tm:
                qkv, gb, z_src, tail = _gdn_in_fused(x, wts["a_w_in"][layer], *prep_args, **prep_kw)
                z_blk = 0
                new_conv.append(tail[:, SUBLANES - (cw - 1):, :])
            else:
                proj = _matmul(x, wts["a_w_in"][layer], rows, wts["a_w_in"][layer].shape[1] // 3,
                               "in_proj")
                qkv, gb = _gdn_prep(proj, *prep_args, **prep_kw)
                z_src, z_blk = proj, conv_dim // (nh * dk)
                new_conv.append(proj.reshape(nseq, t_len, -1)[:, t_len - (cw - 1):, :conv_dim])
            o, s_new = _delta_rule(qkv, gb, delta_state[layer], nseq=nseq, t_len=t_len,
                                   ct=delta_ct, c=delta_c, nh=nh, dk=dk)
            new_delta.append(s_new)
            h = _out_proj(o, x, wts["a_w_out"][layer], wts["ln_g"][layer][0], wts["ln_b"][layer][0],
                          tm=tm, alpha=alpha, z_src=z_src, z_blk=z_blk,
                          norm_w=wts["a_norm"][layer], nh=nh, dv=dk)
        else:
            j = layer - n_a
            w_all = wts["b_w_qkv"][j]
            cos_t, sin_t = _rope_tables(pos, hd, (tm // t_len) if tm > t_len else 1)
            q, k_l, v_l, *t_l = _qkv_rope(x, w_all, cos_t, sin_t, tm=tm, nq=bh * hd, nk=kvh * hd,
                                          half=hd // 2, t_seq=t_len if want_t else None)
            if layer == n_a:
                k_sh, v_sh, t_sh = k_l, v_l, t_l
            att = attend(q, k_sh, v_sh, *t_sh)
            h = _out_proj(att, x, wts["b_w_o"][j], wts["ln_g"][layer][0], wts["ln_b"][layer][0],
                          tm=tm, alpha=alpha)
        x = _ffn(h, wts["ffn_wg"][layer], wts["ffn_wu"][layer], wts["ffn_wd"][layer],
                 wts["ln_g"][layer][1], wts["ln_b"][layer][1], tm=tm, tf=256, alpha=alpha)
    return x, jnp.stack(new_conv), jnp.stack(new_delta), k_sh, v_sh, t_sh


def kernel(x_prompt, x_sample, state_conv, state_delta, cache_k, cache_v, page_table, ln_g, ln_b,
           a_w_in, a_conv, a_log_decay, a_dt_bias, a_norm, a_w_out, kv_w_k, kv_w_v, b_w_q, b_w_o,
           ffn_w_gate, ffn_w_up, ffn_w_down):
    bp, tp, d = x_prompt.shape
    db, ts, _ = x_sample.shape
    depth = ln_g.shape[0]
    n_a = a_w_in.shape[0]
    nh = a_log_decay.shape[1]
    dk = state_delta.shape[-2]
    conv_dim = a_conv.shape[-1]
    cw = a_conv.shape[1]
    n_pool, page, kvh, hd = cache_k.shape
    bh = b_w_q.shape[-1] // hd
    grp = bh // kvh
    n_pages = page_table.shape[1]
    past = n_pages * page
    alpha = (2.0 * depth) ** 0.25
    assert past % MOBA_BLOCK == 0 and ts <= MOBA_BLOCK and ts >= cw - 1 and tp >= cw - 1
    assert state_delta.shape[-1] == dk and conv_dim == 3 * nh * dk

    in_w = a_w_in.shape[-1]
    in_pad = -(-in_w // (3 * LANES)) * (3 * LANES)
    pad_lanes = lambda v: jnp.pad(v, ((0, 0), (0, LANES - v.shape[-1])))[:, None, :]
    wts = {
        "a_w_in": jnp.pad(a_w_in, ((0, 0), (0, 0), (0, in_pad - in_w))).astype(BF16),
        "a_conv": a_conv,
        "a_log": pad_lanes(a_log_decay),
        "a_dtb": pad_lanes(a_dt_bias),
        "a_norm": a_norm[:, None, :],
        "a_w_out": a_w_out.astype(BF16),
        "b_w_qkv": jnp.concatenate(
            [b_w_q, jnp.broadcast_to(kv_w_k, (b_w_q.shape[0],) + kv_w_k.shape),
             jnp.broadcast_to(kv_w_v, (b_w_q.shape[0],) + kv_w_v.shape)], axis=-1).astype(BF16),
        "b_w_o": b_w_o.astype(BF16),
        "ffn_wg": ffn_w_gate.astype(BF16),
        "ffn_wu": ffn_w_up.astype(BF16),
        "ffn_wd": ffn_w_down.astype(BF16),
        "ln_g": ln_g[:, :, None, :],
        "ln_b": ln_b[:, :, None, :],
    }
    dims = (nh, dk, conv_dim, n_a, depth, alpha, bh, kvh, hd)

    nb = tp // MOBA_BLOCK

    def attend_prompt(q, k, v, kt, vt):
        total = bp * nb
        step = SUBLANES if total % SUBLANES == 0 else total
        means = _block_means(k, nblk_step=step)
        return _moba_prompt(q, k, vt, means, nb_batch=bp, t_len=tp, ngrp=kvh, grp=grp, hd=hd)

    tm_p = 512 if (bp * tp) % 512 == 0 else 256
    y_p, p_conv, p_delta, _, _, (p_kt, p_vt) = _trunk(
        x_prompt.reshape(bp * tp, d), jnp.arange(tp, dtype=jnp.int32),
        jnp.zeros((n_a, bp, SUBLANES, conv_dim), F32), jnp.zeros((n_a, bp, nh, dk, dk), F32),
        attend_prompt, wts, nseq=bp, t_len=tp, tm=tm_p, prep_tt=tm_p, delta_ct=256,
        delta_c=DELTA_CHUNK, dims=dims, want_t=True)
    p_k = jnp.transpose(p_kt.reshape(bp, kvh, hd, tp), (0, 3, 1, 2))
    p_v = jnp.transpose(p_vt.reshape(bp, kvh, hd, tp), (0, 3, 1, 2))

    ck = jnp.transpose(cache_k, (0, 2, 3, 1))
    cv = jnp.transpose(cache_v, (0, 2, 3, 1))
    npg = next(n for n in (16, 8, 2 * (MOBA_BLOCK // page)) if n_pages % n == 0)

    def attend_sample(q, k_new, v_new):
        means_t, m_part, l_part, po = _sample_partials(q, ck, cv, page_table, npg=npg, ngrp=kvh,
                                                       grp=grp, hd=hd)
        return _sample_combine(q, k_new, v_new, means_t, m_part, l_part, po, db=db,
                               nblk=past // MOBA_BLOCK, ngrp=kvh, grp=grp, hd=hd)

    conv8 = jnp.pad(state_conv, ((0, 0), (0, 0), (SUBLANES - (cw - 1), 0), (0, 0)))
    y_s, s_conv, s_delta, s_k, s_v, _ = _trunk(
        x_sample.reshape(db * ts, d), past + jnp.arange(ts, dtype=jnp.int32), conv8, state_delta,
        attend_sample, wts, nseq=db, t_len=ts, tm=db * ts, prep_tt=ts, delta_ct=ts, delta_c=ts,
        dims=dims)

    return (y_p.reshape(bp, tp, d), y_s.reshape(db, ts, d), p_conv, p_delta, p_k, p_v,
            s_conv, s_delta, s_k.reshape(db, ts, kvh, hd), s_v.reshape(db, ts, kvh, hd))
```

```python
import functools

import numpy as np
import jax
import jax.numpy as jnp
from jax import lax
from jax.experimental import pallas as pl
from jax.experimental.pallas import tpu as pltpu

F32 = jnp.float32
BF16 = jnp.bfloat16
HI = lax.Precision.HIGHEST

LANES = 128
SUBLANES = 8
VMEM_LIMIT_MB = 56

MOBA_BLOCK = 256
MOBA_TOPK = 3
UNIT_LANES = 256
SKEW_S = 3
SKEW_P = 2
DELTA_CHUNK = 64
DELTA_CHUNKS_PER_ITER = 4
ROPE_THETA = 10000.0
LN_EPS = 1e-5
RMS_EPS = 1e-6
L2_EPS = 1e-6
NEG_INF = float("-inf")
POS_INF = float("inf")
LOG2E = 1.4426950408889634


def _cparams(sem):
    return pltpu.CompilerParams(dimension_semantics=sem,
                                vmem_limit_bytes=VMEM_LIMIT_MB * 1024 * 1024)


def _dot(a, b, prec=None):
    return jnp.dot(a, b, precision=prec, preferred_element_type=F32)


def _dot_nt(a, b, prec=None):
    return lax.dot_general(a, b, (((1,), (1,)), ((), ())), precision=prec,
                           preferred_element_type=F32)


def _dot_tn(a, b, prec=None):
    return lax.dot_general(a, b, (((0,), (0,)), ((), ())), precision=prec,
                           preferred_element_type=F32)


def _split(a):
    hi = a.astype(BF16)
    return hi, (a - hi.astype(F32)).astype(BF16)


def _dot3(a, b):
    return _dot(a[0], b[0]) + (_dot(a[0], b[1]) + _dot(a[1], b[0]))


def _sigmoid(x):
    return 1.0 / (1.0 + jnp.exp(-x))


def _layer_norm(v, g, b):
    mu = jnp.mean(v, axis=-1, keepdims=True)
    d = v - mu
    var = jnp.mean(d * d, axis=-1, keepdims=True)
    return d * lax.rsqrt(var + LN_EPS) * g + b


def _mm_kernel(x_ref, w_ref, o_ref):
    o_ref[...] = _dot(x_ref[...].astype(BF16), w_ref[...])


def _matmul(x, w, tm, tn, name):
    r, k = x.shape
    n = w.shape[1]
    return pl.pallas_call(
        _mm_kernel,
        grid=(r // tm, n // tn),
        in_specs=[pl.BlockSpec((tm, k), lambda i, j: (i, 0)),
                  pl.BlockSpec((k, tn), lambda i, j: (0, j))],
        out_specs=pl.BlockSpec((tm, tn), lambda i, j: (i, j)),
        out_shape=jax.ShapeDtypeStruct((r, n), F32),
        compiler_params=_cparams(("parallel", "parallel")),
        name=name,
    )(x, w)


def _conv_norm(xp_ref, cw_ref, qkv_ref, cols, *, tt, conv_w, n_q, n_qk, q_scale):
    base = SUBLANES - (conv_w - 1)
    for c in cols:
        cs = slice(c * LANES, (c + 1) * LANES)
        y = xp_ref[base:base + tt, cs] * cw_ref[0:1, cs]
        for i in range(1, conv_w):
            y = y + xp_ref[base + i:base + i + tt, cs] * cw_ref[i:i + 1, cs]
        y = y * _sigmoid(y)
        if c < n_qk:
            y = y * lax.rsqrt(jnp.sum(y * y, axis=-1, keepdims=True) + L2_EPS)
            if c < n_q:
                y = y * q_scale
        qkv_ref[:, cs] = y


def _decay_beta(ab, alog, dtb, nh):
    lane = lax.broadcasted_iota(jnp.int32, ab.shape, 1)
    sp = ab + dtb
    softplus = jnp.maximum(sp, 0.0) + jnp.log1p(jnp.exp(-jnp.abs(sp)))
    return jnp.where(lane < nh, -jnp.exp(alog) * softplus, _sigmoid(ab))


def _prep_kernel(cur_ref, prev_ref, st_ref, ab_ref, cw_ref, alog_ref, dtb_ref,
                 qkv_ref, gb_ref, xp_ref, *, tt, conv_w, n_q, n_qk, nh, q_scale):
    t = pl.program_id(1)
    hist = SUBLANES

    @pl.when(t == 0)
    def _():
        xp_ref[0:hist, :] = st_ref[...]

    @pl.when(t != 0)
    def _():
        xp_ref[0:hist, :] = prev_ref[...]

    xp_ref[hist:hist + tt, :] = cur_ref[...]
    _conv_norm(xp_ref, cw_ref, qkv_ref, range(cur_ref.shape[1] // LANES), tt=tt, conv_w=conv_w,
               n_q=n_q, n_qk=n_qk, q_scale=q_scale)
    gb_ref[...] = _decay_beta(ab_ref[...], alog_ref[...], dtb_ref[...], nh)


def _gdn_in_kernel(x_ref, w_ref, st_ref, cw_ref, alog_ref, dtb_ref, qkv_ref, gb_ref, z_ref, tail_ref,
                   xp_ref, *, tt, conv_dim, zw, conv_w, n_q, n_qk, nh, q_scale):
    hist = SUBLANES

    @pl.when(pl.program_id(1) == 0)
    def _():
        xp_ref[0:hist, :] = st_ref[...]

    xb = x_ref[...].astype(BF16)
    mxu_w = 2 * LANES
    for c in range(conv_dim // mxu_w):
        cs = slice(c * mxu_w, (c + 1) * mxu_w)
        xp_ref[hist:hist + tt, cs] = _dot(xb, w_ref[:, cs])
        _conv_norm(xp_ref, cw_ref, qkv_ref, range(2 * c, 2 * c + 2), tt=tt, conv_w=conv_w,
                   n_q=n_q, n_qk=n_qk, q_scale=q_scale)
    z_ref[...] = _dot(xb, w_ref[:, conv_dim:conv_dim + zw])
    ab = _dot(xb, w_ref[:, conv_dim + zw:conv_dim + zw + LANES])
    gb_ref[...] = _decay_beta(ab, alog_ref[...], dtb_ref[...], nh)
    tail = xp_ref[tt:tt + hist, :]
    tail_ref[...] = tail
    xp_ref[0:hist, :] = tail


def _gdn_in_fused(x, w, state8, conv_w, alog, dtb, *, nseq, t_len, tt, conv_dim, nh, dk):
    rows, d = x.shape
    nt = t_len // tt
    cw = conv_w.shape[0]
    zw = nh * dk
    kern = functools.partial(_gdn_in_kernel, tt=tt, conv_dim=conv_dim, zw=zw, conv_w=cw, n_q=nh,
                             n_qk=2 * nh, nh=nh, q_scale=float(dk) ** -0.5)
    row = lambda width: pl.BlockSpec((tt, width), lambda s, t: (s * nt + t, 0))
    const = lambda shape: pl.BlockSpec(shape, lambda s, t: (0, 0))
    per_seq = pl.BlockSpec((None, SUBLANES, conv_dim), lambda s, t: (s, 0, 0))
    return pl.pallas_call(
        kern,
        grid=(nseq, nt),
        in_specs=[row(d),
                  pl.BlockSpec(w.shape, lambda s, t: (0, 0), pipeline_mode=pl.Buffered(1)),
                  per_seq, const((cw, conv_dim)), const((1, LANES)), const((1, LANES))],
        out_specs=[row(conv_dim), row(LANES), row(zw), per_seq],
        out_shape=[jax.ShapeDtypeStruct((rows, conv_dim), F32),
                   jax.ShapeDtypeStruct((rows, LANES), F32),
                   jax.ShapeDtypeStruct((rows, zw), F32),
                   jax.ShapeDtypeStruct((nseq, SUBLANES, conv_dim), F32)],
        scratch_shapes=[pltpu.VMEM((tt + SUBLANES, conv_dim), F32)],
        compiler_params=_cparams(("parallel", "arbitrary")),
        name="gdn_in_fused",
    )(x, w, state8, conv_w, alog, dtb)


def _gdn_prep(proj, state8, conv_w, alog, dtb, *, nseq, t_len, tt, conv_dim, nh, dk):
    rows = nseq * t_len
    nt = t_len // tt
    cw = conv_w.shape[0]
    ab_blk = (conv_dim + nh * dk) // LANES
    kern = functools.partial(_prep_kernel, tt=tt, conv_w=cw, n_q=nh, n_qk=2 * nh, nh=nh,
                             q_scale=float(dk) ** -0.5)
    tpb = tt // SUBLANES
    return pl.pallas_call(
        kern,
        grid=(nseq, nt),
        in_specs=[
            pl.BlockSpec((tt, conv_dim), lambda s, t: (s * nt + t, 0)),
            pl.BlockSpec((SUBLANES, conv_dim),
                         lambda s, t: (jnp.maximum((s * nt + t) * tpb - 1, 0), 0)),
            pl.BlockSpec((None, SUBLANES, conv_dim), lambda s, t: (s, 0, 0)),
            pl.BlockSpec((tt, LANES), lambda s, t: (s * nt + t, ab_blk)),
            pl.BlockSpec((cw, conv_dim), lambda s, t: (0, 0)),
            pl.BlockSpec((1, LANES), lambda s, t: (0, 0)),
            pl.BlockSpec((1, LANES), lambda s, t: (0, 0)),
        ],
        out_specs=[pl.BlockSpec((tt, conv_dim), lambda s, t: (s * nt + t, 0)),
                   pl.BlockSpec((tt, LANES), lambda s, t: (s * nt + t, 0))],
        out_shape=[jax.ShapeDtypeStruct((rows, conv_dim), F32),
                   jax.ShapeDtypeStruct((rows, LANES), F32)],
        scratch_shapes=[pltpu.VMEM((tt + SUBLANES, conv_dim), F32)],
        compiler_params=_cparams(("parallel", "arbitrary")),
        name="gdn_prep",
    )(proj, proj, state8, proj, conv_w, alog, dtb)


def _delta_kernel(q_ref, k_ref, v_ref, gb_ref, s0_ref, o_ref, s_ref, st_scr, *, sb, ct, c, nh, dk):
    t = pl.program_id(1)

    @pl.when(t == 0)
    def _():
        st_scr[...] = s0_ref[...]

    row = lax.broadcasted_iota(jnp.int32, (c, c), 0)
    col = lax.broadcasted_iota(jnp.int32, (c, c), 1)
    causal = row >= col
    strict = row > col
    tril = causal.astype(F32)
    eye = (row == col).astype(F32)
    n_sq = max(int(np.ceil(np.log2(c))) - 1, 0)

    def prepare(rs, seq):
        gbc = gb_ref[rs, :]
        gc = _dot(tril, gbc, HI)
        gt = gc.T
        eg = jnp.exp(gc)
        units = []
        for h in range(nh):
            cs = slice(h * dk, (h + 1) * dk)
            qh = q_ref[rs, cs]
            kh = k_ref[rs, cs]
            vh = v_ref[rs, cs]
            gcol = gc[:, h:h + 1]
            grow = gt[h:h + 1, :]
            bcol = gbc[:, nh + h:nh + h + 1]
            egcol = eg[:, h:h + 1]
            decay = jnp.where(causal, jnp.exp(jnp.where(causal, gcol - grow, 0.0)), 0.0)
            kb = kh.astype(BF16)
            kk = _dot_nt(kb, kb)
            qk = _dot_nt(qh.astype(BF16), kb)
            glast = grow[:, c - 1:c]
            units.append(dict(
                rs=rs, h=h, seq=seq,
                x=-jnp.where(strict, bcol * kk * decay, 0.0),
                qkm=jnp.where(causal, qk * decay, 0.0).astype(BF16),
                rhs=jnp.concatenate([vh * bcol, kh * (bcol * egcol)], axis=-1),
                qd=(qh * egcol).astype(BF16),
                kd=(kh * jnp.exp(glast - gcol)).astype(BF16),
                glast=jnp.exp(glast)))
        return units

    def solve(units):
        ps = [u["x"] for u in units]
        tms = [eye + p for p in ps]
        for _ in range(n_sq):
            sp = [_split(p) for p in ps]
            ps = [_dot3(s, s) for s in sp]
            sp = [_split(p) for p in ps]
            tms = [tm + _dot3(_split(tm), s) for tm, s in zip(tms, sp)]
        return [_dot3(_split(tm), _split(u["rhs"])) for tm, u in zip(tms, units)]

    def update(u, sol):
        h, seq = u["h"], u["seq"]
        s = st_scr[seq, h]
        s_b = s.astype(BF16)
        v_new = sol[:, :dk] - _dot(sol[:, dk:].astype(BF16), s_b)
        vb = v_new.astype(BF16)
        o_ref[u["rs"], h * dk:(h + 1) * dk] = _dot(u["qd"], s_b) + _dot(u["qkm"], vb)
        st_scr[seq, h] = s * u["glast"] + _dot_tn(u["kd"], vb)

    def chunks(r0, n):
        units = []
        for seq in range(sb):
            for i in range(n):
                units += prepare(pl.ds(seq * ct + r0 + i * c, c), seq)
        for u, sol in zip(units, solve(units)):
            update(u, sol)

    n_chunks = ct // c
    per_iter = DELTA_CHUNKS_PER_ITER if n_chunks % DELTA_CHUNKS_PER_ITER == 0 else 1
    if n_chunks == per_iter:
        chunks(0, per_iter)
    else:
        def body(ci, carry):
            chunks(pl.multiple_of(ci * (per_iter * c), per_iter * c), per_iter)
            return carry
        lax.fori_loop(0, n_chunks // per_iter, body, 0)

    @pl.when(t == pl.num_programs(1) - 1)
    def _():
        s_ref[...] = st_scr[...]


def _delta_rule(qkv, gb, s0, *, nseq, t_len, ct, c, nh, dk):
    rows = nseq * t_len
    nt = t_len // ct
    w = nh * dk
    sb = DELTA_CHUNKS_PER_ITER if (nt == 1 and ct == c and nseq % DELTA_CHUNKS_PER_ITER == 0) else 1
    kern = functools.partial(_delta_kernel, sb=sb, ct=ct, c=c, nh=nh, dk=dk)
    return pl.pallas_call(
        kern,
        grid=(nseq // sb, nt),
        in_specs=[
            pl.BlockSpec((sb * ct, w), lambda s, t: (s * nt + t, 0)),
            pl.BlockSpec((sb * ct, w), lambda s, t: (s * nt + t, 1)),
            pl.BlockSpec((sb * ct, w), lambda s, t: (s * nt + t, 2)),
            pl.BlockSpec((sb * ct, LANES), lambda s, t: (s * nt + t, 0)),
            pl.BlockSpec((sb, nh, dk, dk), lambda s, t: (s, 0, 0, 0)),
        ],
        out_specs=[pl.BlockSpec((sb * ct, w), lambda s, t: (s * nt + t, 0)),
                   pl.BlockSpec((sb, nh, dk, dk), lambda s, t: (s, 0, 0, 0))],
        out_shape=[jax.ShapeDtypeStruct((rows, w), F32),
                   jax.ShapeDtypeStruct((nseq, nh, dk, dk), F32)],
        scratch_shapes=[pltpu.VMEM((sb, nh, dk, dk), F32)],
        compiler_params=_cparams(("parallel", "arbitrary")),
        name="delta_rule",
    )(qkv, qkv, qkv, gb, s0)


def _mixer_ffn_kernel(*refs, gated, nh, dv, tf, alpha):
    if gated:
        o_ref, z_ref, nw_ref, *refs = refs
    else:
        o_ref, *refs = refs
    (x_ref, wo_ref, g1_ref, b1_ref, wg_ref, wu_ref, wd_ref, g2_ref, b2_ref,
     out_ref, a_scr, h_scr, acc_scr) = refs
    if gated:
        for h in range(nh):
            cs = slice(h * dv, (h + 1) * dv)
            oh = o_ref[:, cs]
            zh = z_ref[:, cs]
            oh = oh * lax.rsqrt(jnp.mean(oh * oh, axis=-1, keepdims=True) + RMS_EPS) * nw_ref[...]
            a_scr[:, cs] = (oh * (zh * _sigmoid(zh))).astype(BF16)
    else:
        a_scr[...] = o_ref[...].astype(BF16)
    h_scr[...] = _layer_norm(alpha * x_ref[...] + _dot(a_scr[...], wo_ref[...]),
                             g1_ref[...], b1_ref[...])
    hb = h_scr[...].astype(BF16)
    for c in range(wg_ref.shape[1] // tf):
        fs = slice(c * tf, (c + 1) * tf)
        gate = _dot(hb, wg_ref[:, fs])
        up = _dot(hb, wu_ref[:, fs])
        act = (gate * _sigmoid(gate) * up).astype(BF16)
        down = _dot(act, wd_ref[fs, :])
        if c == 0:
            acc_scr[...] = down
        else:
            acc_scr[...] += down
    out_ref[...] = _layer_norm(alpha * h_scr[...] + acc_scr[...], g2_ref[...], b2_ref[...])


def _mixer_ffn(o, x, wo, ln1, wg, wu, wd, ln2, *, tm, tf, alpha, z_src=None, z_blk=0,
               norm_w=None, nh=1, dv=1):
    rows, d_in = o.shape
    d = x.shape[1]
    f = wg.shape[1]
    gated = z_src is not None
    kern = functools.partial(_mixer_ffn_kernel, gated=gated, nh=nh, dv=dv, tf=tf, alpha=alpha)
    row_spec = lambda width: pl.BlockSpec((tm, width), lambda i: (i, 0))
    const = lambda shape: pl.BlockSpec(shape, lambda i: (0, 0))
    resident = lambda shape: pl.BlockSpec(shape, lambda i: (0, 0), pipeline_mode=pl.Buffered(1))
    in_specs = [row_spec(d_in)]
    args = [o]
    if gated:
        in_specs += [pl.BlockSpec((tm, d_in), lambda i: (i, z_blk)), const((1, dv))]
        args += [z_src, norm_w]
    in_specs += [row_spec(d), resident((d_in, d)), const((1, d)), const((1, d)),
                 resident((d, f)), resident((d, f)), resident((f, d)), const((1, d)), const((1, d))]
    args += [x, wo, ln1[0], ln1[1], wg, wu, wd, ln2[0], ln2[1]]
    return pl.pallas_call(
        kern,
        grid=(rows // tm,),
        in_specs=in_specs,
        out_specs=row_spec(d),
        out_shape=jax.ShapeDtypeStruct((rows, d), F32),
        scratch_shapes=[pltpu.VMEM((tm, d_in), BF16), pltpu.VMEM((tm, d), F32),
                        pltpu.VMEM((tm, d), F32)],
        compiler_params=_cparams(("parallel",)),
        name="mixer_ffn_ln",
    )(*args)


def _qkv_kernel(x_ref, w_ref, cos_ref, sin_ref, q_ref, k_ref, v_ref, *t_refs, nq, nk, half):
    y = _dot(x_ref[...].astype(BF16), w_ref[...])
    cos = cos_ref[...]
    sin = sin_ref[...]
    lane = lax.broadcasted_iota(jnp.int32, cos.shape, 1)
    first = jnp.bitwise_and(lane, 2 * half - 1) < half

    def rope(xg):
        partner = jnp.where(first, pltpu.roll(xg, LANES - half, 1), pltpu.roll(xg, half, 1))
        return xg * cos + partner * sin

    for c in range(nq // LANES):
        q_ref[:, c * LANES:(c + 1) * LANES] = rope(y[:, c * LANES:(c + 1) * LANES])
    k = jnp.concatenate([rope(y[:, nq + c * LANES:nq + (c + 1) * LANES])
                         for c in range(nk // LANES)], axis=1)
    k_ref[...] = k
    v_ref[...] = y[:, nq + nk:]
    if t_refs:
        kt_ref, vt_ref = t_refs
        kt_ref[...] = k.T
        vt_ref[...] = y[:, nq + nk:].T


def _qkv_rope(x, w, cos, sin, *, tm, nq, nk, half, t_seq=None):
    rows, d = x.shape
    n_tab = cos.shape[0] // tm
    kern = functools.partial(_qkv_kernel, nq=nq, nk=nk, half=half)
    out_specs = [pl.BlockSpec((tm, nq), lambda i: (i, 0)),
                 pl.BlockSpec((tm, nk), lambda i: (i, 0)),
                 pl.BlockSpec((tm, nk), lambda i: (i, 0))]
    out_shape = [jax.ShapeDtypeStruct((rows, nq), F32),
                 jax.ShapeDtypeStruct((rows, nk), F32),
                 jax.ShapeDtypeStruct((rows, nk), F32)]
    if t_seq is not None:
        nt = t_seq // tm
        out_specs += [pl.BlockSpec((None, nk, tm), lambda i: (i // nt, 0, i % nt))] * 2
        out_shape += [jax.ShapeDtypeStruct((rows // t_seq, nk, t_seq), F32)] * 2
    return pl.pallas_call(
        kern,
        grid=(rows // tm,),
        in_specs=[pl.BlockSpec((tm, d), lambda i: (i, 0)),
                  pl.BlockSpec((d, nq + 2 * nk), lambda i: (0, 0)),
                  pl.BlockSpec((tm, LANES), lambda i: (i % n_tab, 0)),
                  pl.BlockSpec((tm, LANES), lambda i: (i % n_tab, 0))],
        out_specs=out_specs,
        out_shape=out_shape,
        compiler_params=_cparams(("parallel",)),
        name="qkv_rope",
    )(x, w, cos, sin)


def _kmeans_kernel(k_ref, m_ref, *, nblk):
    kb = k_ref[...].reshape(nblk, MOBA_BLOCK, k_ref.shape[1])
    m_ref[...] = jnp.sum(kb, axis=1) * (1.0 / MOBA_BLOCK)


def _block_means(k, *, nblk_step):
    rows, w = k.shape
    total = rows // MOBA_BLOCK
    return pl.pallas_call(
        functools.partial(_kmeans_kernel, nblk=nblk_step),
        grid=(total // nblk_step,),
        in_specs=[pl.BlockSpec((nblk_step * MOBA_BLOCK, w), lambda i: (i, 0))],
        out_specs=pl.BlockSpec((nblk_step, w), lambda i: (i, 0)),
        out_shape=jax.ShapeDtypeStruct((total, w), F32),
        compiler_params=_cparams(("parallel",)),
        name="block_means",
    )(k)


def _top_select(gate, n_past, axis):
    n = gate.shape[axis]
    idx = lax.broadcasted_iota(jnp.int32, gate.shape, axis).astype(F32)
    gm = jnp.where(idx < n_past, gate, NEG_INF)
    sel = jnp.zeros(gate.shape, F32)
    for r in range(MOBA_TOPK):
        mx = jnp.max(gm, axis=axis, keepdims=True)
        first = jnp.min(jnp.where(gm == mx, idx, float(n)), axis=axis, keepdims=True)
        pick = idx == first
        keep = jnp.where(r < n_past, 1.0, 0.0).astype(F32)
        sel = jnp.maximum(sel, jnp.where(pick, keep, 0.0))
        gm = jnp.where(pick, NEG_INF, gm)
    return sel


def _moba_prompt_kernel(qi_ref, ka_ref, kb_ref, last_ref, q_ref, ka, kb, vta, vtb, mean_ref, o_ref,
                        qs_scr, sel_scr, m_scr, l_scr, acc_scr, *, tq, ngrp, grp, hd, scale):
    p = pl.program_id(1)
    qi = qi_ref[p]
    kja = ka_ref[p]
    kjb = kb_ref[p]
    rows = grp * tq

    @pl.when(kja == qi)
    def _():
        qt = q_ref[...].T
        for g in range(ngrp):
            qg = jnp.concatenate(
                [qt[(g * grp + hh) * hd:(g * grp + hh + 1) * hd, :] for hh in range(grp)], axis=1)
            qs_scr[g] = (qg * (scale * LOG2E)).astype(BF16)
            gate = _dot(mean_ref[:, g * hd:(g + 1) * hd], qg, HI)
            sel_scr[g] = _top_select(gate, qi, 0)
        m_scr[...] = jnp.full(m_scr.shape, NEG_INF, F32)
        l_scr[...] = jnp.zeros(l_scr.shape, F32)
        acc_scr[...] = jnp.zeros(acc_scr.shape, F32)

    ones_rows = jnp.ones((2 * SUBLANES, MOBA_BLOCK), BF16)

    def attend(k_ref, vt_ref, kj, diagonal):
        kgs = [k_ref[:, g * hd:(g + 1) * hd].astype(BF16) for g in range(ngrp)]
        vgs = [jnp.concatenate([vt_ref[g * hd:(g + 1) * hd, :].astype(BF16), ones_rows], axis=0)
               for g in range(ngrp)]
        sel_rows = None if diagonal else [sel_scr[g, pl.ds(kj, 1), :] for g in range(ngrp)]
        units = [(g, slice(c * UNIT_LANES, (c + 1) * UNIT_LANES))
                 for g in range(ngrp) for c in range(rows // UNIT_LANES)]
        def scores(g, ls):
            s = _dot(kgs[g], qs_scr[g, :, ls])
            if diagonal:
                kpos = lax.broadcasted_iota(jnp.int32, s.shape, 0)
                lane = lax.broadcasted_iota(jnp.int32, s.shape, 1)
                s = jnp.where(kpos <= jnp.bitwise_and(lane + ls.start, tq - 1), s, NEG_INF)
            return s

        def probs(g, ls, s):
            m_prev = m_scr[g, :, ls]
            m_cand = jnp.maximum(m_prev, jnp.max(s, axis=0, keepdims=True))
            if diagonal:
                m_new = m_ref = m_cand
            else:
                picked = sel_rows[g][:, ls] > 0.5
                m_new = jnp.where(picked, m_cand, m_prev)
                m_ref = jnp.where(picked, m_cand, POS_INF)
            return jnp.exp2(m_prev - m_new), m_new, jnp.exp2(s - m_ref).astype(BF16)

        def finish(g, ls, alpha, m_new, p):
            pv = _dot(vgs[g], p)
            acc_scr[g, :, ls] = alpha * acc_scr[g, :, ls] + pv[:hd, :]
            l_scr[g, :, ls] = alpha * l_scr[g, :, ls] + pv[hd:hd + 1, :]
            m_scr[g, :, ls] = m_new

        n = len(units)
        s_q, p_q = {}, {}
        for step in range(n + SKEW_S + SKEW_P):
            if step < n:
                s_q[step] = scores(*units[step])
            j = step - SKEW_S
            if 0 <= j < n:
                p_q[j] = probs(*units[j], s_q.pop(j))
            k = j - SKEW_P
            if 0 <= k < n:
                finish(*units[k], *p_q.pop(k))

    @pl.when(kja == qi)
    def _():
        attend(ka, vta, kja, True)

    @pl.when(kja < qi)
    def _():
        attend(ka, vta, kja, False)

    @pl.when(kjb < qi)
    def _():
        attend(kb, vtb, kjb, False)

    @pl.when(last_ref[p] == 1)
    def _():
        outs = []
        for g in range(ngrp):
            og = acc_scr[g] / l_scr[g]
            outs += [og[:, hh * tq:(hh + 1) * tq] for hh in range(grp)]
        o_ref[...] = jnp.concatenate(outs, axis=0).T


def _moba_prompt(q, k, vt, means, *, nb_batch, t_len, ngrp, grp, hd):
    tq = MOBA_BLOCK
    nq = t_len // tq
    nb = t_len // MOBA_BLOCK
    steps = []
    for i in range(nq):
        steps.append((i, i, i, int(i == 0)))
        for j in range(0, i, 2):
            steps.append((i, j, min(j + 1, i), int(j + 2 >= i)))
    tabs = [jnp.asarray(np.array([st[c] for st in steps], np.int32)) for c in range(4)]
    rows = grp * tq
    w = ngrp * grp * hd
    kvw = ngrp * hd
    kern = functools.partial(_moba_prompt_kernel, tq=tq, ngrp=ngrp, grp=grp, hd=hd,
                             scale=float(hd) ** -0.5)
    grid_spec = pltpu.PrefetchScalarGridSpec(
        num_scalar_prefetch=4,
        grid=(nb_batch, len(steps)),
        in_specs=[
            pl.BlockSpec((tq, w), lambda b, p, qt, ka, kb, la: (b * nq + qt[p], 0)),
            pl.BlockSpec((MOBA_BLOCK, kvw), lambda b, p, qt, ka, kb, la: (b * nb + ka[p], 0)),
            pl.BlockSpec((MOBA_BLOCK, kvw), lambda b, p, qt, ka, kb, la: (b * nb + kb[p], 0)),
            pl.BlockSpec((None, kvw, MOBA_BLOCK), lambda b, p, qt, ka, kb, la: (b, 0, ka[p])),
            pl.BlockSpec((None, kvw, MOBA_BLOCK), lambda b, p, qt, ka, kb, la: (b, 0, kb[p])),
            pl.BlockSpec((nb, kvw), lambda b, p, qt, ka, kb, la: (b, 0)),
        ],
        out_specs=pl.BlockSpec((tq, w), lambda b, p, qt, ka, kb, la: (b * nq + qt[p], 0)),
        scratch_shapes=[pltpu.VMEM((ngrp, hd, rows), BF16),
                        pltpu.VMEM((ngrp, nb, rows), F32),
                        pltpu.VMEM((ngrp, 1, rows), F32),
                        pltpu.VMEM((ngrp, 1, rows), F32),
                        pltpu.VMEM((ngrp, hd, rows), F32)],
    )
    return pl.pallas_call(
        kern,
        grid_spec=grid_spec,
        out_shape=jax.ShapeDtypeStruct(q.shape, F32),
        compiler_params=_cparams(("parallel", "arbitrary")),
        name="moba_prompt",
    )(*tabs, q, k, k, vt, vt, means)


def _sample_partial_kernel(pt_ref, q_ref, *refs, npg, ngrp, grp, hd, scale):
    k_pages = refs[:npg]
    v_pages = refs[npg:2 * npg]
    mean_ref, m_ref, l_ref, po_ref, qs_scr = refs[2 * npg:]
    step = pl.program_id(1)
    t_new = q_ref.shape[0]
    gr = grp * t_new

    @pl.when(step == 0)
    def _():
        for g in range(ngrp):
            qg = jnp.concatenate(
                [q_ref[:, (g * grp + hh) * hd:(g * grp + hh + 1) * hd] for hh in range(grp)], axis=0)
            qs_scr[g] = (qg * scale).astype(BF16)
        mean_ref[...] = jnp.zeros(mean_ref.shape, F32)
        m_ref[...] = jnp.zeros(m_ref.shape, F32)
        l_ref[...] = jnp.zeros(l_ref.shape, F32)

    ppb = MOBA_BLOCK // k_pages[0].shape[-1]
    bps = npg // ppb
    lane = lax.broadcasted_iota(jnp.int32, m_ref.shape, 1)
    lane_hd = lax.broadcasted_iota(jnp.int32, (hd, LANES), 1)
    blocks = [slice(blk * ppb, (blk + 1) * ppb) for blk in range(bps)]
    kts = [[jnp.concatenate([r[g] for r in k_pages[sl]], axis=1) for g in range(ngrp)]
           for sl in blocks]
    scs = [jnp.concatenate([_dot(qs_scr[g], kt[g].astype(BF16)) for g in range(ngrp)], axis=0)
           for kt in kts]
    ms = [jnp.max(sc, axis=1, keepdims=True) for sc in scs]
    prs = [jnp.exp(sc - m) for sc, m in zip(scs, ms)]
    ls = [jnp.sum(pr, axis=1, keepdims=True) for pr in prs]
    outs = []
    for sl, pr in zip(blocks, prs):
        prb = pr.astype(BF16)
        outs.append(jnp.concatenate(
            [_dot_nt(prb[g * gr:(g + 1) * gr, :],
                     jnp.concatenate([r[g] for r in v_pages[sl]], axis=1).astype(BF16))
             for g in range(ngrp)], axis=0))
    m_tile = m_ref[...]
    l_tile = l_ref[...]
    for blk in range(bps):
        bidx = step * bps + blk
        m_tile = jnp.where(lane == bidx, ms[blk], m_tile)
        l_tile = jnp.where(lane == bidx, ls[blk], l_tile)
        for g in range(ngrp):
            mean_col = jnp.sum(kts[blk][g], axis=1, keepdims=True) * (1.0 / MOBA_BLOCK)
            rs = slice(g * hd, (g + 1) * hd)
            mean_ref[rs, :] = jnp.where(lane_hd == bidx, mean_col, mean_ref[rs, :])
    m_ref[...] = m_tile
    l_ref[...] = l_tile
    ppl = LANES // hd
    for j in range(bps // ppl):
        po_ref[j] = jnp.concatenate(outs[j * ppl:(j + 1) * ppl], axis=1)


def _sample_partials(q, cache_k, cache_v, page_table, *, npg, ngrp, grp, hd):
    db, n_pages = page_table.shape
    page = cache_k.shape[-1]
    kvw = ngrp * hd
    t_new = q.shape[0] // db
    ppb = MOBA_BLOCK // page
    bps = npg // ppb
    ppl = LANES // hd
    nsteps = n_pages // npg
    nblk = n_pages // ppb
    nrow = ngrp * grp * t_new
    assert nblk <= LANES and bps % ppl == 0
    kern = functools.partial(_sample_partial_kernel, npg=npg, ngrp=ngrp, grp=grp, hd=hd,
                             scale=float(hd) ** -0.5)

    def page_spec(i):
        return pl.BlockSpec((None, ngrp, hd, page),
                            lambda b, s, pt: (pt[b * n_pages + s * npg + i], 0, 0, 0))

    grid_spec = pltpu.PrefetchScalarGridSpec(
        num_scalar_prefetch=1,
        grid=(db, nsteps),
        in_specs=[pl.BlockSpec((t_new, q.shape[1]), lambda b, s, pt: (b, 0))]
        + [page_spec(i) for i in range(npg)] + [page_spec(i) for i in range(npg)],
        out_specs=[pl.BlockSpec((None, kvw, LANES), lambda b, s, pt: (b, 0, 0)),
                   pl.BlockSpec((None, nrow, LANES), lambda b, s, pt: (b, 0, 0)),
                   pl.BlockSpec((None, nrow, LANES), lambda b, s, pt: (b, 0, 0)),
                   pl.BlockSpec((None, bps // ppl, nrow, LANES), lambda b, s, pt: (b, s, 0, 0))],
        scratch_shapes=[pltpu.VMEM((ngrp, grp * t_new, hd), BF16)],
    )
    return pl.pallas_call(
        kern,
        grid_spec=grid_spec,
        out_shape=[jax.ShapeDtypeStruct((db, kvw, LANES), F32),
                   jax.ShapeDtypeStruct((db, nrow, LANES), F32),
                   jax.ShapeDtypeStruct((db, nrow, LANES), F32),
                   jax.ShapeDtypeStruct((db, nblk // ppl, nrow, LANES), F32)],
        compiler_params=_cparams(("parallel", "arbitrary")),
        name="moba_sample_partials",
    )(page_table.reshape(-1), q, *([cache_k] * npg), *([cache_v] * npg))


def _sample_combine_kernel(q_ref, kn_ref, vn_ref, mean_ref, m_ref, l_ref, po_ref, o_ref,
                           *, nblk, ngrp, grp, hd, scale):
    t_new = q_ref.shape[0]
    gr = grp * t_new
    nrow = ngrp * gr
    gates, qss = [], []
    for g in range(ngrp):
        qg = jnp.concatenate(
            [q_ref[:, (g * grp + hh) * hd:(g * grp + hh + 1) * hd] for hh in range(grp)], axis=0)
        qss.append((qg * scale).astype(BF16))
        gates.append(_dot(qg, mean_ref[g * hd:(g + 1) * hd, :], HI))
    sel = _top_select(jnp.concatenate(gates, axis=0), nblk, 1) > 0.5
    m_all = m_ref[...]
    m_past = jnp.max(jnp.where(sel, m_all, NEG_INF), axis=1, keepdims=True)
    kpos = lax.broadcasted_iota(jnp.int32, (nrow, t_new), 1)
    qpos = jnp.bitwise_and(lax.broadcasted_iota(jnp.int32, (nrow, t_new), 0), t_new - 1)
    kn = kn_ref[...].astype(BF16)
    vn = vn_ref[...].astype(BF16)
    s_own = jnp.concatenate(
        [_dot_nt(qss[g], kn[:, g * hd:(g + 1) * hd]) for g in range(ngrp)], axis=0)
    s_own = jnp.where(kpos <= qpos, s_own, NEG_INF)
    m_fin = jnp.maximum(m_past, jnp.max(s_own, axis=1, keepdims=True))
    w = jnp.where(sel, jnp.exp(m_all - m_fin), 0.0)
    p_own = jnp.exp(s_own - m_fin)
    l_fin = (jnp.sum(w * l_ref[...], axis=1, keepdims=True)
             + jnp.sum(p_own, axis=1, keepdims=True))
    pb = p_own.astype(BF16)
    o_own = jnp.concatenate(
        [_dot(pb[g * gr:(g + 1) * gr, :], vn[:, g * hd:(g + 1) * hd]) for g in range(ngrp)], axis=0)
    ppl = LANES // hd
    lane = lax.broadcasted_iota(jnp.int32, (nrow, LANES), 1)
    acc = jnp.zeros((nrow, LANES), F32)
    for j in range(nblk // ppl):
        wj = w[:, j * ppl:j * ppl + 1]
        for i in range(1, ppl):
            wj = jnp.where(lane < i * hd, wj, w[:, j * ppl + i:j * ppl + i + 1])
        acc = acc + wj * po_ref[j]
    o = o_own
    for i in range(ppl):
        o = o + acc[:, i * hd:(i + 1) * hd]
    o = o / l_fin
    o_ref[...] = jnp.concatenate(
        [o[r * t_new:(r + 1) * t_new, :] for r in range(ngrp * grp)], axis=1)


def _sample_combine(q, k_new, v_new, means_t, m_part, l_part, po, *, db, nblk, ngrp, grp, hd):
    t_new = q.shape[0] // db
    nrow = ngrp * grp * t_new
    kvw = ngrp * hd
    kern = functools.partial(_sample_combine_kernel, nblk=nblk, ngrp=ngrp, grp=grp, hd=hd,
                             scale=float(hd) ** -0.5)
    return pl.pallas_call(
        kern,
        grid=(db,),
        in_specs=[pl.BlockSpec((t_new, q.shape[1]), lambda b: (b, 0)),
                  pl.BlockSpec((t_new, kvw), lambda b: (b, 0)),
                  pl.BlockSpec((t_new, kvw), lambda b: (b, 0)),
                  pl.BlockSpec((None, kvw, LANES), lambda b: (b, 0, 0)),
                  pl.BlockSpec((None, nrow, LANES), lambda b: (b, 0, 0)),
                  pl.BlockSpec((None, nrow, LANES), lambda b: (b, 0, 0)),
                  pl.BlockSpec((None,) + po.shape[1:], lambda b: (b, 0, 0, 0))],
        out_specs=pl.BlockSpec((t_new, q.shape[1]), lambda b: (b, 0)),
        out_shape=jax.ShapeDtypeStruct(q.shape, F32),
        compiler_params=_cparams(("parallel",)),
        name="moba_sample_combine",
    )(q, k_new, v_new, means_t, m_part, l_part, po)


def _rope_tables(pos, hd, reps):
    half = hd // 2
    inv_freq = ROPE_THETA ** (-jnp.arange(half, dtype=F32) / half)
    ang = pos.astype(F32)[:, None] * inv_freq[None, :]
    cos = jnp.cos(ang)
    sin = jnp.sin(ang)
    per_vreg = LANES // hd
    cos_t = jnp.tile(jnp.concatenate([cos, cos], axis=1), (reps, per_vreg))
    sin_t = jnp.tile(jnp.concatenate([-sin, sin], axis=1), (reps, per_vreg))
    return cos_t, sin_t


def _trunk(x, pos, conv_state8, delta_state, attend, wts, *, nseq, t_len, tm, prep_tt,
           delta_ct, delta_c, dims, want_t=False):
    nh, dk, conv_dim, n_a, depth, alpha, bh, kvh, hd = dims
    rows = nseq * t_len
    new_conv, new_delta = [], []
    k_sh = v_sh = t_sh = None
    for layer in range(depth):
        if layer < n_a:
            cw = wts["a_conv"][layer].shape[0]
            prep_args = (conv_state8[layer], wts["a_conv"][layer], wts["a_log"][layer],
                         wts["a_dtb"][layer])
            prep_kw = dict(nseq=nseq, t_len=t_len, tt=prep_tt, conv_dim=conv_dim, nh=nh, dk=dk)
            if t_len >= tm:
                qkv, gb, z_src, tail = _gdn_in_fused(x, wts["a_w_in"][layer], *prep_args, **prep_kw)
                z_blk = 0
                new_conv.append(tail[:, SUBLANES - (cw - 1):, :])
            else:
                proj = _matmul(x, wts["a_w_in"][layer], rows, wts["a_w_in"][layer].shape[1] // 3,
                               "in_proj")
                qkv, gb = _gdn_prep(proj, *prep_args, **prep_kw)
                z_src, z_blk = proj, conv_dim // (nh * dk)
                new_conv.append(proj.reshape(nseq, t_len, -1)[:, t_len - (cw - 1):, :conv_dim])
            o, s_new = _delta_rule(qkv, gb, delta_state[layer], nseq=nseq, t_len=t_len,
                                   ct=delta_ct, c=delta_c, nh=nh, dk=dk)
            new_delta.append(s_new)
            mix, w_mix = o, wts["a_w_out"][layer]
            mix_kw = dict(z_src=z_src, z_blk=z_blk, norm_w=wts["a_norm"][layer], nh=nh, dv=dk)
        else:
            j = layer - n_a
            w_all = wts["b_w_qkv"][j]
            cos_t, sin_t = _rope_tables(pos, hd, (tm // t_len) if tm > t_len else 1)
            q, k_l, v_l, *t_l = _qkv_rope(x, w_all, cos_t, sin_t, tm=tm, nq=bh * hd, nk=kvh * hd,
                                          half=hd // 2, t_seq=t_len if want_t else None)
            if layer == n_a:
                k_sh, v_sh, t_sh = k_l, v_l, t_l
            mix, w_mix, mix_kw = attend(q, k_sh, v_sh, *t_sh), wts["b_w_o"][j], {}
        ln1 = (wts["ln_g"][layer][0], wts["ln_b"][layer][0])
        ln2 = (wts["ln_g"][layer][1], wts["ln_b"][layer][1])
        x = _mixer_ffn(mix, x, w_mix, ln1, wts["ffn_wg"][layer], wts["ffn_wu"][layer],
                       wts["ffn_wd"][layer], ln2, tm=tm, tf=256, alpha=alpha, **mix_kw)
    return x, jnp.stack(new_conv), jnp.stack(new_delta), k_sh, v_sh, t_sh


def kernel(x_prompt, x_sample, state_conv, state_delta, cache_k, cache_v, page_table, ln_g, ln_b,
           a_w_in, a_conv, a_log_decay, a_dt_bias, a_norm, a_w_out, kv_w_k, kv_w_v, b_w_q, b_w_o,
           ffn_w_gate, ffn_w_up, ffn_w_down):
    bp, tp, d = x_prompt.shape
    db, ts, _ = x_sample.shape
    depth = ln_g.shape[0]
    n_a = a_w_in.shape[0]
    nh = a_log_decay.shape[1]
    dk = state_delta.shape[-2]
    conv_dim = a_conv.shape[-1]
    cw = a_conv.shape[1]
    n_pool, page, kvh, hd = cache_k.shape
    bh = b_w_q.shape[-1] // hd
    grp = bh // kvh
    n_pages = page_table.shape[1]
    past = n_pages * page
    alpha = (2.0 * depth) ** 0.25
    assert past % MOBA_BLOCK == 0 and ts <= MOBA_BLOCK and ts >= cw - 1 and tp >= cw - 1
    assert state_delta.shape[-1] == dk and conv_dim == 3 * nh * dk

    in_w = a_w_in.shape[-1]
    in_pad = -(-in_w // (3 * LANES)) * (3 * LANES)
    pad_lanes = lambda v: jnp.pad(v, ((0, 0), (0, LANES - v.shape[-1])))[:, None, :]
    wts = {
        "a_w_in": jnp.pad(a_w_in, ((0, 0), (0, 0), (0, in_pad - in_w))).astype(BF16),
        "a_conv": a_conv,
        "a_log": pad_lanes(a_log_decay),
        "a_dtb": pad_lanes(a_dt_bias),
        "a_norm": a_norm[:, None, :],
        "a_w_out": a_w_out.astype(BF16),
        "b_w_qkv": jnp.concatenate(
            [b_w_q, jnp.broadcast_to(kv_w_k, (b_w_q.shape[0],) + kv_w_k.shape),
             jnp.broadcast_to(kv_w_v, (b_w_q.shape[0],) + kv_w_v.shape)], axis=-1).astype(BF16),
        "b_w_o": b_w_o.astype(BF16),
        "ffn_wg": ffn_w_gate.astype(BF16),
        "ffn_wu": ffn_w_up.astype(BF16),
        "ffn_wd": ffn_w_down.astype(BF16),
        "ln_g": ln_g[:, :, None, :],
        "ln_b": ln_b[:, :, None, :],
    }
    dims = (nh, dk, conv_dim, n_a, depth, alpha, bh, kvh, hd)

    nb = tp // MOBA_BLOCK

    def attend_prompt(q, k, v, kt, vt):
        total = bp * nb
        step = SUBLANES if total % SUBLANES == 0 else total
        means = _block_means(k, nblk_step=step)
        return _moba_prompt(q, k, vt, means, nb_batch=bp, t_len=tp, ngrp=kvh, grp=grp, hd=hd)

    tm_p = 512 if (bp * tp) % 512 == 0 else 256
    y_p, p_conv, p_delta, _, _, (p_kt, p_vt) = _trunk(
        x_prompt.reshape(bp * tp, d), jnp.arange(tp, dtype=jnp.int32),
        jnp.zeros((n_a, bp, SUBLANES, conv_dim), F32), jnp.zeros((n_a, bp, nh, dk, dk), F32),
        attend_prompt, wts, nseq=bp, t_len=tp, tm=tm_p, prep_tt=tm_p, delta_ct=256,
        delta_c=DELTA_CHUNK, dims=dims, want_t=True)
    p_k = jnp.transpose(p_kt.reshape(bp, kvh, hd, tp), (0, 3, 1, 2))
    p_v = jnp.transpose(p_vt.reshape(bp, kvh, hd, tp), (0, 3, 1, 2))

    ck = jnp.transpose(cache_k, (0, 2, 3, 1))
    cv = jnp.transpose(cache_v, (0, 2, 3, 1))
    npg = next(n for n in (16, 8, 2 * (MOBA_BLOCK // page)) if n_pages % n == 0)

    def attend_sample(q, k_new, v_new):
        means_t, m_part, l_part, po = _sample_partials(q, ck, cv, page_table, npg=npg, ngrp=kvh,
                                                       grp=grp, hd=hd)
        return _sample_combine(q, k_new, v_new, means_t, m_part, l_part, po, db=db,
                               nblk=past // MOBA_BLOCK, ngrp=kvh, grp=grp, hd=hd)

    conv8 = jnp.pad(state_conv, ((0, 0), (0, 0), (SUBLANES - (cw - 1), 0), (0, 0)))
    y_s, s_conv, s_delta, s_k, s_v, _ = _trunk(
        x_sample.reshape(db * ts, d), past + jnp.arange(ts, dtype=jnp.int32), conv8, state_delta,
        attend_sample, wts, nseq=db, t_len=ts, tm=db * ts, prep_tt=ts, delta_ct=ts, delta_c=ts,
        dims=dims)

    return (y_p.reshape(bp, tp, d), y_s.reshape(db, ts, d), p_conv, p_delta, p_k, p_v,
            s_conv, s_delta, s_k.reshape(db, ts, kvh, hd), s_v.reshape(db, ts, kvh, hd))
```

```python
import functools

import numpy as np
import jax
import jax.numpy as jnp
from jax import lax
from jax.experimental import pallas as pl
from jax.experimental.pallas import tpu as pltpu

F32 = jnp.float32
BF16 = jnp.bfloat16
HI = lax.Precision.HIGHEST

LANES = 128
SUBLANES = 8
VMEM_LIMIT_MB = 56

MOBA_BLOCK = 256
MOBA_TOPK = 3
MOBA_PAST_PER_STEP = 2
UNIT_LANES = 256
SKEW_S = 3
SKEW_P = 2
DELTA_CHUNK = 64
DELTA_SOLVE_GROUP = 4
DELTA_CHUNKS_PER_ITER = 8
ROPE_THETA = 10000.0
LN_EPS = 1e-5
RMS_EPS = 1e-6
L2_EPS = 1e-6
NEG_INF = float("-inf")
POS_INF = float("inf")
LOG2E = 1.4426950408889634


def _cparams(sem):
    return pltpu.CompilerParams(dimension_semantics=sem,
                                vmem_limit_bytes=VMEM_LIMIT_MB * 1024 * 1024)


def _dot(a, b, prec=None):
    return jnp.dot(a, b, precision=prec, preferred_element_type=F32)


def _dot_nt(a, b, prec=None):
    return lax.dot_general(a, b, (((1,), (1,)), ((), ())), precision=prec,
                           preferred_element_type=F32)


def _dot_tn(a, b, prec=None):
    return lax.dot_general(a, b, (((0,), (0,)), ((), ())), precision=prec,
                           preferred_element_type=F32)


def _split(a):
    hi = a.astype(BF16)
    return hi, (a - hi.astype(F32)).astype(BF16)


def _dot3(a, b):
    return _dot(a[0], b[0]) + (_dot(a[0], b[1]) + _dot(a[1], b[0]))


def _sigmoid(x):
    return 1.0 / (1.0 + jnp.exp(-x))


def _layer_norm(v, g, b):
    mu = jnp.mean(v, axis=-1, keepdims=True)
    d = v - mu
    var = jnp.mean(d * d, axis=-1, keepdims=True)
    return d * lax.rsqrt(var + LN_EPS) * g + b


def _mm_kernel(x_ref, w_ref, o_ref):
    o_ref[...] = _dot(x_ref[...].astype(BF16), w_ref[...])


def _matmul(x, w, tm, tn, name):
    r, k = x.shape
    n = w.shape[1]
    return pl.pallas_call(
        _mm_kernel,
        grid=(r // tm, n // tn),
        in_specs=[pl.BlockSpec((tm, k), lambda i, j: (i, 0)),
                  pl.BlockSpec((k, tn), lambda i, j: (0, j))],
        out_specs=pl.BlockSpec((tm, tn), lambda i, j: (i, j)),
        out_shape=jax.ShapeDtypeStruct((r, n), F32),
        compiler_params=_cparams(("parallel", "parallel")),
        name=name,
    )(x, w)


def _conv_norm(xp_ref, cw_ref, qkv_ref, cols, *, tt, conv_w, n_q, n_qk, q_scale):
    hist = SUBLANES
    for c in cols:
        cs = slice(c * LANES, (c + 1) * LANES)
        full = xp_ref[0:hist + tt, cs]
        y = pltpu.roll(full, conv_w - 1, 0)[hist:] * cw_ref[0:1, cs]
        for i in range(1, conv_w - 1):
            y = y + pltpu.roll(full, conv_w - 1 - i, 0)[hist:] * cw_ref[i:i + 1, cs]
        y = y + full[hist:] * cw_ref[conv_w - 1:conv_w, cs]
        y = y * _sigmoid(y)
        if c < n_qk:
            y = y * lax.rsqrt(jnp.sum(y * y, axis=-1, keepdims=True) + L2_EPS)
            if c < n_q:
                y = y * q_scale
        qkv_ref[:, cs] = y


def _decay_beta(ab, alog, dtb, nh):
    lane = lax.broadcasted_iota(jnp.int32, ab.shape, 1)
    sp = ab + dtb
    softplus = jnp.maximum(sp, 0.0) + jnp.log1p(jnp.exp(-jnp.abs(sp)))
    return jnp.where(lane < nh, -jnp.exp(alog) * softplus, _sigmoid(ab))


def _prep_kernel(cur_ref, prev_ref, st_ref, ab_ref, cw_ref, alog_ref, dtb_ref,
                 qkv_ref, gb_ref, xp_ref, *, tt, conv_w, n_q, n_qk, nh, q_scale):
    t = pl.program_id(1)
    hist = SUBLANES

    @pl.when(t == 0)
    def _():
        xp_ref[0:hist, :] = st_ref[...]

    @pl.when(t != 0)
    def _():
        xp_ref[0:hist, :] = prev_ref[...]

    xp_ref[hist:hist + tt, :] = cur_ref[...]
    _conv_norm(xp_ref, cw_ref, qkv_ref, range(cur_ref.shape[1] // LANES), tt=tt, conv_w=conv_w,
               n_q=n_q, n_qk=n_qk, q_scale=q_scale)
    gb_ref[...] = _decay_beta(ab_ref[...], alog_ref[...], dtb_ref[...], nh)


def _gdn_in_kernel(x_ref, w_ref, st_ref, cw_ref, alog_ref, dtb_ref, qkv_ref, gb_ref, z_ref, tail_ref,
                   xp_ref, *, tt, conv_dim, zw, conv_w, n_q, n_qk, nh, q_scale):
    hist = SUBLANES

    @pl.when(pl.program_id(1) == 0)
    def _():
        xp_ref[0:hist, :] = st_ref[...]

    xb = x_ref[...].astype(BF16)
    mxu_w = 2 * LANES
    for c in range(conv_dim // mxu_w):
        cs = slice(c * mxu_w, (c + 1) * mxu_w)
        xp_ref[hist:hist + tt, cs] = _dot(xb, w_ref[:, cs])
        _conv_norm(xp_ref, cw_ref, qkv_ref, range(2 * c, 2 * c + 2), tt=tt, conv_w=conv_w,
                   n_q=n_q, n_qk=n_qk, q_scale=q_scale)
    z_ref[...] = _dot(xb, w_ref[:, conv_dim:conv_dim + zw])
    ab = _dot(xb, w_ref[:, conv_dim + zw:conv_dim + zw + LANES])
    gb_ref[...] = _decay_beta(ab, alog_ref[...], dtb_ref[...], nh)
    tail = xp_ref[tt:tt + hist, :]
    tail_ref[...] = tail
    xp_ref[0:hist, :] = tail


def _gdn_in_fused(x, w, state8, conv_w, alog, dtb, *, nseq, t_len, tt, conv_dim, nh, dk):
    rows, d = x.shape
    nt = t_len // tt
    cw = conv_w.shape[0]
    zw = nh * dk
    kern = functools.partial(_gdn_in_kernel, tt=tt, conv_dim=conv_dim, zw=zw, conv_w=cw, n_q=nh,
                             n_qk=2 * nh, nh=nh, q_scale=float(dk) ** -0.5)
    row = lambda width: pl.BlockSpec((tt, width), lambda s, t: (s * nt + t, 0))
    const = lambda shape: pl.BlockSpec(shape, lambda s, t: (0, 0))
    per_seq = pl.BlockSpec((None, SUBLANES, conv_dim), lambda s, t: (s, 0, 0))
    return pl.pallas_call(
        kern,
        grid=(nseq, nt),
        in_specs=[row(d),
                  pl.BlockSpec(w.shape, lambda s, t: (0, 0), pipeline_mode=pl.Buffered(1)),
                  per_seq, const((cw, conv_dim)), const((1, LANES)), const((1, LANES))],
        out_specs=[row(conv_dim), row(LANES), row(zw), per_seq],
        out_shape=[jax.ShapeDtypeStruct((rows, conv_dim), F32),
                   jax.ShapeDtypeStruct((rows, LANES), F32),
                   jax.ShapeDtypeStruct((rows, zw), F32),
                   jax.ShapeDtypeStruct((nseq, SUBLANES, conv_dim), F32)],
        scratch_shapes=[pltpu.VMEM((tt + SUBLANES, conv_dim), F32)],
        compiler_params=_cparams(("parallel", "arbitrary")),
        name="gdn_in_fused",
    )(x, w, state8, conv_w, alog, dtb)


def _gdn_prep(proj, state8, conv_w, alog, dtb, *, nseq, t_len, tt, conv_dim, nh, dk):
    rows = nseq * t_len
    nt = t_len // tt
    cw = conv_w.shape[0]
    ab_blk = (conv_dim + nh * dk) // LANES
    kern = functools.partial(_prep_kernel, tt=tt, conv_w=cw, n_q=nh, n_qk=2 * nh, nh=nh,
                             q_scale=float(dk) ** -0.5)
    tpb = tt // SUBLANES
    return pl.pallas_call(
        kern,
        grid=(nseq, nt),
        in_specs=[
            pl.BlockSpec((tt, conv_dim), lambda s, t: (s * nt + t, 0)),
            pl.BlockSpec((SUBLANES, conv_dim),
                         lambda s, t: (jnp.maximum((s * nt + t) * tpb - 1, 0), 0)),
            pl.BlockSpec((None, SUBLANES, conv_dim), lambda s, t: (s, 0, 0)),
            pl.BlockSpec((tt, LANES), lambda s, t: (s * nt + t, ab_blk)),
            pl.BlockSpec((cw, conv_dim), lambda s, t: (0, 0)),
            pl.BlockSpec((1, LANES), lambda s, t: (0, 0)),
            pl.BlockSpec((1, LANES), lambda s, t: (0, 0)),
        ],
        out_specs=[pl.BlockSpec((tt, conv_dim), lambda s, t: (s * nt + t, 0)),
                   pl.BlockSpec((tt, LANES), lambda s, t: (s * nt + t, 0))],
        out_shape=[jax.ShapeDtypeStruct((rows, conv_dim), F32),
                   jax.ShapeDtypeStruct((rows, LANES), F32)],
        scratch_shapes=[pltpu.VMEM((tt + SUBLANES, conv_dim), F32)],
        compiler_params=_cparams(("parallel", "arbitrary")),
        name="gdn_prep",
    )(proj, proj, state8, proj, conv_w, alog, dtb)


def _delta_kernel(q_ref, k_ref, v_ref, gb_ref, s0_ref, o_ref, s_ref, st_scr, *, sb, ct, c, nh, dk):
    t = pl.program_id(1)

    @pl.when(t == 0)
    def _():
        st_scr[...] = s0_ref[...]

    row = lax.broadcasted_iota(jnp.int32, (c, c), 0)
    col = lax.broadcasted_iota(jnp.int32, (c, c), 1)
    causal = row >= col
    strict = row > col
    tril = causal.astype(F32)
    eye = (row == col).astype(F32)
    n_sq = max(int(np.ceil(np.log2(c))) - 1, 0)

    def prepare(rs, seq):
        gbc = gb_ref[rs, :]
        gc = _dot(tril, gbc, HI)
        gt = gc.T
        eg = jnp.exp(gc)
        units = []
        for h in range(nh):
            cs = slice(h * dk, (h + 1) * dk)
            qh = q_ref[rs, cs]
            kh = k_ref[rs, cs]
            vh = v_ref[rs, cs]
            gcol = gc[:, h:h + 1]
            grow = gt[h:h + 1, :]
            bcol = gbc[:, nh + h:nh + h + 1]
            egcol = eg[:, h:h + 1]
            decay = jnp.where(causal, jnp.exp(jnp.where(causal, gcol - grow, 0.0)), 0.0)
            kb = kh.astype(BF16)
            kk = _dot_nt(kb, kb)
            qk = _dot_nt(qh.astype(BF16), kb)
            glast = grow[:, c - 1:c]
            units.append(dict(
                rs=rs, h=h, seq=seq,
                x=-jnp.where(strict, bcol * kk * decay, 0.0),
                qkm=jnp.where(causal, qk * decay, 0.0).astype(BF16),
                rhs=jnp.concatenate([vh * bcol, kh * (bcol * egcol)], axis=-1),
                qd=(qh * egcol).astype(BF16),
                kd=(kh * jnp.exp(glast - gcol)).astype(BF16),
                glast=jnp.exp(glast)))
        return units

    def solve(units):
        ps = [u["x"] for u in units]
        tms = [eye + p for p in ps]
        for _ in range(n_sq):
            sp = [_split(p) for p in ps]
            ps = [_dot3(s, s) for s in sp]
            sp = [_split(p) for p in ps]
            tms = [tm + _dot3(_split(tm), s) for tm, s in zip(tms, sp)]
        return [_dot3(_split(tm), _split(u["rhs"])) for tm, u in zip(tms, units)]

    def update(u, sol):
        h, seq = u["h"], u["seq"]
        s = st_scr[seq, h]
        s_b = s.astype(BF16)
        v_new = sol[:, :dk] - _dot(sol[:, dk:].astype(BF16), s_b)
        vb = v_new.astype(BF16)
        o_ref[u["rs"], h * dk:(h + 1) * dk] = _dot(u["qd"], s_b) + _dot(u["qkm"], vb)
        st_scr[seq, h] = s * u["glast"] + _dot_tn(u["kd"], vb)

    def chunks(r0, n):
        groups = []
        for seq in range(sb):
            for i0 in range(0, n, DELTA_SOLVE_GROUP):
                units = []
                for i in range(i0, min(i0 + DELTA_SOLVE_GROUP, n)):
                    units += prepare(pl.ds(seq * ct + r0 + i * c, c), seq)
                groups.append(units)
        if sb > 1:
            groups = [sum(groups, [])]
        solved = None
        for units in groups:
            sols = solve(units)
            if solved is not None:
                for u, sol in zip(*solved):
                    update(u, sol)
            solved = (units, sols)
        for u, sol in zip(*solved):
            update(u, sol)

    n_chunks = ct // c
    per_iter = DELTA_CHUNKS_PER_ITER if n_chunks % DELTA_CHUNKS_PER_ITER == 0 else 1
    if n_chunks == per_iter:
        chunks(0, per_iter)
    else:
        def body(ci, carry):
            chunks(pl.multiple_of(ci * (per_iter * c), per_iter * c), per_iter)
            return carry
        lax.fori_loop(0, n_chunks // per_iter, body, 0)

    @pl.when(t == pl.num_programs(1) - 1)
    def _():
        s_ref[...] = st_scr[...]


def _delta_rule(qkv, gb, s0, *, nseq, t_len, ct, c, nh, dk):
    rows = nseq * t_len
    nt = t_len // ct
    w = nh * dk
    sb = DELTA_SOLVE_GROUP if (nt == 1 and ct == c and nseq % DELTA_SOLVE_GROUP == 0) else 1
    kern = functools.partial(_delta_kernel, sb=sb, ct=ct, c=c, nh=nh, dk=dk)
    return pl.pallas_call(
        kern,
        grid=(nseq // sb, nt),
        in_specs=[
            pl.BlockSpec((sb * ct, w), lambda s, t: (s * nt + t, 0)),
            pl.BlockSpec((sb * ct, w), lambda s, t: (s * nt + t, 1)),
            pl.BlockSpec((sb * ct, w), lambda s, t: (s * nt + t, 2)),
            pl.BlockSpec((sb * ct, LANES), lambda s, t: (s * nt + t, 0)),
            pl.BlockSpec((sb, nh, dk, dk), lambda s, t: (s, 0, 0, 0)),
        ],
        out_specs=[pl.BlockSpec((sb * ct, w), lambda s, t: (s * nt + t, 0)),
                   pl.BlockSpec((sb, nh, dk, dk), lambda s, t: (s, 0, 0, 0))],
        out_shape=[jax.ShapeDtypeStruct((rows, w), F32),
                   jax.ShapeDtypeStruct((nseq, nh, dk, dk), F32)],
        scratch_shapes=[pltpu.VMEM((sb, nh, dk, dk), F32)],
        compiler_params=_cparams(("parallel", "arbitrary")),
        name="delta_rule",
    )(qkv, qkv, qkv, gb, s0)


def _mixer_ffn_kernel(*refs, gated, nh, dv, tf, alpha):
    if gated:
        o_ref, z_ref, nw_ref, *refs = refs
    else:
        o_ref, *refs = refs
    (x_ref, wo_ref, g1_ref, b1_ref, wg_ref, wu_ref, wd_ref, g2_ref, b2_ref,
     out_ref, a_scr, h_scr, acc_scr) = refs
    if gated:
        for h in range(nh):
            cs = slice(h * dv, (h + 1) * dv)
            oh = o_ref[:, cs]
            zh = z_ref[:, cs]
            oh = oh * lax.rsqrt(jnp.mean(oh * oh, axis=-1, keepdims=True) + RMS_EPS) * nw_ref[...]
            a_scr[:, cs] = (oh * (zh * _sigmoid(zh))).astype(BF16)
    else:
        a_scr[...] = o_ref[...].astype(BF16)
    h_scr[...] = _layer_norm(alpha * x_ref[...] + _dot(a_scr[...], wo_ref[...]),
                             g1_ref[...], b1_ref[...])
    hb = h_scr[...].astype(BF16)
    for c in range(wg_ref.shape[1] // tf):
        fs = slice(c * tf, (c + 1) * tf)
        gate = _dot(hb, wg_ref[:, fs])
        up = _dot(hb, wu_ref[:, fs])
        act = (gate * _sigmoid(gate) * up).astype(BF16)
        down = _dot(act, wd_ref[fs, :])
        if c == 0:
            acc_scr[...] = down
        else:
            acc_scr[...] += down
    out_ref[...] = _layer_norm(alpha * h_scr[...] + acc_scr[...], g2_ref[...], b2_ref[...])


def _mixer_ffn(o, x, wo, ln1, wg, wu, wd, ln2, *, tm, tf, alpha, z_src=None, z_blk=0,
               norm_w=None, nh=1, dv=1):
    rows, d_in = o.shape
    d = x.shape[1]
    f = wg.shape[1]
    gated = z_src is not None
    kern = functools.partial(_mixer_ffn_kernel, gated=gated, nh=nh, dv=dv, tf=tf, alpha=alpha)
    row_spec = lambda width: pl.BlockSpec((tm, width), lambda i: (i, 0))
    const = lambda shape: pl.BlockSpec(shape, lambda i: (0, 0))
    resident = lambda shape: pl.BlockSpec(shape, lambda i: (0, 0), pipeline_mode=pl.Buffered(1))
    in_specs = [row_spec(d_in)]
    args = [o]
    if gated:
        in_specs += [pl.BlockSpec((tm, d_in), lambda i: (i, z_blk)), const((1, dv))]
        args += [z_src, norm_w]
    in_specs += [row_spec(d), resident((d_in, d)), const((1, d)), const((1, d)),
                 resident((d, f)), resident((d, f)), resident((f, d)), const((1, d)), const((1, d))]
    args += [x, wo, ln1[0], ln1[1], wg, wu, wd, ln2[0], ln2[1]]
    return pl.pallas_call(
        kern,
        grid=(rows // tm,),
        in_specs=in_specs,
        out_specs=row_spec(d),
        out_shape=jax.ShapeDtypeStruct((rows, d), F32),
        scratch_shapes=[pltpu.VMEM((tm, d_in), BF16), pltpu.VMEM((tm, d), F32),
                        pltpu.VMEM((tm, d), F32)],
        compiler_params=_cparams(("parallel",)),
        name="mixer_ffn_ln",
    )(*args)


def _qkv_kernel(x_ref, w_ref, cos_ref, sin_ref, q_ref, k_ref, v_ref, *t_refs, nq, nk, half):
    y = _dot(x_ref[...].astype(BF16), w_ref[...])
    cos = cos_ref[...]
    sin = sin_ref[...]
    lane = lax.broadcasted_iota(jnp.int32, cos.shape, 1)
    first = jnp.bitwise_and(lane, 2 * half - 1) < half

    def rope(xg):
        partner = jnp.where(first, pltpu.roll(xg, LANES - half, 1), pltpu.roll(xg, half, 1))
        return xg * cos + partner * sin

    for c in range(nq // LANES):
        q_ref[:, c * LANES:(c + 1) * LANES] = rope(y[:, c * LANES:(c + 1) * LANES])
    k = jnp.concatenate([rope(y[:, nq + c * LANES:nq + (c + 1) * LANES])
                         for c in range(nk // LANES)], axis=1)
    k_ref[...] = k
    v_ref[...] = y[:, nq + nk:]
    if t_refs:
        kt_ref, vt_ref = t_refs
        kt_ref[...] = k.T
        vt_ref[...] = y[:, nq + nk:].T


def _qkv_rope(x, w, cos, sin, *, tm, nq, nk, half, t_seq=None):
    rows, d = x.shape
    n_tab = cos.shape[0] // tm
    kern = functools.partial(_qkv_kernel, nq=nq, nk=nk, half=half)
    out_specs = [pl.BlockSpec((tm, nq), lambda i: (i, 0)),
                 pl.BlockSpec((tm, nk), lambda i: (i, 0)),
                 pl.BlockSpec((tm, nk), lambda i: (i, 0))]
    out_shape = [jax.ShapeDtypeStruct((rows, nq), F32),
                 jax.ShapeDtypeStruct((rows, nk), F32),
                 jax.ShapeDtypeStruct((rows, nk), F32)]
    if t_seq is not None:
        nt = t_seq // tm
        out_specs += [pl.BlockSpec((None, nk, tm), lambda i: (i // nt, 0, i % nt))] * 2
        out_shape += [jax.ShapeDtypeStruct((rows // t_seq, nk, t_seq), F32)] * 2
    return pl.pallas_call(
        kern,
        grid=(rows // tm,),
        in_specs=[pl.BlockSpec((tm, d), lambda i: (i, 0)),
                  pl.BlockSpec((d, nq + 2 * nk), lambda i: (0, 0)),
                  pl.BlockSpec((tm, LANES), lambda i: (i % n_tab, 0)),
                  pl.BlockSpec((tm, LANES), lambda i: (i % n_tab, 0))],
        out_specs=out_specs,
        out_shape=out_shape,
        compiler_params=_cparams(("parallel",)),
        name="qkv_rope",
    )(x, w, cos, sin)


def _kmeans_kernel(k_ref, m_ref, *, nblk):
    kb = k_ref[...].reshape(nblk, MOBA_BLOCK, k_ref.shape[1])
    m_ref[...] = jnp.sum(kb, axis=1) * (1.0 / MOBA_BLOCK)


def _block_means(k, *, nblk_step):
    rows, w = k.shape
    total = rows // MOBA_BLOCK
    return pl.pallas_call(
        functools.partial(_kmeans_kernel, nblk=nblk_step),
        grid=(total // nblk_step,),
        in_specs=[pl.BlockSpec((nblk_step * MOBA_BLOCK, w), lambda i: (i, 0))],
        out_specs=pl.BlockSpec((nblk_step, w), lambda i: (i, 0)),
        out_shape=jax.ShapeDtypeStruct((total, w), F32),
        compiler_params=_cparams(("parallel",)),
        name="block_means",
    )(k)


def _top_select(gate, n_past, axis):
    n = gate.shape[axis]
    idx = lax.broadcasted_iota(jnp.int32, gate.shape, axis).astype(F32)
    gm = jnp.where(idx < n_past, gate, NEG_INF)
    sel = jnp.zeros(gate.shape, F32)
    for r in range(MOBA_TOPK):
        mx = jnp.max(gm, axis=axis, keepdims=True)
        first = jnp.min(jnp.where(gm == mx, idx, float(n)), axis=axis, keepdims=True)
        pick = idx == first
        keep = jnp.where(r < n_past, 1.0, 0.0).astype(F32)
        sel = jnp.maximum(sel, jnp.where(pick, keep, 0.0))
        gm = jnp.where(pick, NEG_INF, gm)
    return sel


def _moba_prompt_kernel(*refs, tq, ngrp, grp, hd, scale):
    npast = MOBA_PAST_PER_STEP
    qi_ref, kj_refs, last_ref = refs[0], refs[1:1 + npast], refs[1 + npast]
    q_ref = refs[2 + npast]
    k_refs = refs[3 + npast:3 + 2 * npast]
    vt_refs = refs[3 + 2 * npast:3 + 3 * npast]
    mean_ref, o_ref, qs_scr, sel_scr, m_scr, l_scr, acc_scr = refs[3 + 3 * npast:]
    p = pl.program_id(1)
    qi = qi_ref[p]
    kjs = [r[p] for r in kj_refs]
    rows = grp * tq

    @pl.when(kjs[0] == qi)
    def _():
        qt = q_ref[...].T
        for g in range(ngrp):
            qg = jnp.concatenate(
                [qt[(g * grp + hh) * hd:(g * grp + hh + 1) * hd, :] for hh in range(grp)], axis=1)
            qs_scr[g] = (qg * (scale * LOG2E)).astype(BF16)
            gate = _dot(mean_ref[:, g * hd:(g + 1) * hd], qg, HI)
            sel_scr[g] = _top_select(gate, qi, 0)
        m_scr[...] = jnp.full(m_scr.shape, NEG_INF, F32)
        l_scr[...] = jnp.zeros(l_scr.shape, F32)
        acc_scr[...] = jnp.zeros(acc_scr.shape, F32)

    ones_rows = jnp.ones((2 * SUBLANES, MOBA_BLOCK), BF16)

    def attend(k_ref, vt_ref, kj, diagonal):
        kgs = [k_ref[:, g * hd:(g + 1) * hd].astype(BF16) for g in range(ngrp)]
        vgs = [jnp.concatenate([vt_ref[g * hd:(g + 1) * hd, :].astype(BF16), ones_rows], axis=0)
               for g in range(ngrp)]
        sel_rows = None if diagonal else [sel_scr[g, pl.ds(kj, 1), :] for g in range(ngrp)]
        units = [(g, slice(c * UNIT_LANES, (c + 1) * UNIT_LANES))
                 for g in range(ngrp) for c in range(rows // UNIT_LANES)]
        def scores(g, ls):
            s = _dot(kgs[g], qs_scr[g, :, ls])
            if diagonal:
                kpos = lax.broadcasted_iota(jnp.int32, s.shape, 0)
                lane = lax.broadcasted_iota(jnp.int32, s.shape, 1)
                s = jnp.where(kpos <= jnp.bitwise_and(lane + ls.start, tq - 1), s, NEG_INF)
            return s

        def probs(g, ls, s):
            m_prev = m_scr[g, :, ls]
            m_cand = jnp.maximum(m_prev, jnp.max(s, axis=0, keepdims=True))
            if diagonal:
                m_new = m_ref = m_cand
            else:
                picked = sel_rows[g][:, ls] > 0.5
                m_new = jnp.where(picked, m_cand, m_prev)
                m_ref = jnp.where(picked, m_cand, POS_INF)
            return jnp.exp2(m_prev - m_new), m_new, jnp.exp2(s - m_ref).astype(BF16)

        def finish(g, ls, alpha, m_new, p):
            pv = _dot(vgs[g], p)
            acc_scr[g, :, ls] = alpha * acc_scr[g, :, ls] + pv[:hd, :]
            l_scr[g, :, ls] = alpha * l_scr[g, :, ls] + pv[hd:hd + 1, :]
            m_scr[g, :, ls] = m_new

        n = len(units)
        s_q, p_q = {}, {}
        for step in range(n + SKEW_S + SKEW_P):
            if step < n:
                s_q[step] = scores(*units[step])
            j = step - SKEW_S
            if 0 <= j < n:
                p_q[j] = probs(*units[j], s_q.pop(j))
            k = j - SKEW_P
            if 0 <= k < n:
                finish(*units[k], *p_q.pop(k))

    @pl.when(kjs[0] == qi)
    def _():
        attend(k_refs[0], vt_refs[0], kjs[0], True)

    for k_ref, vt_ref, kj in zip(k_refs, vt_refs, kjs):
        @pl.when(kj < qi)
        def _(k_ref=k_ref, vt_ref=vt_ref, kj=kj):
            attend(k_ref, vt_ref, kj, False)

    @pl.when(last_ref[p] == 1)
    def _():
        outs = []
        for g in range(ngrp):
            og = acc_scr[g] / l_scr[g]
            outs += [og[:, hh * tq:(hh + 1) * tq] for hh in range(grp)]
        o_ref[...] = jnp.concatenate(outs, axis=0).T


def _moba_prompt(q, k, vt, means, *, nb_batch, t_len, ngrp, grp, hd):
    tq = MOBA_BLOCK
    nq = t_len // tq
    nb = t_len // MOBA_BLOCK
    npast = MOBA_PAST_PER_STEP
    steps = []
    for i in range(nq):
        steps.append([i] + [i] * npast + [int(i == 0)])
        for j in range(0, i, npast):
            steps.append([i] + [min(j + s, i) for s in range(npast)] + [int(j + npast >= i)])
    tabs = [jnp.asarray(np.array([st[c] for st in steps], np.int32)) for c in range(npast + 2)]
    rows = grp * tq
    w = ngrp * grp * hd
    kvw = ngrp * hd
    kern = functools.partial(_moba_prompt_kernel, tq=tq, ngrp=ngrp, grp=grp, hd=hd,
                             scale=float(hd) ** -0.5)
    q_map = lambda b, p, *tb: (b * nq + tb[0][p], 0)
    k_spec = lambda s: pl.BlockSpec((MOBA_BLOCK, kvw), lambda b, p, *tb: (b * nb + tb[1 + s][p], 0))
    vt_spec = lambda s: pl.BlockSpec((None, kvw, MOBA_BLOCK), lambda b, p, *tb: (b, 0, tb[1 + s][p]))
    grid_spec = pltpu.PrefetchScalarGridSpec(
        num_scalar_prefetch=npast + 2,
        grid=(nb_batch, len(steps)),
        in_specs=[pl.BlockSpec((tq, w), q_map)]
        + [k_spec(s) for s in range(npast)] + [vt_spec(s) for s in range(npast)]
        + [pl.BlockSpec((nb, kvw), lambda b, p, *tb: (b, 0))],
        out_specs=pl.BlockSpec((tq, w), q_map),
        scratch_shapes=[pltpu.VMEM((ngrp, hd, rows), BF16),
                        pltpu.VMEM((ngrp, nb, rows), F32),
                        pltpu.VMEM((ngrp, 1, rows), F32),
                        pltpu.VMEM((ngrp, 1, rows), F32),
                        pltpu.VMEM((ngrp, hd, rows), F32)],
    )
    return pl.pallas_call(
        kern,
        grid_spec=grid_spec,
        out_shape=jax.ShapeDtypeStruct(q.shape, F32),
        compiler_params=_cparams(("parallel", "arbitrary")),
        name="moba_prompt",
    )(*tabs, q, *([k] * npast), *([vt] * npast), means)


def _sample_partial_kernel(pt_ref, q_ref, *refs, npg, ngrp, grp, hd, scale):
    k_pages = refs[:npg]
    v_pages = refs[npg:2 * npg]
    mean_ref, m_ref, l_ref, po_ref, qs_scr = refs[2 * npg:]
    step = pl.program_id(1)
    t_new = q_ref.shape[0]
    gr = grp * t_new

    @pl.when(step == 0)
    def _():
        for g in range(ngrp):
            qg = jnp.concatenate(
                [q_ref[:, (g * grp + hh) * hd:(g * grp + hh + 1) * hd] for hh in range(grp)], axis=0)
            qs_scr[g] = (qg * scale).astype(BF16)
        mean_ref[...] = jnp.zeros(mean_ref.shape, F32)
        m_ref[...] = jnp.zeros(m_ref.shape, F32)
        l_ref[...] = jnp.zeros(l_ref.shape, F32)

    ppb = MOBA_BLOCK // k_pages[0].shape[-1]
    bps = npg // ppb
    lane = lax.broadcasted_iota(jnp.int32, m_ref.shape, 1)
    lane_hd = lax.broadcasted_iota(jnp.int32, (hd, LANES), 1)
    blocks = [slice(blk * ppb, (blk + 1) * ppb) for blk in range(bps)]
    kts = [[jnp.concatenate([r[g] for r in k_pages[sl]], axis=1) for g in range(ngrp)]
           for sl in blocks]
    scs = [jnp.concatenate([_dot(qs_scr[g], kt[g].astype(BF16)) for g in range(ngrp)], axis=0)
           for kt in kts]
    ms = [jnp.max(sc, axis=1, keepdims=True) for sc in scs]
    prs = [jnp.exp(sc - m) for sc, m in zip(scs, ms)]
    ls = [jnp.sum(pr, axis=1, keepdims=True) for pr in prs]
    outs = []
    for sl, pr in zip(blocks, prs):
        prb = pr.astype(BF16)
        outs.append(jnp.concatenate(
            [_dot_nt(prb[g * gr:(g + 1) * gr, :],
                     jnp.concatenate([r[g] for r in v_pages[sl]], axis=1).astype(BF16))
             for g in range(ngrp)], axis=0))
    m_tile = m_ref[...]
    l_tile = l_ref[...]
    for blk in range(bps):
        bidx = step * bps + blk
        m_tile = jnp.where(lane == bidx, ms[blk], m_tile)
        l_tile = jnp.where(lane == bidx, ls[blk], l_tile)
        for g in range(ngrp):
            mean_col = jnp.sum(kts[blk][g], axis=1, keepdims=True) * (1.0 / MOBA_BLOCK)
            rs = slice(g * hd, (g + 1) * hd)
            mean_ref[rs, :] = jnp.where(lane_hd == bidx, mean_col, mean_ref[rs, :])
    m_ref[...] = m_tile
    l_ref[...] = l_tile
    ppl = LANES // hd
    for j in range(bps // ppl):
        po_ref[j] = jnp.concatenate(outs[j * ppl:(j + 1) * ppl], axis=1)


def _sample_partials(q, cache_k, cache_v, page_table, *, npg, ngrp, grp, hd):
    db, n_pages = page_table.shape
    page = cache_k.shape[-1]
    kvw = ngrp * hd
    t_new = q.shape[0] // db
    ppb = MOBA_BLOCK // page
    bps = npg // ppb
    ppl = LANES // hd
    nsteps = n_pages // npg
    nblk = n_pages // ppb
    nrow = ngrp * grp * t_new
    assert nblk <= LANES and bps % ppl == 0
    kern = functools.partial(_sample_partial_kernel, npg=npg, ngrp=ngrp, grp=grp, hd=hd,
                             scale=float(hd) ** -0.5)

    def page_spec(i):
        return pl.BlockSpec((None, ngrp, hd, page),
                            lambda b, s, pt: (pt[b * n_pages + s * npg + i], 0, 0, 0))

    grid_spec = pltpu.PrefetchScalarGridSpec(
        num_scalar_prefetch=1,
        grid=(db, nsteps),
        in_specs=[pl.BlockSpec((t_new, q.shape[1]), lambda b, s, pt: (b, 0))]
        + [page_spec(i) for i in range(npg)] + [page_spec(i) for i in range(npg)],
        out_specs=[pl.BlockSpec((None, kvw, LANES), lambda b, s, pt: (b, 0, 0)),
                   pl.BlockSpec((None, nrow, LANES), lambda b, s, pt: (b, 0, 0)),
                   pl.BlockSpec((None, nrow, LANES), lambda b, s, pt: (b, 0, 0)),
                   pl.BlockSpec((None, bps // ppl, nrow, LANES), lambda b, s, pt: (b, s, 0, 0))],
        scratch_shapes=[pltpu.VMEM((ngrp, grp * t_new, hd), BF16)],
    )
    return pl.pallas_call(
        kern,
        grid_spec=grid_spec,
        out_shape=[jax.ShapeDtypeStruct((db, kvw, LANES), F32),
                   jax.ShapeDtypeStruct((db, nrow, LANES), F32),
                   jax.ShapeDtypeStruct((db, nrow, LANES), F32),
                   jax.ShapeDtypeStruct((db, nblk // ppl, nrow, LANES), F32)],
        compiler_params=_cparams(("parallel", "arbitrary")),
        name="moba_sample_partials",
    )(page_table.reshape(-1), q, *([cache_k] * npg), *([cache_v] * npg))


def _sample_combine_kernel(q_ref, kn_ref, vn_ref, mean_ref, m_ref, l_ref, po_ref, o_ref,
                           *, nblk, ngrp, grp, hd, scale):
    t_new = q_ref.shape[0]
    gr = grp * t_new
    nrow = ngrp * gr
    gates, qss = [], []
    for g in range(ngrp):
        qg = jnp.concatenate(
            [q_ref[:, (g * grp + hh) * hd:(g * grp + hh + 1) * hd] for hh in range(grp)], axis=0)
        qss.append((qg * scale).astype(BF16))
        gates.append(_dot(qg, mean_ref[g * hd:(g + 1) * hd, :], HI))
    sel = _top_select(jnp.concatenate(gates, axis=0), nblk, 1) > 0.5
    m_all = m_ref[...]
    m_past = jnp.max(jnp.where(sel, m_all, NEG_INF), axis=1, keepdims=True)
    kpos = lax.broadcasted_iota(jnp.int32, (nrow, t_new), 1)
    qpos = jnp.bitwise_and(lax.broadcasted_iota(jnp.int32, (nrow, t_new), 0), t_new - 1)
    kn = kn_ref[...].astype(BF16)
    vn = vn_ref[...].astype(BF16)
    s_own = jnp.concatenate(
        [_dot_nt(qss[g], kn[:, g * hd:(g + 1) * hd]) for g in range(ngrp)], axis=0)
    s_own = jnp.where(kpos <= qpos, s_own, NEG_INF)
    m_fin = jnp.maximum(m_past, jnp.max(s_own, axis=1, keepdims=True))
    w = jnp.where(sel, jnp.exp(m_all - m_fin), 0.0)
    p_own = jnp.exp(s_own - m_fin)
    l_fin = (jnp.sum(w * l_ref[...], axis=1, keepdims=True)
             + jnp.sum(p_own, axis=1, keepdims=True))
    pb = p_own.astype(BF16)
    o_own = jnp.concatenate(
        [_dot(pb[g * gr:(g + 1) * gr, :], vn[:, g * hd:(g + 1) * hd]) for g in range(ngrp)], axis=0)
    ppl = LANES // hd
    lane = lax.broadcasted_iota(jnp.int32, (nrow, LANES), 1)
    acc = jnp.zeros((nrow, LANES), F32)
    for j in range(nblk // ppl):
        wj = w[:, j * ppl:j * ppl + 1]
        for i in range(1, ppl):
            wj = jnp.where(lane < i * hd, wj, w[:, j * ppl + i:j * ppl + i + 1])
        acc = acc + wj * po_ref[j]
    o = o_own
    for i in range(ppl):
        o = o + acc[:, i * hd:(i + 1) * hd]
    o = o / l_fin
    o_ref[...] = jnp.concatenate(
        [o[r * t_new:(r + 1) * t_new, :] for r in range(ngrp * grp)], axis=1)


def _sample_combine(q, k_new, v_new, means_t, m_part, l_part, po, *, db, nblk, ngrp, grp, hd):
    t_new = q.shape[0] // db
    nrow = ngrp * grp * t_new
    kvw = ngrp * hd
    kern = functools.partial(_sample_combine_kernel, nblk=nblk, ngrp=ngrp, grp=grp, hd=hd,
                             scale=float(hd) ** -0.5)
    return pl.pallas_call(
        kern,
        grid=(db,),
        in_specs=[pl.BlockSpec((t_new, q.shape[1]), lambda b: (b, 0)),
                  pl.BlockSpec((t_new, kvw), lambda b: (b, 0)),
                  pl.BlockSpec((t_new, kvw), lambda b: (b, 0)),
                  pl.BlockSpec((None, kvw, LANES), lambda b: (b, 0, 0)),
                  pl.BlockSpec((None, nrow, LANES), lambda b: (b, 0, 0)),
                  pl.BlockSpec((None, nrow, LANES), lambda b: (b, 0, 0)),
                  pl.BlockSpec((None,) + po.shape[1:], lambda b: (b, 0, 0, 0))],
        out_specs=pl.BlockSpec((t_new, q.shape[1]), lambda b: (b, 0)),
        out_shape=jax.ShapeDtypeStruct(q.shape, F32),
        compiler_params=_cparams(("parallel",)),
        name="moba_sample_combine",
    )(q, k_new, v_new, means_t, m_part, l_part, po)


def _rope_tables(pos, hd, reps):
    half = hd // 2
    inv_freq = ROPE_THETA ** (-jnp.arange(half, dtype=F32) / half)
    ang = pos.astype(F32)[:, None] * inv_freq[None, :]
    cos = jnp.cos(ang)
    sin = jnp.sin(ang)
    per_vreg = LANES // hd
    cos_t = jnp.tile(jnp.concatenate([cos, cos], axis=1), (reps, per_vreg))
    sin_t = jnp.tile(jnp.concatenate([-sin, sin], axis=1), (reps, per_vreg))
    return cos_t, sin_t


def _trunk(x, pos, conv_state8, delta_state, attend, wts, *, nseq, t_len, tm, prep_tt,
           delta_ct, delta_c, dims, want_t=False):
    nh, dk, conv_dim, n_a, depth, alpha, bh, kvh, hd = dims
    rows = nseq * t_len
    new_conv, new_delta = [], []
    k_sh = v_sh = t_sh = None
    for layer in range(depth):
        if layer < n_a:
            cw = wts["a_conv"][layer].shape[0]
            prep_args = (conv_state8[layer], wts["a_conv"][layer], wts["a_log"][layer],
                         wts["a_dtb"][layer])
            prep_kw = dict(nseq=nseq, t_len=t_len, tt=prep_tt, conv_dim=conv_dim, nh=nh, dk=dk)
            if t_len >= tm:
                qkv, gb, z_src, tail = _gdn_in_fused(x, wts["a_w_in"][layer], *prep_args, **prep_kw)
                z_blk = 0
                new_conv.append(tail[:, SUBLANES - (cw - 1):, :])
            else:
                proj = _matmul(x, wts["a_w_in"][layer], rows, wts["a_w_in"][layer].shape[1] // 3,
                               "in_proj")
                qkv, gb = _gdn_prep(proj, *prep_args, **prep_kw)
                z_src, z_blk = proj, conv_dim // (nh * dk)
                new_conv.append(proj.reshape(nseq, t_len, -1)[:, t_len - (cw - 1):, :conv_dim])
            o, s_new = _delta_rule(qkv, gb, delta_state[layer], nseq=nseq, t_len=t_len,
                                   ct=delta_ct, c=delta_c, nh=nh, dk=dk)
            new_delta.append(s_new)
            mix, w_mix = o, wts["a_w_out"][layer]
            mix_kw = dict(z_src=z_src, z_blk=z_blk, norm_w=wts["a_norm"][layer], nh=nh, dv=dk)
        else:
            j = layer - n_a
            w_all = wts["b_w_qkv"][j]
            cos_t, sin_t = _rope_tables(pos, hd, (tm // t_len) if tm > t_len else 1)
            q, k_l, v_l, *t_l = _qkv_rope(x, w_all, cos_t, sin_t, tm=tm, nq=bh * hd, nk=kvh * hd,
                                          half=hd // 2, t_seq=t_len if want_t else None)
            if layer == n_a:
                k_sh, v_sh, t_sh = k_l, v_l, t_l
            mix, w_mix, mix_kw = attend(q, k_sh, v_sh, *t_sh), wts["b_w_o"][j], {}
        ln1 = (wts["ln_g"][layer][0], wts["ln_b"][layer][0])
        ln2 = (wts["ln_g"][layer][1], wts["ln_b"][layer][1])
        x = _mixer_ffn(mix, x, w_mix, ln1, wts["ffn_wg"][layer], wts["ffn_wu"][layer],
                       wts["ffn_wd"][layer], ln2, tm=tm, tf=256, alpha=alpha, **mix_kw)
    return x, jnp.stack(new_conv), jnp.stack(new_delta), k_sh, v_sh, t_sh


def kernel(x_prompt, x_sample, state_conv, state_delta, cache_k, cache_v, page_table, ln_g, ln_b,
           a_w_in, a_conv, a_log_decay, a_dt_bias, a_norm, a_w_out, kv_w_k, kv_w_v, b_w_q, b_w_o,
           ffn_w_gate, ffn_w_up, ffn_w_down):
    bp, tp, d = x_prompt.shape
    db, ts, _ = x_sample.shape
    depth = ln_g.shape[0]
    n_a = a_w_in.shape[0]
    nh = a_log_decay.shape[1]
    dk = state_delta.shape[-2]
    conv_dim = a_conv.shape[-1]
    cw = a_conv.shape[1]
    n_pool, page, kvh, hd = cache_k.shape
    bh = b_w_q.shape[-1] // hd
    grp = bh // kvh
    n_pages = page_table.shape[1]
    past = n_pages * page
    alpha = (2.0 * depth) ** 0.25
    assert past % MOBA_BLOCK == 0 and ts <= MOBA_BLOCK and ts >= cw - 1 and tp >= cw - 1
    assert state_delta.shape[-1] == dk and conv_dim == 3 * nh * dk

    in_w = a_w_in.shape[-1]
    in_pad = -(-in_w // (3 * LANES)) * (3 * LANES)
    pad_lanes = lambda v: jnp.pad(v, ((0, 0), (0, LANES - v.shape[-1])))[:, None, :]
    wts = {
        "a_w_in": [jnp.pad(a_w_in[l].astype(BF16), ((0, 0), (0, in_pad - in_w))) for l in range(n_a)],
        "a_conv": a_conv,
        "a_log": pad_lanes(a_log_decay),
        "a_dtb": pad_lanes(a_dt_bias),
        "a_norm": a_norm[:, None, :],
        "a_w_out": a_w_out.astype(BF16),
        "b_w_qkv": jnp.concatenate(
            [b_w_q, jnp.broadcast_to(kv_w_k, (b_w_q.shape[0],) + kv_w_k.shape),
             jnp.broadcast_to(kv_w_v, (b_w_q.shape[0],) + kv_w_v.shape)], axis=-1).astype(BF16),
        "b_w_o": b_w_o.astype(BF16),
        "ffn_wg": ffn_w_gate.astype(BF16),
        "ffn_wu": ffn_w_up.astype(BF16),
        "ffn_wd": ffn_w_down.astype(BF16),
        "ln_g": ln_g[:, :, None, :],
        "ln_b": ln_b[:, :, None, :],
    }
    dims = (nh, dk, conv_dim, n_a, depth, alpha, bh, kvh, hd)

    nb = tp // MOBA_BLOCK

    def attend_prompt(q, k, v, kt, vt):
        total = bp * nb
        step = SUBLANES if total % SUBLANES == 0 else total
        means = _block_means(k, nblk_step=step)
        return _moba_prompt(q, k, vt, means, nb_batch=bp, t_len=tp, ngrp=kvh, grp=grp, hd=hd)

    tm_p = 512 if (bp * tp) % 512 == 0 else 256
    y_p, p_conv, p_delta, _, _, (p_kt, p_vt) = _trunk(
        x_prompt.reshape(bp * tp, d), jnp.arange(tp, dtype=jnp.int32),
        jnp.zeros((n_a, bp, SUBLANES, conv_dim), F32), jnp.zeros((n_a, bp, nh, dk, dk), F32),
        attend_prompt, wts, nseq=bp, t_len=tp, tm=tm_p, prep_tt=tm_p,
        delta_ct=DELTA_CHUNK * DELTA_CHUNKS_PER_ITER,
        delta_c=DELTA_CHUNK, dims=dims, want_t=True)
    p_k = jnp.transpose(p_kt.reshape(bp, kvh, hd, tp), (0, 3, 1, 2))
    p_v = jnp.transpose(p_vt.reshape(bp, kvh, hd, tp), (0, 3, 1, 2))

    ck = jnp.transpose(cache_k, (0, 2, 3, 1))
    cv = jnp.transpose(cache_v, (0, 2, 3, 1))
    npg = next(n for n in (16, 8, 2 * (MOBA_BLOCK // page)) if n_pages % n == 0)

    def attend_sample(q, k_new, v_new):
        means_t, m_part, l_part, po = _sample_partials(q, ck, cv, page_table, npg=npg, ngrp=kvh,
                                                       grp=grp, hd=hd)
        return _sample_combine(q, k_new, v_new, means_t, m_part, l_part, po, db=db,
                               nblk=past // MOBA_BLOCK, ngrp=kvh, grp=grp, hd=hd)

    conv8 = jnp.pad(state_conv, ((0, 0), (0, 0), (SUBLANES - (cw - 1), 0), (0, 0)))
    y_s, s_conv, s_delta, s_k, s_v, _ = _trunk(
        x_sample.reshape(db * ts, d), past + jnp.arange(ts, dtype=jnp.int32), conv8, state_delta,
        attend_sample, wts, nseq=db, t_len=ts, tm=db * ts, prep_tt=ts, delta_ct=ts, delta_c=ts,
        dims=dims)

    return (y_p.reshape(bp, tp, d), y_s.reshape(db, ts, d), p_conv, p_delta, p_k, p_v,
            s_conv, s_delta, s_k.reshape(db, ts, kvh, hd), s_v.reshape(db, ts, kvh, hd))
```

```python
import functools

import numpy as np
import jax
import jax.numpy as jnp
from jax import lax
from jax.experimental import pallas as pl
from jax.experimental.pallas import tpu as pltpu

F32 = jnp.float32
BF16 = jnp.bfloat16
HI = lax.Precision.HIGHEST

LANES = 128
SUBLANES = 8
VMEM_LIMIT_MB = 56

MOBA_BLOCK = 256
MOBA_TOPK = 3
MOBA_PAST_PER_STEP = 2
UNIT_LANES = 256
SKEW_S = 3
SKEW_P = 2
DELTA_CHUNK = 64
DELTA_SOLVE_GROUP = 4
DELTA_CHUNKS_PER_ITER = 8
ROPE_THETA = 10000.0
LN_EPS = 1e-5
RMS_EPS = 1e-6
L2_EPS = 1e-6
NEG_INF = float("-inf")
POS_INF = float("inf")
LOG2E = 1.4426950408889634


def _cparams(sem):
    return pltpu.CompilerParams(dimension_semantics=sem,
                                vmem_limit_bytes=VMEM_LIMIT_MB * 1024 * 1024)


def _dot(a, b, prec=None):
    return jnp.dot(a, b, precision=prec, preferred_element_type=F32)


def _dot_nt(a, b, prec=None):
    return lax.dot_general(a, b, (((1,), (1,)), ((), ())), precision=prec,
                           preferred_element_type=F32)


def _dot_tn(a, b, prec=None):
    return lax.dot_general(a, b, (((0,), (0,)), ((), ())), precision=prec,
                           preferred_element_type=F32)


def _split(a):
    hi = a.astype(BF16)
    return hi, (a - hi.astype(F32)).astype(BF16)


def _dot3(a, b):
    return _dot(a[0], b[0]) + (_dot(a[0], b[1]) + _dot(a[1], b[0]))


def _sigmoid(x):
    return 1.0 / (1.0 + jnp.exp(-x))


def _layer_norm(v, g, b):
    mu = jnp.mean(v, axis=-1, keepdims=True)
    d = v - mu
    var = jnp.mean(d * d, axis=-1, keepdims=True)
    return d * lax.rsqrt(var + LN_EPS) * g + b


def _mm_kernel(x_ref, w_ref, o_ref):
    o_ref[...] = _dot(x_ref[...].astype(BF16), w_ref[...])


def _matmul(x, w, tm, tn, name):
    r, k = x.shape
    n = w.shape[1]
    return pl.pallas_call(
        _mm_kernel,
        grid=(r // tm, n // tn),
        in_specs=[pl.BlockSpec((tm, k), lambda i, j: (i, 0)),
                  pl.BlockSpec((k, tn), lambda i, j: (0, j))],
        out_specs=pl.BlockSpec((tm, tn), lambda i, j: (i, j)),
        out_shape=jax.ShapeDtypeStruct((r, n), F32),
        compiler_params=_cparams(("parallel", "parallel")),
        name=name,
    )(x, w)


def _conv_norm(xp_ref, cw_ref, qkv_ref, cols, *, tt, conv_w, n_q, n_qk, q_scale):
    hist = SUBLANES
    for c in cols:
        cs = slice(c * LANES, (c + 1) * LANES)
        full = xp_ref[0:hist + tt, cs]
        y = pltpu.roll(full, conv_w - 1, 0)[hist:] * cw_ref[0:1, cs]
        for i in range(1, conv_w - 1):
            y = y + pltpu.roll(full, conv_w - 1 - i, 0)[hist:] * cw_ref[i:i + 1, cs]
        y = y + full[hist:] * cw_ref[conv_w - 1:conv_w, cs]
        y = y * _sigmoid(y)
        if c < n_qk:
            r = lax.rsqrt(jnp.sum(y * y, axis=-1, keepdims=True) + L2_EPS)
            y = y * (r * q_scale if c < n_q else r)
        qkv_ref[:, cs] = y


def _decay_beta(ab, alog, dtb, nh):
    lane = lax.broadcasted_iota(jnp.int32, ab.shape, 1)
    sp = ab + dtb
    softplus = jnp.maximum(sp, 0.0) + jnp.log1p(jnp.exp(-jnp.abs(sp)))
    return jnp.where(lane < nh, -jnp.exp(alog) * softplus, _sigmoid(ab))


def _prep_kernel(cur_ref, prev_ref, st_ref, ab_ref, cw_ref, alog_ref, dtb_ref,
                 qkv_ref, gb_ref, xp_ref, *, tt, conv_w, n_q, n_qk, nh, q_scale):
    t = pl.program_id(1)
    hist = SUBLANES

    @pl.when(t == 0)
    def _():
        xp_ref[0:hist, :] = st_ref[...]

    @pl.when(t != 0)
    def _():
        xp_ref[0:hist, :] = prev_ref[...]

    xp_ref[hist:hist + tt, :] = cur_ref[...]
    _conv_norm(xp_ref, cw_ref, qkv_ref, range(cur_ref.shape[1] // LANES), tt=tt, conv_w=conv_w,
               n_q=n_q, n_qk=n_qk, q_scale=q_scale)
    gb_ref[...] = _decay_beta(ab_ref[...], alog_ref[...], dtb_ref[...], nh)


def _gdn_in_kernel(x_ref, w_ref, st_ref, cw_ref, alog_ref, dtb_ref, qkv_ref, gb_ref, z_ref, tail_ref,
                   xp_ref, *, tt, conv_dim, zw, conv_w, n_q, n_qk, nh, q_scale):
    hist = SUBLANES

    @pl.when(pl.program_id(1) == 0)
    def _():
        xp_ref[0:hist, :] = st_ref[...]

    xb = x_ref[...].astype(BF16)
    mxu_w = 2 * LANES
    for c in range(conv_dim // mxu_w):
        cs = slice(c * mxu_w, (c + 1) * mxu_w)
        xp_ref[hist:hist + tt, cs] = _dot(xb, w_ref[:, cs])
        _conv_norm(xp_ref, cw_ref, qkv_ref, range(2 * c, 2 * c + 2), tt=tt, conv_w=conv_w,
                   n_q=n_q, n_qk=n_qk, q_scale=q_scale)
    z_ref[...] = _dot(xb, w_ref[:, conv_dim:conv_dim + zw])
    ab = _dot(xb, w_ref[:, conv_dim + zw:conv_dim + zw + LANES])
    gb_ref[...] = _decay_beta(ab, alog_ref[...], dtb_ref[...], nh)
    tail = xp_ref[tt:tt + hist, :]
    tail_ref[...] = tail
    xp_ref[0:hist, :] = tail


def _gdn_in_fused(x, w, state8, conv_w, alog, dtb, *, nseq, t_len, tt, conv_dim, nh, dk):
    rows, d = x.shape
    nt = t_len // tt
    cw = conv_w.shape[0]
    zw = nh * dk
    kern = functools.partial(_gdn_in_kernel, tt=tt, conv_dim=conv_dim, zw=zw, conv_w=cw, n_q=nh,
                             n_qk=2 * nh, nh=nh, q_scale=float(dk) ** -0.5)
    row = lambda width: pl.BlockSpec((tt, width), lambda s, t: (s * nt + t, 0))
    const = lambda shape: pl.BlockSpec(shape, lambda s, t: (0, 0))
    per_seq = pl.BlockSpec((None, SUBLANES, conv_dim), lambda s, t: (s, 0, 0))
    return pl.pallas_call(
        kern,
        grid=(nseq, nt),
        in_specs=[row(d),
                  pl.BlockSpec(w.shape, lambda s, t: (0, 0), pipeline_mode=pl.Buffered(1)),
                  per_seq, const((cw, conv_dim)), const((1, LANES)), const((1, LANES))],
        out_specs=[row(conv_dim), row(LANES), row(zw), per_seq],
        out_shape=[jax.ShapeDtypeStruct((rows, conv_dim), F32),
                   jax.ShapeDtypeStruct((rows, LANES), F32),
                   jax.ShapeDtypeStruct((rows, zw), F32),
                   jax.ShapeDtypeStruct((nseq, SUBLANES, conv_dim), F32)],
        scratch_shapes=[pltpu.VMEM((tt + SUBLANES, conv_dim), F32)],
        compiler_params=_cparams(("parallel", "arbitrary")),
        name="gdn_in_fused",
    )(x, w, state8, conv_w, alog, dtb)


def _gdn_prep(proj, state8, conv_w, alog, dtb, *, nseq, t_len, tt, conv_dim, nh, dk):
    rows = nseq * t_len
    nt = t_len // tt
    cw = conv_w.shape[0]
    ab_blk = (conv_dim + nh * dk) // LANES
    kern = functools.partial(_prep_kernel, tt=tt, conv_w=cw, n_q=nh, n_qk=2 * nh, nh=nh,
                             q_scale=float(dk) ** -0.5)
    tpb = tt // SUBLANES
    return pl.pallas_call(
        kern,
        grid=(nseq, nt),
        in_specs=[
            pl.BlockSpec((tt, conv_dim), lambda s, t: (s * nt + t, 0)),
            pl.BlockSpec((SUBLANES, conv_dim),
                         lambda s, t: (jnp.maximum((s * nt + t) * tpb - 1, 0), 0)),
            pl.BlockSpec((None, SUBLANES, conv_dim), lambda s, t: (s, 0, 0)),
            pl.BlockSpec((tt, LANES), lambda s, t: (s * nt + t, ab_blk)),
            pl.BlockSpec((cw, conv_dim), lambda s, t: (0, 0)),
            pl.BlockSpec((1, LANES), lambda s, t: (0, 0)),
            pl.BlockSpec((1, LANES), lambda s, t: (0, 0)),
        ],
        out_specs=[pl.BlockSpec((tt, conv_dim), lambda s, t: (s * nt + t, 0)),
                   pl.BlockSpec((tt, LANES), lambda s, t: (s * nt + t, 0))],
        out_shape=[jax.ShapeDtypeStruct((rows, conv_dim), F32),
                   jax.ShapeDtypeStruct((rows, LANES), F32)],
        scratch_shapes=[pltpu.VMEM((tt + SUBLANES, conv_dim), F32)],
        compiler_params=_cparams(("parallel", "arbitrary")),
        name="gdn_prep",
    )(proj, proj, state8, proj, conv_w, alog, dtb)


def _delta_kernel(q_ref, k_ref, v_ref, gb_ref, s0_ref, o_ref, s_ref, st_scr, *, sb, ct, c, nh, dk):
    t = pl.program_id(1)

    @pl.when(t == 0)
    def _():
        st_scr[...] = s0_ref[...]

    row = lax.broadcasted_iota(jnp.int32, (c, c), 0)
    col = lax.broadcasted_iota(jnp.int32, (c, c), 1)
    causal = row >= col
    strict = row > col
    tril = causal.astype(F32)
    eye = (row == col).astype(F32)
    n_sq = max(int(np.ceil(np.log2(c))) - 1, 0)

    def prepare(rs, seq):
        gbc = gb_ref[rs, :]
        gc = _dot(tril, gbc, HI)
        gt = gc.T
        eg = jnp.exp(gc)
        units = []
        for h in range(nh):
            cs = slice(h * dk, (h + 1) * dk)
            qh = q_ref[rs, cs]
            kh = k_ref[rs, cs]
            vh = v_ref[rs, cs]
            gcol = gc[:, h:h + 1]
            grow = gt[h:h + 1, :]
            bcol = gbc[:, nh + h:nh + h + 1]
            egcol = eg[:, h:h + 1]
            decay = jnp.where(causal, jnp.exp(jnp.where(causal, gcol - grow, 0.0)), 0.0)
            kb = kh.astype(BF16)
            kk = _dot_nt(kb, kb)
            qk = _dot_nt(qh.astype(BF16), kb)
            glast = grow[:, c - 1:c]
            units.append(dict(
                rs=rs, h=h, seq=seq,
                x=-jnp.where(strict, bcol * kk * decay, 0.0),
                qkm=jnp.where(causal, qk * decay, 0.0).astype(BF16),
                rhs=jnp.concatenate([vh * bcol, kh * (bcol * egcol)], axis=-1),
                qd=(qh * egcol).astype(BF16),
                kd=(kh * jnp.exp(glast - gcol)).astype(BF16),
                glast=jnp.exp(glast)))
        return units

    def solve(units):
        ps = [u["x"] for u in units]
        tms = [eye + p for p in ps]
        for _ in range(n_sq):
            sp = [_split(p) for p in ps]
            ps = [_dot3(s, s) for s in sp]
            sp = [_split(p) for p in ps]
            tms = [tm + _dot3(_split(tm), s) for tm, s in zip(tms, sp)]
        return [_dot3(_split(tm), _split(u["rhs"])) for tm, u in zip(tms, units)]

    def update(u, sol):
        h, seq = u["h"], u["seq"]
        s = st_scr[seq, h]
        s_b = s.astype(BF16)
        v_new = sol[:, :dk] - _dot(sol[:, dk:].astype(BF16), s_b)
        vb = v_new.astype(BF16)
        o_ref[u["rs"], h * dk:(h + 1) * dk] = _dot(u["qd"], s_b) + _dot(u["qkm"], vb)
        st_scr[seq, h] = s * u["glast"] + _dot_tn(u["kd"], vb)

    def chunks(r0, n):
        groups = []
        for seq in range(sb):
            for i0 in range(0, n, DELTA_SOLVE_GROUP):
                units = []
                for i in range(i0, min(i0 + DELTA_SOLVE_GROUP, n)):
                    units += prepare(pl.ds(seq * ct + r0 + i * c, c), seq)
                groups.append(units)
        if sb > 1:
            groups = [sum(groups, [])]
        solved = None
        for units in groups:
            sols = solve(units)
            if solved is not None:
                for u, sol in zip(*solved):
                    update(u, sol)
            solved = (units, sols)
        for u, sol in zip(*solved):
            update(u, sol)

    n_chunks = ct // c
    per_iter = DELTA_CHUNKS_PER_ITER if n_chunks % DELTA_CHUNKS_PER_ITER == 0 else 1
    if n_chunks == per_iter:
        chunks(0, per_iter)
    else:
        def body(ci, carry):
            chunks(pl.multiple_of(ci * (per_iter * c), per_iter * c), per_iter)
            return carry
        lax.fori_loop(0, n_chunks // per_iter, body, 0)

    @pl.when(t == pl.num_programs(1) - 1)
    def _():
        s_ref[...] = st_scr[...]


def _delta_rule(qkv, gb, s0, *, nseq, t_len, ct, c, nh, dk):
    rows = nseq * t_len
    nt = t_len // ct
    w = nh * dk
    sb = DELTA_SOLVE_GROUP if (nt == 1 and ct == c and nseq % DELTA_SOLVE_GROUP == 0) else 1
    kern = functools.partial(_delta_kernel, sb=sb, ct=ct, c=c, nh=nh, dk=dk)
    return pl.pallas_call(
        kern,
        grid=(nseq // sb, nt),
        in_specs=[
            pl.BlockSpec((sb * ct, w), lambda s, t: (s * nt + t, 0)),
            pl.BlockSpec((sb * ct, w), lambda s, t: (s * nt + t, 1)),
            pl.BlockSpec((sb * ct, w), lambda s, t: (s * nt + t, 2)),
            pl.BlockSpec((sb * ct, LANES), lambda s, t: (s * nt + t, 0)),
            pl.BlockSpec((sb, nh, dk, dk), lambda s, t: (s, 0, 0, 0)),
        ],
        out_specs=[pl.BlockSpec((sb * ct, w), lambda s, t: (s * nt + t, 0)),
                   pl.BlockSpec((sb, nh, dk, dk), lambda s, t: (s, 0, 0, 0))],
        out_shape=[jax.ShapeDtypeStruct((rows, w), F32),
                   jax.ShapeDtypeStruct((nseq, nh, dk, dk), F32)],
        scratch_shapes=[pltpu.VMEM((sb, nh, dk, dk), F32)],
        compiler_params=_cparams(("parallel", "arbitrary")),
        name="delta_rule",
    )(qkv, qkv, qkv, gb, s0)


def _mixer_ffn_kernel(*refs, gated, nh, dv, tf, alpha):
    if gated:
        o_ref, z_ref, nw_ref, *refs = refs
    else:
        o_ref, *refs = refs
    (x_ref, wo_ref, g1_ref, b1_ref, wg_ref, wu_ref, wd_ref, g2_ref, b2_ref,
     out_ref, a_scr, h_scr, acc_scr) = refs
    if gated:
        for h in range(nh):
            cs = slice(h * dv, (h + 1) * dv)
            oh = o_ref[:, cs]
            zh = z_ref[:, cs]
            oh = oh * lax.rsqrt(jnp.mean(oh * oh, axis=-1, keepdims=True) + RMS_EPS) * nw_ref[...]
            a_scr[:, cs] = (oh * (zh * _sigmoid(zh))).astype(BF16)
    else:
        a_scr[...] = o_ref[...].astype(BF16)
    h_scr[...] = _layer_norm(alpha * x_ref[...] + _dot(a_scr[...], wo_ref[...]),
                             g1_ref[...], b1_ref[...])
    hb = h_scr[...].astype(BF16)
    for c in range(wg_ref.shape[1] // tf):
        fs = slice(c * tf, (c + 1) * tf)
        gate = _dot(hb, wg_ref[:, fs])
        up = _dot(hb, wu_ref[:, fs])
        act = (gate * _sigmoid(gate) * up).astype(BF16)
        down = _dot(act, wd_ref[fs, :])
        if c == 0:
            acc_scr[...] = down
        else:
            acc_scr[...] += down
    out_ref[...] = _layer_norm(alpha * h_scr[...] + acc_scr[...], g2_ref[...], b2_ref[...])


def _mixer_ffn(o, x, wo, ln1, wg, wu, wd, ln2, *, tm, tf, alpha, z_src=None, z_blk=0,
               norm_w=None, nh=1, dv=1):
    rows, d_in = o.shape
    d = x.shape[1]
    f = wg.shape[1]
    gated = z_src is not None
    kern = functools.partial(_mixer_ffn_kernel, gated=gated, nh=nh, dv=dv, tf=tf, alpha=alpha)
    row_spec = lambda width: pl.BlockSpec((tm, width), lambda i: (i, 0))
    const = lambda shape: pl.BlockSpec(shape, lambda i: (0, 0))
    resident = lambda shape: pl.BlockSpec(shape, lambda i: (0, 0), pipeline_mode=pl.Buffered(1))
    in_specs = [row_spec(d_in)]
    args = [o]
    if gated:
        in_specs += [pl.BlockSpec((tm, d_in), lambda i: (i, z_blk)), const((1, dv))]
        args += [z_src, norm_w]
    in_specs += [row_spec(d), resident((d_in, d)), const((1, d)), const((1, d)),
                 resident((d, f)), resident((d, f)), resident((f, d)), const((1, d)), const((1, d))]
    args += [x, wo, ln1[0], ln1[1], wg, wu, wd, ln2[0], ln2[1]]
    return pl.pallas_call(
        kern,
        grid=(rows // tm,),
        in_specs=in_specs,
        out_specs=row_spec(d),
        out_shape=jax.ShapeDtypeStruct((rows, d), F32),
        scratch_shapes=[pltpu.VMEM((tm, d_in), BF16), pltpu.VMEM((tm, d), F32),
                        pltpu.VMEM((tm, d), F32)],
        compiler_params=_cparams(("parallel",)),
        name="mixer_ffn_ln",
    )(*args)


def _qkv_kernel(x_ref, w_ref, cos_ref, sin_ref, q_ref, k_ref, v_ref, *t_refs, nq, nk, half):
    y = _dot(x_ref[...].astype(BF16), w_ref[...])
    cos = cos_ref[...]
    sin = sin_ref[...]
    lane = lax.broadcasted_iota(jnp.int32, cos.shape, 1)
    first = jnp.bitwise_and(lane, 2 * half - 1) < half

    def rope(xg):
        partner = jnp.where(first, pltpu.roll(xg, LANES - half, 1), pltpu.roll(xg, half, 1))
        return xg * cos + partner * sin

    for c in range(nq // LANES):
        q_ref[:, c * LANES:(c + 1) * LANES] = rope(y[:, c * LANES:(c + 1) * LANES])
    k = jnp.concatenate([rope(y[:, nq + c * LANES:nq + (c + 1) * LANES])
                         for c in range(nk // LANES)], axis=1)
    k_ref[...] = k
    v_ref[...] = y[:, nq + nk:]
    if t_refs:
        kt_ref, vt_ref = t_refs
        kt_ref[...] = k.T
        vt_ref[...] = y[:, nq + nk:].T


def _qkv_rope(x, w, cos, sin, *, tm, nq, nk, half, t_seq=None):
    rows, d = x.shape
    n_tab = cos.shape[0] // tm
    kern = functools.partial(_qkv_kernel, nq=nq, nk=nk, half=half)
    out_specs = [pl.BlockSpec((tm, nq), lambda i: (i, 0)),
                 pl.BlockSpec((tm, nk), lambda i: (i, 0)),
                 pl.BlockSpec((tm, nk), lambda i: (i, 0))]
    out_shape = [jax.ShapeDtypeStruct((rows, nq), F32),
                 jax.ShapeDtypeStruct((rows, nk), F32),
                 jax.ShapeDtypeStruct((rows, nk), F32)]
    if t_seq is not None:
        nt = t_seq // tm
        out_specs += [pl.BlockSpec((None, nk, tm), lambda i: (i // nt, 0, i % nt))] * 2
        out_shape += [jax.ShapeDtypeStruct((rows // t_seq, nk, t_seq), F32)] * 2
    return pl.pallas_call(
        kern,
        grid=(rows // tm,),
        in_specs=[pl.BlockSpec((tm, d), lambda i: (i, 0)),
                  pl.BlockSpec((d, nq + 2 * nk), lambda i: (0, 0)),
                  pl.BlockSpec((tm, LANES), lambda i: (i % n_tab, 0)),
                  pl.BlockSpec((tm, LANES), lambda i: (i % n_tab, 0))],
        out_specs=out_specs,
        out_shape=out_shape,
        compiler_params=_cparams(("parallel",)),
        name="qkv_rope",
    )(x, w, cos, sin)


def _kmeans_kernel(k_ref, m_ref, *, nblk):
    kb = k_ref[...].reshape(nblk, MOBA_BLOCK, k_ref.shape[1])
    m_ref[...] = jnp.sum(kb, axis=1) * (1.0 / MOBA_BLOCK)


def _block_means(k, *, nblk_step):
    rows, w = k.shape
    total = rows // MOBA_BLOCK
    return pl.pallas_call(
        functools.partial(_kmeans_kernel, nblk=nblk_step),
        grid=(total // nblk_step,),
        in_specs=[pl.BlockSpec((nblk_step * MOBA_BLOCK, w), lambda i: (i, 0))],
        out_specs=pl.BlockSpec((nblk_step, w), lambda i: (i, 0)),
        out_shape=jax.ShapeDtypeStruct((total, w), F32),
        compiler_params=_cparams(("parallel",)),
        name="block_means",
    )(k)


def _top_select(gate, n_past, axis):
    n = gate.shape[axis]
    idx = lax.broadcasted_iota(jnp.int32, gate.shape, axis).astype(F32)
    gm = jnp.where(idx < n_past, gate, NEG_INF)
    sel = jnp.zeros(gate.shape, F32)
    for r in range(MOBA_TOPK):
        mx = jnp.max(gm, axis=axis, keepdims=True)
        first = jnp.min(jnp.where(gm == mx, idx, float(n)), axis=axis, keepdims=True)
        pick = idx == first
        keep = jnp.where(r < n_past, 1.0, 0.0).astype(F32)
        sel = jnp.maximum(sel, jnp.where(pick, keep, 0.0))
        gm = jnp.where(pick, NEG_INF, gm)
    return sel


def _moba_prompt_kernel(*refs, tq, ngrp, grp, hd, scale):
    npast = MOBA_PAST_PER_STEP
    qi_ref, kj_refs, last_ref = refs[0], refs[1:1 + npast], refs[1 + npast]
    q_ref = refs[2 + npast]
    k_refs = refs[3 + npast:3 + 2 * npast]
    vt_refs = refs[3 + 2 * npast:3 + 3 * npast]
    mean_ref, o_ref, qs_scr, sel_scr, m_scr, l_scr, acc_scr = refs[3 + 3 * npast:]
    p = pl.program_id(1)
    qi = qi_ref[p]
    kjs = [r[p] for r in kj_refs]
    rows = grp * tq

    @pl.when(kjs[0] == qi)
    def _():
        qt = q_ref[...].T
        for g in range(ngrp):
            qg = jnp.concatenate(
                [qt[(g * grp + hh) * hd:(g * grp + hh + 1) * hd, :] for hh in range(grp)], axis=1)
            qs_scr[g] = (qg * (scale * LOG2E)).astype(BF16)
            gate = _dot(mean_ref[:, g * hd:(g + 1) * hd], qg, HI)
            sel_scr[g] = _top_select(gate, qi, 0)
        m_scr[...] = jnp.full(m_scr.shape, NEG_INF, F32)
        l_scr[...] = jnp.zeros(l_scr.shape, F32)
        acc_scr[...] = jnp.zeros(acc_scr.shape, F32)

    ones_rows = jnp.ones((2 * SUBLANES, MOBA_BLOCK), BF16)

    def attend(k_ref, vt_ref, kj, diagonal):
        kgs = [k_ref[:, g * hd:(g + 1) * hd].astype(BF16) for g in range(ngrp)]
        vgs = [jnp.concatenate([vt_ref[g * hd:(g + 1) * hd, :].astype(BF16), ones_rows], axis=0)
               for g in range(ngrp)]
        sel_rows = None if diagonal else [sel_scr[g, pl.ds(kj, 1), :] for g in range(ngrp)]
        units = [(g, slice(c * UNIT_LANES, (c + 1) * UNIT_LANES))
                 for g in range(ngrp) for c in range(rows // UNIT_LANES)]
        def scores(g, ls):
            s = _dot(kgs[g], qs_scr[g, :, ls])
            if diagonal:
                kpos = lax.broadcasted_iota(jnp.int32, s.shape, 0)
                lane = lax.broadcasted_iota(jnp.int32, s.shape, 1)
                s = jnp.where(kpos <= jnp.bitwise_and(lane + ls.start, tq - 1), s, NEG_INF)
            return s

        def probs(g, ls, s):
            m_prev = m_scr[g, :, ls]
            m_cand = jnp.maximum(m_prev, jnp.max(s, axis=0, keepdims=True))
            if diagonal:
                m_new = m_ref = m_cand
            else:
                picked = sel_rows[g][:, ls] > 0.5
                m_new = jnp.where(picked, m_cand, m_prev)
                m_ref = jnp.where(picked, m_cand, POS_INF)
            return jnp.exp2(m_prev - m_new), m_new, jnp.exp2(s - m_ref).astype(BF16)

        def finish(g, ls, alpha, m_new, p):
            pv = _dot(vgs[g], p)
            acc_scr[g, :, ls] = alpha * acc_scr[g, :, ls] + pv[:hd, :]
            l_scr[g, :, ls] = alpha * l_scr[g, :, ls] + pv[hd:hd + 1, :]
            m_scr[g, :, ls] = m_new

        n = len(units)
        s_q, p_q = {}, {}
        for step in range(n + SKEW_S + SKEW_P):
            if step < n:
                s_q[step] = scores(*units[step])
            j = step - SKEW_S
            if 0 <= j < n:
                p_q[j] = probs(*units[j], s_q.pop(j))
            k = j - SKEW_P
            if 0 <= k < n:
                finish(*units[k], *p_q.pop(k))

    @pl.when(kjs[0] == qi)
    def _():
        attend(k_refs[0], vt_refs[0], kjs[0], True)

    for k_ref, vt_ref, kj in zip(k_refs, vt_refs, kjs):
        @pl.when(kj < qi)
        def _(k_ref=k_ref, vt_ref=vt_ref, kj=kj):
            attend(k_ref, vt_ref, kj, False)

    @pl.when(last_ref[p] == 1)
    def _():
        outs = []
        for g in range(ngrp):
            og = acc_scr[g] / l_scr[g]
            outs += [og[:, hh * tq:(hh + 1) * tq] for hh in range(grp)]
        o_ref[...] = jnp.concatenate(outs, axis=0).T


def _moba_prompt(q, k, vt, means, *, nb_batch, t_len, ngrp, grp, hd):
    tq = MOBA_BLOCK
    nq = t_len // tq
    nb = t_len // MOBA_BLOCK
    npast = MOBA_PAST_PER_STEP
    steps = []
    for i in range(nq):
        steps.append([i] + [i] * npast + [int(i == 0)])
        for j in range(0, i, npast):
            steps.append([i] + [min(j + s, i) for s in range(npast)] + [int(j + npast >= i)])
    tabs = [jnp.asarray(np.array([st[c] for st in steps], np.int32)) for c in range(npast + 2)]
    rows = grp * tq
    w = ngrp * grp * hd
    kvw = ngrp * hd
    kern = functools.partial(_moba_prompt_kernel, tq=tq, ngrp=ngrp, grp=grp, hd=hd,
                             scale=float(hd) ** -0.5)
    q_map = lambda b, p, *tb: (b * nq + tb[0][p], 0)
    k_spec = lambda s: pl.BlockSpec((MOBA_BLOCK, kvw), lambda b, p, *tb: (b * nb + tb[1 + s][p], 0))
    vt_spec = lambda s: pl.BlockSpec((None, kvw, MOBA_BLOCK), lambda b, p, *tb: (b, 0, tb[1 + s][p]))
    grid_spec = pltpu.PrefetchScalarGridSpec(
        num_scalar_prefetch=npast + 2,
        grid=(nb_batch, len(steps)),
        in_specs=[pl.BlockSpec((tq, w), q_map)]
        + [k_spec(s) for s in range(npast)] + [vt_spec(s) for s in range(npast)]
        + [pl.BlockSpec((nb, kvw), lambda b, p, *tb: (b, 0))],
        out_specs=pl.BlockSpec((tq, w), q_map),
        scratch_shapes=[pltpu.VMEM((ngrp, hd, rows), BF16),
                        pltpu.VMEM((ngrp, nb, rows), F32),
                        pltpu.VMEM((ngrp, 1, rows), F32),
                        pltpu.VMEM((ngrp, 1, rows), F32),
                        pltpu.VMEM((ngrp, hd, rows), F32)],
    )
    return pl.pallas_call(
        kern,
        grid_spec=grid_spec,
        out_shape=jax.ShapeDtypeStruct(q.shape, F32),
        compiler_params=_cparams(("parallel", "arbitrary")),
        name="moba_prompt",
    )(*tabs, q, *([k] * npast), *([vt] * npast), means)


def _sample_partial_kernel(pt_ref, q_ref, *refs, npg, ngrp, grp, hd, scale):
    k_pages = refs[:npg]
    v_pages = refs[npg:2 * npg]
    mean_ref, m_ref, l_ref, po_ref, qs_scr = refs[2 * npg:]
    step = pl.program_id(1)
    t_new = q_ref.shape[0]
    gr = grp * t_new

    @pl.when(step == 0)
    def _():
        for g in range(ngrp):
            qg = jnp.concatenate(
                [q_ref[:, (g * grp + hh) * hd:(g * grp + hh + 1) * hd] for hh in range(grp)], axis=0)
            qs_scr[g] = (qg * scale).astype(BF16)
        mean_ref[...] = jnp.zeros(mean_ref.shape, F32)
        m_ref[...] = jnp.zeros(m_ref.shape, F32)
        l_ref[...] = jnp.zeros(l_ref.shape, F32)

    ppb = MOBA_BLOCK // k_pages[0].shape[-1]
    bps = npg // ppb
    lane = lax.broadcasted_iota(jnp.int32, m_ref.shape, 1)
    lane_hd = lax.broadcasted_iota(jnp.int32, (hd, LANES), 1)
    blocks = [slice(blk * ppb, (blk + 1) * ppb) for blk in range(bps)]
    kts = [[jnp.concatenate([r[g] for r in k_pages[sl]], axis=1) for g in range(ngrp)]
           for sl in blocks]
    scs = [jnp.concatenate([_dot(qs_scr[g], kt[g].astype(BF16)) for g in range(ngrp)], axis=0)
           for kt in kts]
    ms = [jnp.max(sc, axis=1, keepdims=True) for sc in scs]
    prs = [jnp.exp(sc - m) for sc, m in zip(scs, ms)]
    ls = [jnp.sum(pr, axis=1, keepdims=True) for pr in prs]
    outs = []
    for sl, pr in zip(blocks, prs):
        prb = pr.astype(BF16)
        outs.append(jnp.concatenate(
            [_dot_nt(prb[g * gr:(g + 1) * gr, :],
                     jnp.concatenate([r[g] for r in v_pages[sl]], axis=1).astype(BF16))
             for g in range(ngrp)], axis=0))
    m_tile = m_ref[...]
    l_tile = l_ref[...]
    for blk in range(bps):
        bidx = step * bps + blk
        m_tile = jnp.where(lane == bidx, ms[blk], m_tile)
        l_tile = jnp.where(lane == bidx, ls[blk], l_tile)
        for g in range(ngrp):
            mean_col = jnp.sum(kts[blk][g], axis=1, keepdims=True) * (1.0 / MOBA_BLOCK)
            rs = slice(g * hd, (g + 1) * hd)
            mean_ref[rs, :] = jnp.where(lane_hd == bidx, mean_col, mean_ref[rs, :])
    m_ref[...] = m_tile
    l_ref[...] = l_tile
    for blk in range(bps):
        po_ref[blk] = outs[blk].T


def _sample_partials(q, cache_k, cache_v, page_table, *, npg, ngrp, grp, hd):
    db, n_pages = page_table.shape
    page = cache_k.shape[-1]
    kvw = ngrp * hd
    t_new = q.shape[0] // db
    ppb = MOBA_BLOCK // page
    bps = npg // ppb
    nsteps = n_pages // npg
    nblk = n_pages // ppb
    nrow = ngrp * grp * t_new
    assert nblk <= LANES
    kern = functools.partial(_sample_partial_kernel, npg=npg, ngrp=ngrp, grp=grp, hd=hd,
                             scale=float(hd) ** -0.5)

    def page_spec(i):
        return pl.BlockSpec((None, ngrp, hd, page),
                            lambda b, s, pt: (pt[b * n_pages + s * npg + i], 0, 0, 0))

    grid_spec = pltpu.PrefetchScalarGridSpec(
        num_scalar_prefetch=1,
        grid=(db, nsteps),
        in_specs=[pl.BlockSpec((t_new, q.shape[1]), lambda b, s, pt: (b, 0))]
        + [page_spec(i) for i in range(npg)] + [page_spec(i) for i in range(npg)],
        out_specs=[pl.BlockSpec((None, kvw, LANES), lambda b, s, pt: (b, 0, 0)),
                   pl.BlockSpec((None, nrow, LANES), lambda b, s, pt: (b, 0, 0)),
                   pl.BlockSpec((None, nrow, LANES), lambda b, s, pt: (b, 0, 0)),
                   pl.BlockSpec((None, bps, hd, nrow), lambda b, s, pt: (b, s, 0, 0))],
        scratch_shapes=[pltpu.VMEM((ngrp, grp * t_new, hd), BF16)],
    )
    return pl.pallas_call(
        kern,
        grid_spec=grid_spec,
        out_shape=[jax.ShapeDtypeStruct((db, kvw, LANES), F32),
                   jax.ShapeDtypeStruct((db, nrow, LANES), F32),
                   jax.ShapeDtypeStruct((db, nrow, LANES), F32),
                   jax.ShapeDtypeStruct((db, nblk, hd, nrow), F32)],
        compiler_params=_cparams(("parallel", "arbitrary")),
        name="moba_sample_partials",
    )(page_table.reshape(-1), q, *([cache_k] * npg), *([cache_v] * npg))


def _sample_combine_kernel(q_ref, kn_ref, vn_ref, mean_ref, m_ref, l_ref, po_ref, o_ref,
                           *, nblk, ngrp, grp, hd, scale):
    t_new = q_ref.shape[0]
    gr = grp * t_new
    nrow = ngrp * gr
    gates, qss = [], []
    for g in range(ngrp):
        qg = jnp.concatenate(
            [q_ref[:, (g * grp + hh) * hd:(g * grp + hh + 1) * hd] for hh in range(grp)], axis=0)
        qss.append((qg * scale).astype(BF16))
        gates.append(_dot(qg, mean_ref[g * hd:(g + 1) * hd, :], HI))
    sel = _top_select(jnp.concatenate(gates, axis=0), nblk, 1) > 0.5
    m_all = m_ref[...]
    m_past = jnp.max(jnp.where(sel, m_all, NEG_INF), axis=1, keepdims=True)
    kpos = lax.broadcasted_iota(jnp.int32, (nrow, t_new), 1)
    qpos = jnp.bitwise_and(lax.broadcasted_iota(jnp.int32, (nrow, t_new), 0), t_new - 1)
    kn = kn_ref[...].astype(BF16)
    vn = vn_ref[...].astype(BF16)
    s_own = jnp.concatenate(
        [_dot_nt(qss[g], kn[:, g * hd:(g + 1) * hd]) for g in range(ngrp)], axis=0)
    s_own = jnp.where(kpos <= qpos, s_own, NEG_INF)
    m_fin = jnp.maximum(m_past, jnp.max(s_own, axis=1, keepdims=True))
    w = jnp.where(sel, jnp.exp(m_all - m_fin), 0.0)
    p_own = jnp.exp(s_own - m_fin)
    l_fin = (jnp.sum(w * l_ref[...], axis=1, keepdims=True)
             + jnp.sum(p_own, axis=1, keepdims=True))
    pb = p_own.astype(BF16)
    o_own = jnp.concatenate(
        [_dot(pb[g * gr:(g + 1) * gr, :], vn[:, g * hd:(g + 1) * hd]) for g in range(ngrp)], axis=0)
    w_t = w.T
    acc_t = jnp.zeros((hd, nrow), F32)
    for j in range(nblk):
        acc_t = acc_t + w_t[j:j + 1, :] * po_ref[j]
    o = (acc_t.T + o_own) / l_fin
    o_ref[...] = jnp.concatenate(
        [o[r * t_new:(r + 1) * t_new, :] for r in range(ngrp * grp)], axis=1)


def _sample_combine(q, k_new, v_new, means_t, m_part, l_part, po, *, db, nblk, ngrp, grp, hd):
    t_new = q.shape[0] // db
    nrow = ngrp * grp * t_new
    kvw = ngrp * hd
    kern = functools.partial(_sample_combine_kernel, nblk=nblk, ngrp=ngrp, grp=grp, hd=hd,
                             scale=float(hd) ** -0.5)
    return pl.pallas_call(
        kern,
        grid=(db,),
        in_specs=[pl.BlockSpec((t_new, q.shape[1]), lambda b: (b, 0)),
                  pl.BlockSpec((t_new, kvw), lambda b: (b, 0)),
                  pl.BlockSpec((t_new, kvw), lambda b: (b, 0)),
                  pl.BlockSpec((None, kvw, LANES), lambda b: (b, 0, 0)),
                  pl.BlockSpec((None, nrow, LANES), lambda b: (b, 0, 0)),
                  pl.BlockSpec((None, nrow, LANES), lambda b: (b, 0, 0)),
                  pl.BlockSpec((None,) + po.shape[1:], lambda b: (b, 0, 0, 0))],
        out_specs=pl.BlockSpec((t_new, q.shape[1]), lambda b: (b, 0)),
        out_shape=jax.ShapeDtypeStruct(q.shape, F32),
        compiler_params=_cparams(("parallel",)),
        name="moba_sample_combine",
    )(q, k_new, v_new, means_t, m_part, l_part, po)


def _rope_tables(pos, hd, reps):
    half = hd // 2
    inv_freq = ROPE_THETA ** (-jnp.arange(half, dtype=F32) / half)
    ang = pos.astype(F32)[:, None] * inv_freq[None, :]
    cos = jnp.cos(ang)
    sin = jnp.sin(ang)
    per_vreg = LANES // hd
    cos_t = jnp.tile(jnp.concatenate([cos, cos], axis=1), (reps, per_vreg))
    sin_t = jnp.tile(jnp.concatenate([-sin, sin], axis=1), (reps, per_vreg))
    return cos_t, sin_t


def _trunk(x, pos, conv_state8, delta_state, attend, wts, *, nseq, t_len, tm, prep_tt,
           delta_ct, delta_c, dims, want_t=False):
    nh, dk, conv_dim, n_a, depth, alpha, bh, kvh, hd = dims
    rows = nseq * t_len
    new_conv, new_delta = [], []
    k_sh = v_sh = t_sh = None
    for layer in range(depth):
        if layer < n_a:
            cw = wts["a_conv"][layer].shape[0]
            prep_args = (conv_state8[layer], wts["a_conv"][layer], wts["a_log"][layer],
                         wts["a_dtb"][layer])
            prep_kw = dict(nseq=nseq, t_len=t_len, tt=prep_tt, conv_dim=conv_dim, nh=nh, dk=dk)
            if t_len >= tm:
                qkv, gb, z_src, tail = _gdn_in_fused(x, wts["a_w_in"][layer], *prep_args, **prep_kw)
                z_blk = 0
                new_conv.append(tail[:, SUBLANES - (cw - 1):, :])
            else:
                proj = _matmul(x, wts["a_w_in"][layer], rows, wts["a_w_in"][layer].shape[1] // 3,
                               "in_proj")
                qkv, gb = _gdn_prep(proj, *prep_args, **prep_kw)
                z_src, z_blk = proj, conv_dim // (nh * dk)
                new_conv.append(proj.reshape(nseq, t_len, -1)[:, t_len - (cw - 1):, :conv_dim])
            o, s_new = _delta_rule(qkv, gb, delta_state[layer], nseq=nseq, t_len=t_len,
                                   ct=delta_ct, c=delta_c, nh=nh, dk=dk)
            new_delta.append(s_new)
            mix, w_mix = o, wts["a_w_out"][layer]
            mix_kw = dict(z_src=z_src, z_blk=z_blk, norm_w=wts["a_norm"][layer], nh=nh, dv=dk)
        else:
            j = layer - n_a
            w_all = wts["b_w_qkv"][j]
            cos_t, sin_t = _rope_tables(pos, hd, (tm // t_len) if tm > t_len else 1)
            q, k_l, v_l, *t_l = _qkv_rope(x, w_all, cos_t, sin_t, tm=tm, nq=bh * hd, nk=kvh * hd,
                                          half=hd // 2, t_seq=t_len if want_t else None)
            if layer == n_a:
                k_sh, v_sh, t_sh = k_l, v_l, t_l
            mix, w_mix, mix_kw = attend(q, k_sh, v_sh, *t_sh), wts["b_w_o"][j], {}
        ln1 = (wts["ln_g"][layer][0], wts["ln_b"][layer][0])
        ln2 = (wts["ln_g"][layer][1], wts["ln_b"][layer][1])
        x = _mixer_ffn(mix, x, w_mix, ln1, wts["ffn_wg"][layer], wts["ffn_wu"][layer],
                       wts["ffn_wd"][layer], ln2, tm=tm, tf=256, alpha=alpha, **mix_kw)
    return x, jnp.stack(new_conv), jnp.stack(new_delta), k_sh, v_sh, t_sh


def kernel(x_prompt, x_sample, state_conv, state_delta, cache_k, cache_v, page_table, ln_g, ln_b,
           a_w_in, a_conv, a_log_decay, a_dt_bias, a_norm, a_w_out, kv_w_k, kv_w_v, b_w_q, b_w_o,
           ffn_w_gate, ffn_w_up, ffn_w_down):
    bp, tp, d = x_prompt.shape
    db, ts, _ = x_sample.shape
    depth = ln_g.shape[0]
    n_a = a_w_in.shape[0]
    nh = a_log_decay.shape[1]
    dk = state_delta.shape[-2]
    conv_dim = a_conv.shape[-1]
    cw = a_conv.shape[1]
    n_pool, page, kvh, hd = cache_k.shape
    bh = b_w_q.shape[-1] // hd
    grp = bh // kvh
    n_pages = page_table.shape[1]
    past = n_pages * page
    alpha = (2.0 * depth) ** 0.25
    assert past % MOBA_BLOCK == 0 and ts <= MOBA_BLOCK and ts >= cw - 1 and tp >= cw - 1
    assert state_delta.shape[-1] == dk and conv_dim == 3 * nh * dk

    in_w = a_w_in.shape[-1]
    in_pad = -(-in_w // (3 * LANES)) * (3 * LANES)
    pad_lanes = lambda v: jnp.pad(v, ((0, 0), (0, LANES - v.shape[-1])))[:, None, :]
    wts = {
        "a_w_in": [jnp.pad(a_w_in[l].astype(BF16), ((0, 0), (0, in_pad - in_w))) for l in range(n_a)],
        "a_conv": a_conv,
        "a_log": pad_lanes(a_log_decay),
        "a_dtb": pad_lanes(a_dt_bias),
        "a_norm": a_norm[:, None, :],
        "a_w_out": a_w_out.astype(BF16),
        "b_w_qkv": jnp.concatenate(
            [b_w_q, jnp.broadcast_to(kv_w_k, (b_w_q.shape[0],) + kv_w_k.shape),
             jnp.broadcast_to(kv_w_v, (b_w_q.shape[0],) + kv_w_v.shape)], axis=-1).astype(BF16),
        "b_w_o": b_w_o.astype(BF16),
        "ffn_wg": ffn_w_gate.astype(BF16),
        "ffn_wu": ffn_w_up.astype(BF16),
        "ffn_wd": ffn_w_down.astype(BF16),
        "ln_g": ln_g[:, :, None, :],
        "ln_b": ln_b[:, :, None, :],
    }
    dims = (nh, dk, conv_dim, n_a, depth, alpha, bh, kvh, hd)

    nb = tp // MOBA_BLOCK

    def attend_prompt(q, k, v, kt, vt):
        total = bp * nb
        step = SUBLANES if total % SUBLANES == 0 else total
        means = _block_means(k, nblk_step=step)
        return _moba_prompt(q, k, vt, means, nb_batch=bp, t_len=tp, ngrp=kvh, grp=grp, hd=hd)

    tm_p = 512 if (bp * tp) % 512 == 0 else 256
    y_p, p_conv, p_delta, _, _, (p_kt, p_vt) = _trunk(
        x_prompt.reshape(bp * tp, d), jnp.arange(tp, dtype=jnp.int32),
        jnp.zeros((n_a, bp, SUBLANES, conv_dim), F32), jnp.zeros((n_a, bp, nh, dk, dk), F32),
        attend_prompt, wts, nseq=bp, t_len=tp, tm=tm_p, prep_tt=tm_p,
        delta_ct=DELTA_CHUNK * DELTA_CHUNKS_PER_ITER,
        delta_c=DELTA_CHUNK, dims=dims, want_t=True)
    p_k = jnp.transpose(p_kt.reshape(bp, kvh, hd, tp), (0, 3, 1, 2))
    p_v = jnp.transpose(p_vt.reshape(bp, kvh, hd, tp), (0, 3, 1, 2))

    ck = jnp.transpose(cache_k, (0, 2, 3, 1))
    cv = jnp.transpose(cache_v, (0, 2, 3, 1))
    npg = next(n for n in (16, 8, 2 * (MOBA_BLOCK // page)) if n_pages % n == 0)

    def attend_sample(q, k_new, v_new):
        means_t, m_part, l_part, po = _sample_partials(q, ck, cv, page_table, npg=npg, ngrp=kvh,
                                                       grp=grp, hd=hd)
        return _sample_combine(q, k_new, v_new, means_t, m_part, l_part, po, db=db,
                               nblk=past // MOBA_BLOCK, ngrp=kvh, grp=grp, hd=hd)

    conv8 = jnp.pad(state_conv, ((0, 0), (0, 0), (SUBLANES - (cw - 1), 0), (0, 0)))
    y_s, s_conv, s_delta, s_k, s_v, _ = _trunk(
        x_sample.reshape(db * ts, d), past + jnp.arange(ts, dtype=jnp.int32), conv8, state_delta,
        attend_sample, wts, nseq=db, t_len=ts, tm=db * ts, prep_tt=ts, delta_ct=ts, delta_c=ts,
        dims=dims)

    return (y_p.reshape(bp, tp, d), y_s.reshape(db, ts, d), p_conv, p_delta, p_k, p_v,
            s_conv, s_delta, s_k.reshape(db, ts, kvh, hd), s_v.reshape(db, ts, kvh, hd))
```

```python
import functools

import numpy as np
import jax
import jax.numpy as jnp
from jax import lax
from jax.experimental import pallas as pl
from jax.experimental.pallas import tpu as pltpu

F32 = jnp.float32
BF16 = jnp.bfloat16
HI = lax.Precision.HIGHEST

LANES = 128
SUBLANES = 8
VMEM_LIMIT_MB = 56

MOBA_BLOCK = 256
MOBA_TOPK = 3
MOBA_PAST_PER_STEP = 2
UNIT_LANES = 256
SKEW_S = 3
SKEW_P = 2
DELTA_CHUNK = 64
DELTA_SOLVE_GROUP = 4
DELTA_CHUNKS_PER_ITER = 8
ROPE_THETA = 10000.0
LN_EPS = 1e-5
RMS_EPS = 1e-6
L2_EPS = 1e-6
NEG_INF = float("-inf")
POS_INF = float("inf")
LOG2E = 1.4426950408889634


def _cparams(sem):
    return pltpu.CompilerParams(dimension_semantics=sem,
                                vmem_limit_bytes=VMEM_LIMIT_MB * 1024 * 1024)


def _dot(a, b, prec=None):
    return jnp.dot(a, b, precision=prec, preferred_element_type=F32)


def _dot_nt(a, b, prec=None):
    return lax.dot_general(a, b, (((1,), (1,)), ((), ())), precision=prec,
                           preferred_element_type=F32)


def _dot_tn(a, b, prec=None):
    return lax.dot_general(a, b, (((0,), (0,)), ((), ())), precision=prec,
                           preferred_element_type=F32)


def _split(a):
    hi = a.astype(BF16)
    return hi, (a - hi.astype(F32)).astype(BF16)


def _dot3(a, b):
    return _dot(a[0], b[0]) + (_dot(a[0], b[1]) + _dot(a[1], b[0]))


def _sigmoid(x):
    return 1.0 / (1.0 + jnp.exp(-x))


def _layer_norm(v, g, b):
    mu = jnp.mean(v, axis=-1, keepdims=True)
    d = v - mu
    var = jnp.mean(d * d, axis=-1, keepdims=True)
    return d * lax.rsqrt(var + LN_EPS) * g + b


def _mm_kernel(x_ref, w_ref, o_ref):
    o_ref[...] = _dot(x_ref[...].astype(BF16), w_ref[...])


def _matmul(x, w, tm, tn, name):
    r, k = x.shape
    n = w.shape[1]
    return pl.pallas_call(
        _mm_kernel,
        grid=(r // tm, n // tn),
        in_specs=[pl.BlockSpec((tm, k), lambda i, j: (i, 0)),
                  pl.BlockSpec((k, tn), lambda i, j: (0, j))],
        out_specs=pl.BlockSpec((tm, tn), lambda i, j: (i, j)),
        out_shape=jax.ShapeDtypeStruct((r, n), F32),
        compiler_params=_cparams(("parallel", "parallel")),
        name=name,
    )(x, w)


def _conv_norm(xp_ref, cw_ref, qkv_ref, cols, *, tt, conv_w, n_q, n_qk, q_scale):
    hist = SUBLANES
    for c in cols:
        cs = slice(c * LANES, (c + 1) * LANES)
        full = xp_ref[0:hist + tt, cs]
        y = pltpu.roll(full, conv_w - 1, 0)[hist:] * cw_ref[0:1, cs]
        for i in range(1, conv_w - 1):
            y = y + pltpu.roll(full, conv_w - 1 - i, 0)[hist:] * cw_ref[i:i + 1, cs]
        y = y + full[hist:] * cw_ref[conv_w - 1:conv_w, cs]
        y = y * _sigmoid(y)
        if c < n_qk:
            r = lax.rsqrt(jnp.sum(y * y, axis=-1, keepdims=True) + L2_EPS)
            y = y * (r * q_scale if c < n_q else r)
        qkv_ref[:, cs] = y


def _decay_beta(ab, alog, dtb, nh):
    lane = lax.broadcasted_iota(jnp.int32, ab.shape, 1)
    sp = ab + dtb
    softplus = jnp.maximum(sp, 0.0) + jnp.log1p(jnp.exp(-jnp.abs(sp)))
    return jnp.where(lane < nh, -jnp.exp(alog) * softplus, _sigmoid(ab))


def _prep_kernel(cur_ref, prev_ref, st_ref, ab_ref, cw_ref, alog_ref, dtb_ref,
                 qkv_ref, gb_ref, xp_ref, *, tt, conv_w, n_q, n_qk, nh, q_scale):
    t = pl.program_id(1)
    hist = SUBLANES

    @pl.when(t == 0)
    def _():
        xp_ref[0:hist, :] = st_ref[...]

    @pl.when(t != 0)
    def _():
        xp_ref[0:hist, :] = prev_ref[...]

    xp_ref[hist:hist + tt, :] = cur_ref[...]
    _conv_norm(xp_ref, cw_ref, qkv_ref, range(cur_ref.shape[1] // LANES), tt=tt, conv_w=conv_w,
               n_q=n_q, n_qk=n_qk, q_scale=q_scale)
    gb_ref[...] = _decay_beta(ab_ref[...], alog_ref[...], dtb_ref[...], nh)


def _gdn_in_kernel(x_ref, w_ref, st_ref, cw_ref, alog_ref, dtb_ref, qkv_ref, gb_ref, z_ref, tail_ref,
                   xp_ref, *, tt, conv_dim, zw, conv_w, n_q, n_qk, nh, q_scale):
    hist = SUBLANES

    @pl.when(pl.program_id(1) == 0)
    def _():
        xp_ref[0:hist, :] = st_ref[...]

    xb = x_ref[...].astype(BF16)
    mxu_w = 2 * LANES
    for c in range(conv_dim // mxu_w):
        cs = slice(c * mxu_w, (c + 1) * mxu_w)
        xp_ref[hist:hist + tt, cs] = _dot(xb, w_ref[:, cs])
        _conv_norm(xp_ref, cw_ref, qkv_ref, range(2 * c, 2 * c + 2), tt=tt, conv_w=conv_w,
                   n_q=n_q, n_qk=n_qk, q_scale=q_scale)
    z_ref[...] = _dot(xb, w_ref[:, conv_dim:conv_dim + zw])
    ab = _dot(xb, w_ref[:, conv_dim + zw:conv_dim + zw + LANES])
    gb_ref[...] = _decay_beta(ab, alog_ref[...], dtb_ref[...], nh)
    tail = xp_ref[tt:tt + hist, :]
    tail_ref[...] = tail
    xp_ref[0:hist, :] = tail


def _gdn_in_fused(x, w, state8, conv_w, alog, dtb, *, nseq, t_len, tt, conv_dim, nh, dk):
    rows, d = x.shape
    nt = t_len // tt
    cw = conv_w.shape[0]
    zw = nh * dk
    kern = functools.partial(_gdn_in_kernel, tt=tt, conv_dim=conv_dim, zw=zw, conv_w=cw, n_q=nh,
                             n_qk=2 * nh, nh=nh, q_scale=float(dk) ** -0.5)
    row = lambda width: pl.BlockSpec((tt, width), lambda s, t: (s * nt + t, 0))
    const = lambda shape: pl.BlockSpec(shape, lambda s, t: (0, 0))
    per_seq = pl.BlockSpec((None, SUBLANES, conv_dim), lambda s, t: (s, 0, 0))
    return pl.pallas_call(
        kern,
        grid=(nseq, nt),
        in_specs=[row(d),
                  pl.BlockSpec(w.shape, lambda s, t: (0, 0), pipeline_mode=pl.Buffered(1)),
                  per_seq, const((cw, conv_dim)), const((1, LANES)), const((1, LANES))],
        out_specs=[row(conv_dim), row(LANES), row(zw), per_seq],
        out_shape=[jax.ShapeDtypeStruct((rows, conv_dim), F32),
                   jax.ShapeDtypeStruct((rows, LANES), F32),
                   jax.ShapeDtypeStruct((rows, zw), F32),
                   jax.ShapeDtypeStruct((nseq, SUBLANES, conv_dim), F32)],
        scratch_shapes=[pltpu.VMEM((tt + SUBLANES, conv_dim), F32)],
        compiler_params=_cparams(("parallel", "arbitrary")),
        name="gdn_in_fused",
    )(x, w, state8, conv_w, alog, dtb)


def _gdn_prep(proj, state8, conv_w, alog, dtb, *, nseq, t_len, tt, conv_dim, nh, dk):
    rows = nseq * t_len
    nt = t_len // tt
    cw = conv_w.shape[0]
    ab_blk = (conv_dim + nh * dk) // LANES
    kern = functools.partial(_prep_kernel, tt=tt, conv_w=cw, n_q=nh, n_qk=2 * nh, nh=nh,
                             q_scale=float(dk) ** -0.5)
    tpb = tt // SUBLANES
    return pl.pallas_call(
        kern,
        grid=(nseq, nt),
        in_specs=[
            pl.BlockSpec((tt, conv_dim), lambda s, t: (s * nt + t, 0)),
            pl.BlockSpec((SUBLANES, conv_dim),
                         lambda s, t: (jnp.maximum((s * nt + t) * tpb - 1, 0), 0)),
            pl.BlockSpec((None, SUBLANES, conv_dim), lambda s, t: (s, 0, 0)),
            pl.BlockSpec((tt, LANES), lambda s, t: (s * nt + t, ab_blk)),
            pl.BlockSpec((cw, conv_dim), lambda s, t: (0, 0)),
            pl.BlockSpec((1, LANES), lambda s, t: (0, 0)),
            pl.BlockSpec((1, LANES), lambda s, t: (0, 0)),
        ],
        out_specs=[pl.BlockSpec((tt, conv_dim), lambda s, t: (s * nt + t, 0)),
                   pl.BlockSpec((tt, LANES), lambda s, t: (s * nt + t, 0))],
        out_shape=[jax.ShapeDtypeStruct((rows, conv_dim), F32),
                   jax.ShapeDtypeStruct((rows, LANES), F32)],
        scratch_shapes=[pltpu.VMEM((tt + SUBLANES, conv_dim), F32)],
        compiler_params=_cparams(("parallel", "arbitrary")),
        name="gdn_prep",
    )(proj, proj, state8, proj, conv_w, alog, dtb)


def _delta_kernel(q_ref, k_ref, v_ref, gb_ref, s0_ref, o_ref, s_ref, st_scr, *, sb, ct, c, nh, dk):
    t = pl.program_id(1)

    @pl.when(t == 0)
    def _():
        st_scr[...] = s0_ref[...]

    row = lax.broadcasted_iota(jnp.int32, (c, c), 0)
    col = lax.broadcasted_iota(jnp.int32, (c, c), 1)
    causal = row >= col
    strict = row > col
    tril = causal.astype(F32)
    eye = (row == col).astype(F32)
    n_sq = max(int(np.ceil(np.log2(c))) - 1, 0)

    def prepare(rs, seq):
        gbc = gb_ref[rs, :]
        gc = _dot(tril, gbc, HI)
        gt = gc.T
        eg = jnp.exp(gc)
        units = []
        for h in range(nh):
            cs = slice(h * dk, (h + 1) * dk)
            qh = q_ref[rs, cs]
            kh = k_ref[rs, cs]
            vh = v_ref[rs, cs]
            gcol = gc[:, h:h + 1]
            grow = gt[h:h + 1, :]
            bcol = gbc[:, nh + h:nh + h + 1]
            egcol = eg[:, h:h + 1]
            decay = jnp.where(causal, jnp.exp(jnp.where(causal, gcol - grow, 0.0)), 0.0)
            kb = kh.astype(BF16)
            kk = _dot_nt(kb, kb)
            qk = _dot_nt(qh.astype(BF16), kb)
            glast = grow[:, c - 1:c]
            units.append(dict(
                rs=rs, h=h, seq=seq,
                x=-jnp.where(strict, bcol * kk * decay, 0.0),
                qkm=jnp.where(causal, qk * decay, 0.0).astype(BF16),
                rhs=jnp.concatenate([vh * bcol, kh * (bcol * egcol)], axis=-1),
                qd=(qh * egcol).astype(BF16),
                kd=(kh * jnp.exp(glast - gcol)).astype(BF16),
                glast=jnp.exp(glast)))
        return units

    def solve(units):
        ps = [u["x"] for u in units]
        tms = [eye + p for p in ps]
        for _ in range(n_sq):
            sp = [_split(p) for p in ps]
            ps = [_dot3(s, s) for s in sp]
            sp = [_split(p) for p in ps]
            tms = [tm + _dot3(_split(tm), s) for tm, s in zip(tms, sp)]
        return [_dot3(_split(tm), _split(u["rhs"])) for tm, u in zip(tms, units)]

    def update(u, sol):
        h, seq = u["h"], u["seq"]
        s = st_scr[seq, h]
        s_b = s.astype(BF16)
        v_new = sol[:, :dk] - _dot(sol[:, dk:].astype(BF16), s_b)
        vb = v_new.astype(BF16)
        o_ref[u["rs"], h * dk:(h + 1) * dk] = _dot(u["qd"], s_b) + _dot(u["qkm"], vb)
        st_scr[seq, h] = s * u["glast"] + _dot_tn(u["kd"], vb)

    def chunks(r0, n):
        groups = []
        for seq in range(sb):
            for i0 in range(0, n, DELTA_SOLVE_GROUP):
                units = []
                for i in range(i0, min(i0 + DELTA_SOLVE_GROUP, n)):
                    units += prepare(pl.ds(seq * ct + r0 + i * c, c), seq)
                groups.append(units)
        if sb > 1:
            groups = [sum(groups, [])]
        solved = None
        for units in groups:
            sols = solve(units)
            if solved is not None:
                for u, sol in zip(*solved):
                    update(u, sol)
            solved = (units, sols)
        for u, sol in zip(*solved):
            update(u, sol)

    n_chunks = ct // c
    per_iter = DELTA_CHUNKS_PER_ITER if n_chunks % DELTA_CHUNKS_PER_ITER == 0 else 1
    if n_chunks == per_iter:
        chunks(0, per_iter)
    else:
        def body(ci, carry):
            chunks(pl.multiple_of(ci * (per_iter * c), per_iter * c), per_iter)
            return carry
        lax.fori_loop(0, n_chunks // per_iter, body, 0)

    @pl.when(t == pl.num_programs(1) - 1)
    def _():
        s_ref[...] = st_scr[...]


def _delta_rule(qkv, gb, s0, *, nseq, t_len, ct, c, nh, dk):
    rows = nseq * t_len
    nt = t_len // ct
    w = nh * dk
    sb = DELTA_SOLVE_GROUP if (nt == 1 and ct == c and nseq % DELTA_SOLVE_GROUP == 0) else 1
    kern = functools.partial(_delta_kernel, sb=sb, ct=ct, c=c, nh=nh, dk=dk)
    return pl.pallas_call(
        kern,
        grid=(nseq // sb, nt),
        in_specs=[
            pl.BlockSpec((sb * ct, w), lambda s, t: (s * nt + t, 0)),
            pl.BlockSpec((sb * ct, w), lambda s, t: (s * nt + t, 1)),
            pl.BlockSpec((sb * ct, w), lambda s, t: (s * nt + t, 2)),
            pl.BlockSpec((sb * ct, LANES), lambda s, t: (s * nt + t, 0)),
            pl.BlockSpec((sb, nh, dk, dk), lambda s, t: (s, 0, 0, 0)),
        ],
        out_specs=[pl.BlockSpec((sb * ct, w), lambda s, t: (s * nt + t, 0)),
                   pl.BlockSpec((sb, nh, dk, dk), lambda s, t: (s, 0, 0, 0))],
        out_shape=[jax.ShapeDtypeStruct((rows, w), F32),
                   jax.ShapeDtypeStruct((nseq, nh, dk, dk), F32)],
        scratch_shapes=[pltpu.VMEM((sb, nh, dk, dk), F32)],
        compiler_params=_cparams(("parallel", "arbitrary")),
        name="delta_rule",
    )(qkv, qkv, qkv, gb, s0)


def _mixer_ffn_kernel(*refs, gated, nh, dv, tf, alpha):
    if gated:
        o_ref, z_ref, nw_ref, *refs = refs
    else:
        o_ref, *refs = refs
    (x_ref, wo_ref, g1_ref, b1_ref, wg_ref, wu_ref, wd_ref, g2_ref, b2_ref,
     out_ref, a_scr, h_scr, acc_scr) = refs
    step = pl.program_id(0)
    n_tiles = pl.num_programs(0) - 1
    cur = lax.rem(step, 2)
    prev = 1 - cur

    def mix():
        if gated:
            for h in range(nh):
                cs = slice(h * dv, (h + 1) * dv)
                oh = o_ref[:, cs]
                zh = z_ref[:, cs]
                oh = oh * lax.rsqrt(jnp.mean(oh * oh, axis=-1, keepdims=True) + RMS_EPS) * nw_ref[...]
                a_scr[:, cs] = (oh * (zh * _sigmoid(zh))).astype(BF16)
        else:
            a_scr[...] = o_ref[...].astype(BF16)
        h_scr[cur] = _layer_norm(alpha * x_ref[...] + _dot(a_scr[...], wo_ref[...]),
                                 g1_ref[...], b1_ref[...])

    def ffn(with_mix):
        hb = h_scr[prev].astype(BF16)
        for c in range(wg_ref.shape[1] // tf):
            fs = slice(c * tf, (c + 1) * tf)
            gate = _dot(hb, wg_ref[:, fs])
            up = _dot(hb, wu_ref[:, fs])
            act = (gate * _sigmoid(gate) * up).astype(BF16)
            down = _dot(act, wd_ref[fs, :])
            if c == 0:
                acc_scr[...] = down
            else:
                acc_scr[...] += down
            if with_mix and c == 0:
                mix()
        out_ref[...] = _layer_norm(alpha * h_scr[prev] + acc_scr[...], g2_ref[...], b2_ref[...])

    @pl.when(step == 0)
    def _():
        mix()

    @pl.when(jnp.logical_and(step > 0, step < n_tiles))
    def _():
        ffn(True)

    @pl.when(step == n_tiles)
    def _():
        ffn(False)


def _mixer_ffn(o, x, wo, ln1, wg, wu, wd, ln2, *, tm, tf, alpha, z_src=None, z_blk=0,
               norm_w=None, nh=1, dv=1):
    rows, d_in = o.shape
    d = x.shape[1]
    f = wg.shape[1]
    gated = z_src is not None
    kern = functools.partial(_mixer_ffn_kernel, gated=gated, nh=nh, dv=dv, tf=tf, alpha=alpha)
    n_tiles = rows // tm
    in_row = lambda width, col=0: pl.BlockSpec(
        (tm, width), lambda s: (jnp.minimum(s, n_tiles - 1), col))
    const = lambda shape: pl.BlockSpec(shape, lambda s: (0, 0))
    resident = lambda shape: pl.BlockSpec(shape, lambda s: (0, 0), pipeline_mode=pl.Buffered(1))
    in_specs = [in_row(d_in)]
    args = [o]
    if gated:
        in_specs += [in_row(d_in, z_blk), const((1, dv))]
        args += [z_src, norm_w]
    in_specs += [in_row(d), resident((d_in, d)), const((1, d)), const((1, d)),
                 resident((d, f)), resident((d, f)), resident((f, d)), const((1, d)), const((1, d))]
    args += [x, wo, ln1[0], ln1[1], wg, wu, wd, ln2[0], ln2[1]]
    return pl.pallas_call(
        kern,
        grid=(n_tiles + 1,),
        in_specs=in_specs,
        out_specs=pl.BlockSpec((tm, d), lambda s: (jnp.maximum(s - 1, 0), 0)),
        out_shape=jax.ShapeDtypeStruct((rows, d), F32),
        scratch_shapes=[pltpu.VMEM((tm, d_in), BF16), pltpu.VMEM((2, tm, d), F32),
                        pltpu.VMEM((tm, d), F32)],
        compiler_params=_cparams(("arbitrary",)),
        name="mixer_ffn_ln",
    )(*args)


def _qkv_kernel(x_ref, w_ref, cos_ref, sin_ref, q_ref, k_ref, v_ref, *t_refs, nq, nk, half):
    y = _dot(x_ref[...].astype(BF16), w_ref[...])
    cos = cos_ref[...]
    sin = sin_ref[...]
    lane = lax.broadcasted_iota(jnp.int32, cos.shape, 1)
    first = jnp.bitwise_and(lane, 2 * half - 1) < half

    def rope(xg):
        partner = jnp.where(first, pltpu.roll(xg, LANES - half, 1), pltpu.roll(xg, half, 1))
        return xg * cos + partner * sin

    for c in range(nq // LANES):
        q_ref[:, c * LANES:(c + 1) * LANES] = rope(y[:, c * LANES:(c + 1) * LANES])
    k = jnp.concatenate([rope(y[:, nq + c * LANES:nq + (c + 1) * LANES])
                         for c in range(nk // LANES)], axis=1)
    k_ref[...] = k
    v_ref[...] = y[:, nq + nk:]
    if t_refs:
        kt_ref, vt_ref = t_refs
        kt_ref[...] = k.T
        vt_ref[...] = y[:, nq + nk:].T


def _qkv_rope(x, w, cos, sin, *, tm, nq, nk, half, t_seq=None):
    rows, d = x.shape
    n_tab = cos.shape[0] // tm
    kern = functools.partial(_qkv_kernel, nq=nq, nk=nk, half=half)
    out_specs = [pl.BlockSpec((tm, nq), lambda i: (i, 0)),
                 pl.BlockSpec((tm, nk), lambda i: (i, 0)),
                 pl.BlockSpec((tm, nk), lambda i: (i, 0))]
    out_shape = [jax.ShapeDtypeStruct((rows, nq), F32),
                 jax.ShapeDtypeStruct((rows, nk), F32),
                 jax.ShapeDtypeStruct((rows, nk), F32)]
    if t_seq is not None:
        nt = t_seq // tm
        out_specs += [pl.BlockSpec((None, nk, tm), lambda i: (i // nt, 0, i % nt))] * 2
        out_shape += [jax.ShapeDtypeStruct((rows // t_seq, nk, t_seq), F32)] * 2
    return pl.pallas_call(
        kern,
        grid=(rows // tm,),
        in_specs=[pl.BlockSpec((tm, d), lambda i: (i, 0)),
                  pl.BlockSpec((d, nq + 2 * nk), lambda i: (0, 0)),
                  pl.BlockSpec((tm, LANES), lambda i: (i % n_tab, 0)),
                  pl.BlockSpec((tm, LANES), lambda i: (i % n_tab, 0))],
        out_specs=out_specs,
        out_shape=out_shape,
        compiler_params=_cparams(("parallel",)),
        name="qkv_rope",
    )(x, w, cos, sin)


def _kmeans_kernel(k_ref, m_ref, *, nblk):
    kb = k_ref[...].reshape(nblk, MOBA_BLOCK, k_ref.shape[1])
    m_ref[...] = jnp.sum(kb, axis=1) * (1.0 / MOBA_BLOCK)


def _block_means(k, *, nblk_step):
    rows, w = k.shape
    total = rows // MOBA_BLOCK
    return pl.pallas_call(
        functools.partial(_kmeans_kernel, nblk=nblk_step),
        grid=(total // nblk_step,),
        in_specs=[pl.BlockSpec((nblk_step * MOBA_BLOCK, w), lambda i: (i, 0))],
        out_specs=pl.BlockSpec((nblk_step, w), lambda i: (i, 0)),
        out_shape=jax.ShapeDtypeStruct((total, w), F32),
        compiler_params=_cparams(("parallel",)),
        name="block_means",
    )(k)


def _top_select(gate, n_past, axis):
    n = gate.shape[axis]
    idx = lax.broadcasted_iota(jnp.int32, gate.shape, axis).astype(F32)
    gm = jnp.where(idx < n_past, gate, NEG_INF)
    sel = jnp.zeros(gate.shape, F32)
    for r in range(MOBA_TOPK):
        mx = jnp.max(gm, axis=axis, keepdims=True)
        first = jnp.min(jnp.where(gm == mx, idx, float(n)), axis=axis, keepdims=True)
        pick = idx == first
        keep = jnp.where(r < n_past, 1.0, 0.0).astype(F32)
        sel = jnp.maximum(sel, jnp.where(pick, keep, 0.0))
        gm = jnp.where(pick, NEG_INF, gm)
    return sel


def _moba_prompt_kernel(*refs, tq, ngrp, grp, hd, scale):
    npast = MOBA_PAST_PER_STEP
    qi_ref, kj_refs, last_ref = refs[0], refs[1:1 + npast], refs[1 + npast]
    q_ref = refs[2 + npast]
    k_refs = refs[3 + npast:3 + 2 * npast]
    vt_refs = refs[3 + 2 * npast:3 + 3 * npast]
    mean_ref, o_ref, qs_scr, sel_scr, m_scr, l_scr, acc_scr = refs[3 + 3 * npast:]
    p = pl.program_id(1)
    qi = qi_ref[p]
    kjs = [r[p] for r in kj_refs]
    rows = grp * tq

    @pl.when(kjs[0] == qi)
    def _():
        qt = q_ref[...].T
        for g in range(ngrp):
            qg = jnp.concatenate(
                [qt[(g * grp + hh) * hd:(g * grp + hh + 1) * hd, :] for hh in range(grp)], axis=1)
            qs_scr[g] = (qg * (scale * LOG2E)).astype(BF16)
            gate = _dot(mean_ref[:, g * hd:(g + 1) * hd], qg, HI)
            sel_scr[g] = _top_select(gate, qi, 0)
        m_scr[...] = jnp.full(m_scr.shape, NEG_INF, F32)
        l_scr[...] = jnp.zeros(l_scr.shape, F32)
        acc_scr[...] = jnp.zeros(acc_scr.shape, F32)

    ones_rows = jnp.ones((2 * SUBLANES, MOBA_BLOCK), BF16)

    def attend(k_ref, vt_ref, kj, diagonal):
        kgs = [k_ref[:, g * hd:(g + 1) * hd].astype(BF16) for g in range(ngrp)]
        vgs = [jnp.concatenate([vt_ref[g * hd:(g + 1) * hd, :].astype(BF16), ones_rows], axis=0)
               for g in range(ngrp)]
        sel_rows = None if diagonal else [sel_scr[g, pl.ds(kj, 1), :] for g in range(ngrp)]
        units = [(g, slice(c * UNIT_LANES, (c + 1) * UNIT_LANES))
                 for g in range(ngrp) for c in range(rows // UNIT_LANES)]
        def scores(g, ls):
            s = _dot(kgs[g], qs_scr[g, :, ls])
            if diagonal:
                kpos = lax.broadcasted_iota(jnp.int32, s.shape, 0)
                lane = lax.broadcasted_iota(jnp.int32, s.shape, 1)
                s = jnp.where(kpos <= jnp.bitwise_and(lane + ls.start, tq - 1), s, NEG_INF)
            return s

        def probs(g, ls, s):
            m_prev = m_scr[g, :, ls]
            m_cand = jnp.maximum(m_prev, jnp.max(s, axis=0, keepdims=True))
            if diagonal:
                m_new = m_ref = m_cand
            else:
                picked = sel_rows[g][:, ls] > 0.5
                m_new = jnp.where(picked, m_cand, m_prev)
                m_ref = jnp.where(picked, m_cand, POS_INF)
            return jnp.exp2(m_prev - m_new), m_new, jnp.exp2(s - m_ref).astype(BF16)

        def finish(g, ls, alpha, m_new, p):
            pv = _dot(vgs[g], p)
            acc_scr[g, :, ls] = alpha * acc_scr[g, :, ls] + pv[:hd, :]
            l_scr[g, :, ls] = alpha * l_scr[g, :, ls] + pv[hd:hd + 1, :]
            m_scr[g, :, ls] = m_new

        n = len(units)
        s_q, p_q = {}, {}
        for step in range(n + SKEW_S + SKEW_P):
            if step < n:
                s_q[step] = scores(*units[step])
            j = step - SKEW_S
            if 0 <= j < n:
                p_q[j] = probs(*units[j], s_q.pop(j))
            k = j - SKEW_P
            if 0 <= k < n:
                finish(*units[k], *p_q.pop(k))

    @pl.when(kjs[0] == qi)
    def _():
        attend(k_refs[0], vt_refs[0], kjs[0], True)

    for k_ref, vt_ref, kj in zip(k_refs, vt_refs, kjs):
        @pl.when(kj < qi)
        def _(k_ref=k_ref, vt_ref=vt_ref, kj=kj):
            attend(k_ref, vt_ref, kj, False)

    @pl.when(last_ref[p] == 1)
    def _():
        outs = []
        for g in range(ngrp):
            og = acc_scr[g] / l_scr[g]
            outs += [og[:, hh * tq:(hh + 1) * tq] for hh in range(grp)]
        o_ref[...] = jnp.concatenate(outs, axis=0).T


def _moba_prompt(q, k, vt, means, *, nb_batch, t_len, ngrp, grp, hd):
    tq = MOBA_BLOCK
    nq = t_len // tq
    nb = t_len // MOBA_BLOCK
    npast = MOBA_PAST_PER_STEP
    steps = []
    for i in range(nq):
        steps.append([i] + [i] * npast + [int(i == 0)])
        for j in range(0, i, npast):
            steps.append([i] + [min(j + s, i) for s in range(npast)] + [int(j + npast >= i)])
    tabs = [jnp.asarray(np.array([st[c] for st in steps], np.int32)) for c in range(npast + 2)]
    rows = grp * tq
    w = ngrp * grp * hd
    kvw = ngrp * hd
    kern = functools.partial(_moba_prompt_kernel, tq=tq, ngrp=ngrp, grp=grp, hd=hd,
                             scale=float(hd) ** -0.5)
    q_map = lambda b, p, *tb: (b * nq + tb[0][p], 0)
    k_spec = lambda s: pl.BlockSpec((MOBA_BLOCK, kvw), lambda b, p, *tb: (b * nb + tb[1 + s][p], 0))
    vt_spec = lambda s: pl.BlockSpec((None, kvw, MOBA_BLOCK), lambda b, p, *tb: (b, 0, tb[1 + s][p]))
    grid_spec = pltpu.PrefetchScalarGridSpec(
        num_scalar_prefetch=npast + 2,
        grid=(nb_batch, len(steps)),
        in_specs=[pl.BlockSpec((tq, w), q_map)]
        + [k_spec(s) for s in range(npast)] + [vt_spec(s) for s in range(npast)]
        + [pl.BlockSpec((nb, kvw), lambda b, p, *tb: (b, 0))],
        out_specs=pl.BlockSpec((tq, w), q_map),
        scratch_shapes=[pltpu.VMEM((ngrp, hd, rows), BF16),
                        pltpu.VMEM((ngrp, nb, rows), F32),
                        pltpu.VMEM((ngrp, 1, rows), F32),
                        pltpu.VMEM((ngrp, 1, rows), F32),
                        pltpu.VMEM((ngrp, hd, rows), F32)],
    )
    return pl.pallas_call(
        kern,
        grid_spec=grid_spec,
        out_shape=jax.ShapeDtypeStruct(q.shape, F32),
        compiler_params=_cparams(("parallel", "arbitrary")),
        name="moba_prompt",
    )(*tabs, q, *([k] * npast), *([vt] * npast), means)


def _sample_partial_kernel(pt_ref, q_ref, *refs, npg, ngrp, grp, hd, scale):
    k_pages = refs[:npg]
    v_pages = refs[npg:2 * npg]
    mean_ref, m_ref, l_ref, po_ref, qs_scr = refs[2 * npg:]
    step = pl.program_id(1)
    t_new = q_ref.shape[0]
    gr = grp * t_new

    @pl.when(step == 0)
    def _():
        for g in range(ngrp):
            qg = jnp.concatenate(
                [q_ref[:, (g * grp + hh) * hd:(g * grp + hh + 1) * hd] for hh in range(grp)], axis=0)
            qs_scr[g] = (qg * scale).astype(BF16)
        mean_ref[...] = jnp.zeros(mean_ref.shape, F32)
        m_ref[...] = jnp.zeros(m_ref.shape, F32)
        l_ref[...] = jnp.zeros(l_ref.shape, F32)

    ppb = MOBA_BLOCK // k_pages[0].shape[-1]
    bps = npg // ppb
    lane = lax.broadcasted_iota(jnp.int32, m_ref.shape, 1)
    lane_hd = lax.broadcasted_iota(jnp.int32, (hd, LANES), 1)
    blocks = [slice(blk * ppb, (blk + 1) * ppb) for blk in range(bps)]
    kts = [[jnp.concatenate([r[g] for r in k_pages[sl]], axis=1) for g in range(ngrp)]
           for sl in blocks]
    scs = [jnp.concatenate([_dot(qs_scr[g], kt[g].astype(BF16)) for g in range(ngrp)], axis=0)
           for kt in kts]
    ms = [jnp.max(sc, axis=1, keepdims=True) for sc in scs]
    prs = [jnp.exp(sc - m) for sc, m in zip(scs, ms)]
    ls = [jnp.sum(pr, axis=1, keepdims=True) for pr in prs]
    outs = []
    for sl, pr in zip(blocks, prs):
        prb = pr.astype(BF16)
        outs.append(jnp.concatenate(
            [_dot_nt(prb[g * gr:(g + 1) * gr, :],
                     jnp.concatenate([r[g] for r in v_pages[sl]], axis=1).astype(BF16))
             for g in range(ngrp)], axis=0))
    m_tile = m_ref[...]
    l_tile = l_ref[...]
    for blk in range(bps):
        bidx = step * bps + blk
        m_tile = jnp.where(lane == bidx, ms[blk], m_tile)
        l_tile = jnp.where(lane == bidx, ls[blk], l_tile)
        for g in range(ngrp):
            mean_col = jnp.sum(kts[blk][g], axis=1, keepdims=True) * (1.0 / MOBA_BLOCK)
            rs = slice(g * hd, (g + 1) * hd)
            mean_ref[rs, :] = jnp.where(lane_hd == bidx, mean_col, mean_ref[rs, :])
    m_ref[...] = m_tile
    l_ref[...] = l_tile
    for blk in range(bps):
        po_ref[blk] = outs[blk].T


def _sample_partials(q, cache_k, cache_v, page_table, *, npg, ngrp, grp, hd):
    db, n_pages = page_table.shape
    page = cache_k.shape[-1]
    kvw = ngrp * hd
    t_new = q.shape[0] // db
    ppb = MOBA_BLOCK // page
    bps = npg // ppb
    nsteps = n_pages // npg
    nblk = n_pages // ppb
    nrow = ngrp * grp * t_new
    assert nblk <= LANES
    kern = functools.partial(_sample_partial_kernel, npg=npg, ngrp=ngrp, grp=grp, hd=hd,
                             scale=float(hd) ** -0.5)

    def page_spec(i):
        return pl.BlockSpec((None, ngrp, hd, page),
                            lambda b, s, pt: (pt[b * n_pages + s * npg + i], 0, 0, 0))

    grid_spec = pltpu.PrefetchScalarGridSpec(
        num_scalar_prefetch=1,
        grid=(db, nsteps),
        in_specs=[pl.BlockSpec((t_new, q.shape[1]), lambda b, s, pt: (b, 0))]
        + [page_spec(i) for i in range(npg)] + [page_spec(i) for i in range(npg)],
        out_specs=[pl.BlockSpec((None, kvw, LANES), lambda b, s, pt: (b, 0, 0)),
                   pl.BlockSpec((None, nrow, LANES), lambda b, s, pt: (b, 0, 0)),
                   pl.BlockSpec((None, nrow, LANES), lambda b, s, pt: (b, 0, 0)),
                   pl.BlockSpec((None, bps, hd, nrow), lambda b, s, pt: (b, s, 0, 0))],
        scratch_shapes=[pltpu.VMEM((ngrp, grp * t_new, hd), BF16)],
    )
    return pl.pallas_call(
        kern,
        grid_spec=grid_spec,
        out_shape=[jax.ShapeDtypeStruct((db, kvw, LANES), F32),
                   jax.ShapeDtypeStruct((db, nrow, LANES), F32),
                   jax.ShapeDtypeStruct((db, nrow, LANES), F32),
                   jax.ShapeDtypeStruct((db, nblk, hd, nrow), F32)],
        compiler_params=_cparams(("parallel", "arbitrary")),
        name="moba_sample_partials",
    )(page_table.reshape(-1), q, *([cache_k] * npg), *([cache_v] * npg))


def _sample_combine_kernel(q_ref, kn_ref, vn_ref, mean_ref, m_ref, l_ref, po_ref, o_ref,
                           *, nblk, ngrp, grp, hd, scale):
    t_new = q_ref.shape[0]
    gr = grp * t_new
    nrow = ngrp * gr
    gates, qss = [], []
    for g in range(ngrp):
        qg = jnp.concatenate(
            [q_ref[:, (g * grp + hh) * hd:(g * grp + hh + 1) * hd] for hh in range(grp)], axis=0)
        qss.append((qg * scale).astype(BF16))
        gates.append(_dot(qg, mean_ref[g * hd:(g + 1) * hd, :], HI))
    sel = _top_select(jnp.concatenate(gates, axis=0), nblk, 1) > 0.5
    m_all = m_ref[...]
    m_past = jnp.max(jnp.where(sel, m_all, NEG_INF), axis=1, keepdims=True)
    kpos = lax.broadcasted_iota(jnp.int32, (nrow, t_new), 1)
    qpos = jnp.bitwise_and(lax.broadcasted_iota(jnp.int32, (nrow, t_new), 0), t_new - 1)
    kn = kn_ref[...].astype(BF16)
    vn = vn_ref[...].astype(BF16)
    s_own = jnp.concatenate(
        [_dot_nt(qss[g], kn[:, g * hd:(g + 1) * hd]) for g in range(ngrp)], axis=0)
    s_own = jnp.where(kpos <= qpos, s_own, NEG_INF)
    m_fin = jnp.maximum(m_past, jnp.max(s_own, axis=1, keepdims=True))
    w = jnp.where(sel, jnp.exp(m_all - m_fin), 0.0)
    p_own = jnp.exp(s_own - m_fin)
    l_fin = (jnp.sum(w * l_ref[...], axis=1, keepdims=True)
             + jnp.sum(p_own, axis=1, keepdims=True))
    pb = p_own.astype(BF16)
    o_own = jnp.concatenate(
        [_dot(pb[g * gr:(g + 1) * gr, :], vn[:, g * hd:(g + 1) * hd]) for g in range(ngrp)], axis=0)
    w_t = w.T
    acc_t = jnp.zeros((hd, nrow), F32)
    for j in range(nblk):
        acc_t = acc_t + w_t[j:j + 1, :] * po_ref[j]
    o = (acc_t.T + o_own) / l_fin
    o_ref[...] = jnp.concatenate(
        [o[r * t_new:(r + 1) * t_new, :] for r in range(ngrp * grp)], axis=1)


def _sample_combine(q, k_new, v_new, means_t, m_part, l_part, po, *, db, nblk, ngrp, grp, hd):
    t_new = q.shape[0] // db
    nrow = ngrp * grp * t_new
    kvw = ngrp * hd
    kern = functools.partial(_sample_combine_kernel, nblk=nblk, ngrp=ngrp, grp=grp, hd=hd,
                             scale=float(hd) ** -0.5)
    return pl.pallas_call(
        kern,
        grid=(db,),
        in_specs=[pl.BlockSpec((t_new, q.shape[1]), lambda b: (b, 0)),
                  pl.BlockSpec((t_new, kvw), lambda b: (b, 0)),
                  pl.BlockSpec((t_new, kvw), lambda b: (b, 0)),
                  pl.BlockSpec((None, kvw, LANES), lambda b: (b, 0, 0)),
                  pl.BlockSpec((None, nrow, LANES), lambda b: (b, 0, 0)),
                  pl.BlockSpec((None, nrow, LANES), lambda b: (b, 0, 0)),
                  pl.BlockSpec((None,) + po.shape[1:], lambda b: (b, 0, 0, 0))],
        out_specs=pl.BlockSpec((t_new, q.shape[1]), lambda b: (b, 0)),
        out_shape=jax.ShapeDtypeStruct(q.shape, F32),
        compiler_params=_cparams(("parallel",)),
        name="moba_sample_combine",
    )(q, k_new, v_new, means_t, m_part, l_part, po)


def _rope_tables(pos, hd, reps):
    half = hd // 2
    inv_freq = ROPE_THETA ** (-jnp.arange(half, dtype=F32) / half)
    ang = pos.astype(F32)[:, None] * inv_freq[None, :]
    cos = jnp.cos(ang)
    sin = jnp.sin(ang)
    per_vreg = LANES // hd
    cos_t = jnp.tile(jnp.concatenate([cos, cos], axis=1), (reps, per_vreg))
    sin_t = jnp.tile(jnp.concatenate([-sin, sin], axis=1), (reps, per_vreg))
    return cos_t, sin_t


def _trunk(x, pos, conv_state8, delta_state, attend, wts, *, nseq, t_len, tm, prep_tt,
           delta_ct, delta_c, dims, want_t=False):
    nh, dk, conv_dim, n_a, depth, alpha, bh, kvh, hd = dims
    rows = nseq * t_len
    new_conv, new_delta = [], []
    k_sh = v_sh = t_sh = None
    for layer in range(depth):
        if layer < n_a:
            cw = wts["a_conv"][layer].shape[0]
            prep_args = (conv_state8[layer], wts["a_conv"][layer], wts["a_log"][layer],
                         wts["a_dtb"][layer])
            prep_kw = dict(nseq=nseq, t_len=t_len, tt=prep_tt, conv_dim=conv_dim, nh=nh, dk=dk)
            if t_len >= tm:
                qkv, gb, z_src, tail = _gdn_in_fused(x, wts["a_w_in"][layer], *prep_args, **prep_kw)
                z_blk = 0
                new_conv.append(tail[:, SUBLANES - (cw - 1):, :])
            else:
                proj = _matmul(x, wts["a_w_in"][layer], rows, wts["a_w_in"][layer].shape[1] // 3,
                               "in_proj")
                qkv, gb = _gdn_prep(proj, *prep_args, **prep_kw)
                z_src, z_blk = proj, conv_dim // (nh * dk)
                new_conv.append(proj.reshape(nseq, t_len, -1)[:, t_len - (cw - 1):, :conv_dim])
            o, s_new = _delta_rule(qkv, gb, delta_state[layer], nseq=nseq, t_len=t_len,
                                   ct=delta_ct, c=delta_c, nh=nh, dk=dk)
            new_delta.append(s_new)
            mix, w_mix = o, wts["a_w_out"][layer]
            mix_kw = dict(z_src=z_src, z_blk=z_blk, norm_w=wts["a_norm"][layer], nh=nh, dv=dk)
        else:
            j = layer - n_a
            w_all = wts["b_w_qkv"][j]
            cos_t, sin_t = _rope_tables(pos, hd, (tm // t_len) if tm > t_len else 1)
            q, k_l, v_l, *t_l = _qkv_rope(x, w_all, cos_t, sin_t, tm=tm, nq=bh * hd, nk=kvh * hd,
                                          half=hd // 2, t_seq=t_len if want_t else None)
            if layer == n_a:
                k_sh, v_sh, t_sh = k_l, v_l, t_l
            mix, w_mix, mix_kw = attend(q, k_sh, v_sh, *t_sh), wts["b_w_o"][j], {}
        ln1 = (wts["ln_g"][layer][0], wts["ln_b"][layer][0])
        ln2 = (wts["ln_g"][layer][1], wts["ln_b"][layer][1])
        x = _mixer_ffn(mix, x, w_mix, ln1, wts["ffn_wg"][layer], wts["ffn_wu"][layer],
                       wts["ffn_wd"][layer], ln2, tm=tm, tf=256, alpha=alpha, **mix_kw)
    return x, jnp.stack(new_conv), jnp.stack(new_delta), k_sh, v_sh, t_sh


def kernel(x_prompt, x_sample, state_conv, state_delta, cache_k, cache_v, page_table, ln_g, ln_b,
           a_w_in, a_conv, a_log_decay, a_dt_bias, a_norm, a_w_out, kv_w_k, kv_w_v, b_w_q, b_w_o,
           ffn_w_gate, ffn_w_up, ffn_w_down):
    bp, tp, d = x_prompt.shape
    db, ts, _ = x_sample.shape
    depth = ln_g.shape[0]
    n_a = a_w_in.shape[0]
    nh = a_log_decay.shape[1]
    dk = state_delta.shape[-2]
    conv_dim = a_conv.shape[-1]
    cw = a_conv.shape[1]
    n_pool, page, kvh, hd = cache_k.shape
    bh = b_w_q.shape[-1] // hd
    grp = bh // kvh
    n_pages = page_table.shape[1]
    past = n_pages * page
    alpha = (2.0 * depth) ** 0.25
    assert past % MOBA_BLOCK == 0 and ts <= MOBA_BLOCK and ts >= cw - 1 and tp >= cw - 1
    assert state_delta.shape[-1] == dk and conv_dim == 3 * nh * dk

    in_w = a_w_in.shape[-1]
    in_pad = -(-in_w // (3 * LANES)) * (3 * LANES)
    pad_lanes = lambda v: jnp.pad(v, ((0, 0), (0, LANES - v.shape[-1])))[:, None, :]
    wts = {
        "a_w_in": [jnp.pad(a_w_in[l].astype(BF16), ((0, 0), (0, in_pad - in_w))) for l in range(n_a)],
        "a_conv": a_conv,
        "a_log": pad_lanes(a_log_decay),
        "a_dtb": pad_lanes(a_dt_bias),
        "a_norm": a_norm[:, None, :],
        "a_w_out": a_w_out.astype(BF16),
        "b_w_qkv": jnp.concatenate(
            [b_w_q, jnp.broadcast_to(kv_w_k, (b_w_q.shape[0],) + kv_w_k.shape),
             jnp.broadcast_to(kv_w_v, (b_w_q.shape[0],) + kv_w_v.shape)], axis=-1).astype(BF16),
        "b_w_o": b_w_o.astype(BF16),
        "ffn_wg": ffn_w_gate.astype(BF16),
        "ffn_wu": ffn_w_up.astype(BF16),
        "ffn_wd": ffn_w_down.astype(BF16),
        "ln_g": ln_g[:, :, None, :],
        "ln_b": ln_b[:, :, None, :],
    }
    dims = (nh, dk, conv_dim, n_a, depth, alpha, bh, kvh, hd)

    nb = tp // MOBA_BLOCK

    def attend_prompt(q, k, v, kt, vt):
        total = bp * nb
        step = SUBLANES if total % SUBLANES == 0 else total
        means = _block_means(k, nblk_step=step)
        return _moba_prompt(q, k, vt, means, nb_batch=bp, t_len=tp, ngrp=kvh, grp=grp, hd=hd)

    tm_p = 512 if (bp * tp) % 512 == 0 else 256
    y_p, p_conv, p_delta, _, _, (p_kt, p_vt) = _trunk(
        x_prompt.reshape(bp * tp, d), jnp.arange(tp, dtype=jnp.int32),
        jnp.zeros((n_a, bp, SUBLANES, conv_dim), F32), jnp.zeros((n_a, bp, nh, dk, dk), F32),
        attend_prompt, wts, nseq=bp, t_len=tp, tm=tm_p, prep_tt=tm_p,
        delta_ct=DELTA_CHUNK * DELTA_CHUNKS_PER_ITER,
        delta_c=DELTA_CHUNK, dims=dims, want_t=True)
    p_k = jnp.transpose(p_kt.reshape(bp, kvh, hd, tp), (0, 3, 1, 2))
    p_v = jnp.transpose(p_vt.reshape(bp, kvh, hd, tp), (0, 3, 1, 2))

    ck = jnp.transpose(cache_k, (0, 2, 3, 1))
    cv = jnp.transpose(cache_v, (0, 2, 3, 1))
    npg = next(n for n in (16, 8, 2 * (MOBA_BLOCK // page)) if n_pages % n == 0)

    def attend_sample(q, k_new, v_new):
        means_t, m_part, l_part, po = _sample_partials(q, ck, cv, page_table, npg=npg, ngrp=kvh,
                                                       grp=grp, hd=hd)
        return _sample_combine(q, k_new, v_new, means_t, m_part, l_part, po, db=db,
                               nblk=past // MOBA_BLOCK, ngrp=kvh, grp=grp, hd=hd)

    conv8 = jnp.pad(state_conv, ((0, 0), (0, 0), (SUBLANES - (cw - 1), 0), (0, 0)))
    y_s, s_conv, s_delta, s_k, s_v, _ = _trunk(
        x_sample.reshape(db * ts, d), past + jnp.arange(ts, dtype=jnp.int32), conv8, state_delta,
        attend_sample, wts, nseq=db, t_len=ts, tm=db * ts, prep_tt=ts, delta_ct=ts, delta_c=ts,
        dims=dims)

    return (y_p.reshape(bp, tp, d), y_s.reshape(db, ts, d), p_conv, p_delta, p_k, p_v,
            s_conv, s_delta, s_k.reshape(db, ts, kvh, hd), s_v.reshape(db, ts, kvh, hd))
```

```python
import functools

import numpy as np
import jax
import jax.numpy as jnp
from jax import lax
from jax.experimental import pallas as pl
from jax.experimental.pallas import tpu as pltpu

F32 = jnp.float32
BF16 = jnp.bfloat16
HI = lax.Precision.HIGHEST

LANES = 128
SUBLANES = 8
VMEM_LIMIT_MB = 56

MOBA_BLOCK = 256
MOBA_TOPK = 3
MOBA_PAST_PER_STEP = 2
UNIT_LANES = 256
SKEW_S = 4
SKEW_P = 2
DELTA_CHUNK = 64
DELTA_SOLVE_GROUP = 4
DELTA_CHUNKS_PER_ITER = 8
ROPE_THETA = 10000.0
LN_EPS = 1e-5
RMS_EPS = 1e-6
L2_EPS = 1e-6
NEG_INF = float("-inf")
POS_INF = float("inf")
LOG2E = 1.4426950408889634


def _cparams(sem):
    return pltpu.CompilerParams(dimension_semantics=sem,
                                vmem_limit_bytes=VMEM_LIMIT_MB * 1024 * 1024)


def _dot(a, b, prec=None):
    return jnp.dot(a, b, precision=prec, preferred_element_type=F32)


def _dot_nt(a, b, prec=None):
    return lax.dot_general(a, b, (((1,), (1,)), ((), ())), precision=prec,
                           preferred_element_type=F32)


def _dot_tn(a, b, prec=None):
    return lax.dot_general(a, b, (((0,), (0,)), ((), ())), precision=prec,
                           preferred_element_type=F32)


def _split(a):
    hi = a.astype(BF16)
    return hi, (a - hi.astype(F32)).astype(BF16)


def _dot3(a, b):
    return _dot(a[0], b[0]) + (_dot(a[0], b[1]) + _dot(a[1], b[0]))


def _sigmoid(x):
    return 1.0 / (1.0 + jnp.exp(-x))


def _layer_norm(v, g, b):
    mu = jnp.mean(v, axis=-1, keepdims=True)
    d = v - mu
    var = jnp.mean(d * d, axis=-1, keepdims=True)
    return d * lax.rsqrt(var + LN_EPS) * g + b


def _mm_kernel(x_ref, w_ref, o_ref):
    o_ref[...] = _dot(x_ref[...].astype(BF16), w_ref[...])


def _matmul(x, w, tm, tn, name):
    r, k = x.shape
    n = w.shape[1]
    return pl.pallas_call(
        _mm_kernel,
        grid=(r // tm, n // tn),
        in_specs=[pl.BlockSpec((tm, k), lambda i, j: (i, 0)),
                  pl.BlockSpec((k, tn), lambda i, j: (0, j))],
        out_specs=pl.BlockSpec((tm, tn), lambda i, j: (i, j)),
        out_shape=jax.ShapeDtypeStruct((r, n), F32),
        compiler_params=_cparams(("parallel", "parallel")),
        name=name,
    )(x, w)


def _conv_norm(xp_ref, cw_ref, qkv_ref, cols, *, tt, conv_w, n_q, n_qk, q_scale):
    hist = SUBLANES
    for c in cols:
        cs = slice(c * LANES, (c + 1) * LANES)
        full = xp_ref[0:hist + tt, cs]
        y = pltpu.roll(full, conv_w - 1, 0)[hist:] * cw_ref[0:1, cs]
        for i in range(1, conv_w - 1):
            y = y + pltpu.roll(full, conv_w - 1 - i, 0)[hist:] * cw_ref[i:i + 1, cs]
        y = y + full[hist:] * cw_ref[conv_w - 1:conv_w, cs]
        y = y * _sigmoid(y)
        if c < n_qk:
            r = lax.rsqrt(jnp.sum(y * y, axis=-1, keepdims=True) + L2_EPS)
            y = y * (r * q_scale if c < n_q else r)
        qkv_ref[:, cs] = y


def _decay_beta(ab, alog, dtb, nh):
    lane = lax.broadcasted_iota(jnp.int32, ab.shape, 1)
    sp = ab + dtb
    softplus = jnp.maximum(sp, 0.0) + jnp.log1p(jnp.exp(-jnp.abs(sp)))
    return jnp.where(lane < nh, -jnp.exp(alog) * softplus, _sigmoid(ab))


def _prep_kernel(cur_ref, prev_ref, st_ref, ab_ref, cw_ref, alog_ref, dtb_ref,
                 qkv_ref, gb_ref, xp_ref, *, tt, conv_w, n_q, n_qk, nh, q_scale):
    t = pl.program_id(1)
    hist = SUBLANES

    @pl.when(t == 0)
    def _():
        xp_ref[0:hist, :] = st_ref[...]

    @pl.when(t != 0)
    def _():
        xp_ref[0:hist, :] = prev_ref[...]

    xp_ref[hist:hist + tt, :] = cur_ref[...]
    _conv_norm(xp_ref, cw_ref, qkv_ref, range(cur_ref.shape[1] // LANES), tt=tt, conv_w=conv_w,
               n_q=n_q, n_qk=n_qk, q_scale=q_scale)
    gb_ref[...] = _decay_beta(ab_ref[...], alog_ref[...], dtb_ref[...], nh)


def _gdn_in_kernel(x_ref, w_ref, st_ref, cw_ref, alog_ref, dtb_ref, qkv_ref, gb_ref, z_ref, tail_ref,
                   xp_ref, *, tt, conv_dim, zw, conv_w, n_q, n_qk, nh, q_scale):
    hist = SUBLANES

    @pl.when(pl.program_id(1) == 0)
    def _():
        xp_ref[0:hist, :] = st_ref[...]

    xb = x_ref[...].astype(BF16)
    mxu_w = 2 * LANES
    for c in range(conv_dim // mxu_w):
        cs = slice(c * mxu_w, (c + 1) * mxu_w)
        xp_ref[hist:hist + tt, cs] = _dot(xb, w_ref[:, cs])
        _conv_norm(xp_ref, cw_ref, qkv_ref, range(2 * c, 2 * c + 2), tt=tt, conv_w=conv_w,
                   n_q=n_q, n_qk=n_qk, q_scale=q_scale)
    z_ref[...] = _dot(xb, w_ref[:, conv_dim:conv_dim + zw])
    ab = _dot(xb, w_ref[:, conv_dim + zw:conv_dim + zw + LANES])
    gb_ref[...] = _decay_beta(ab, alog_ref[...], dtb_ref[...], nh)
    tail = xp_ref[tt:tt + hist, :]
    tail_ref[...] = tail
    xp_ref[0:hist, :] = tail


def _gdn_in_fused(x, w, state8, conv_w, alog, dtb, *, nseq, t_len, tt, conv_dim, nh, dk):
    rows, d = x.shape
    nt = t_len // tt
    cw = conv_w.shape[0]
    zw = nh * dk
    kern = functools.partial(_gdn_in_kernel, tt=tt, conv_dim=conv_dim, zw=zw, conv_w=cw, n_q=nh,
                             n_qk=2 * nh, nh=nh, q_scale=float(dk) ** -0.5)
    row = lambda width: pl.BlockSpec((tt, width), lambda s, t: (s * nt + t, 0))
    const = lambda shape: pl.BlockSpec(shape, lambda s, t: (0, 0))
    per_seq = pl.BlockSpec((None, SUBLANES, conv_dim), lambda s, t: (s, 0, 0))
    return pl.pallas_call(
        kern,
        grid=(nseq, nt),
        in_specs=[row(d),
                  pl.BlockSpec(w.shape, lambda s, t: (0, 0), pipeline_mode=pl.Buffered(1)),
                  per_seq, const((cw, conv_dim)), const((1, LANES)), const((1, LANES))],
        out_specs=[row(conv_dim), row(LANES), row(zw), per_seq],
        out_shape=[jax.ShapeDtypeStruct((rows, conv_dim), F32),
                   jax.ShapeDtypeStruct((rows, LANES), F32),
                   jax.ShapeDtypeStruct((rows, zw), F32),
                   jax.ShapeDtypeStruct((nseq, SUBLANES, conv_dim), F32)],
        scratch_shapes=[pltpu.VMEM((tt + SUBLANES, conv_dim), F32)],
        compiler_params=_cparams(("parallel", "arbitrary")),
        name="gdn_in_fused",
    )(x, w, state8, conv_w, alog, dtb)


def _gdn_prep(proj, state8, conv_w, alog, dtb, *, nseq, t_len, tt, conv_dim, nh, dk):
    rows = nseq * t_len
    nt = t_len // tt
    cw = conv_w.shape[0]
    ab_blk = (conv_dim + nh * dk) // LANES
    kern = functools.partial(_prep_kernel, tt=tt, conv_w=cw, n_q=nh, n_qk=2 * nh, nh=nh,
                             q_scale=float(dk) ** -0.5)
    tpb = tt // SUBLANES
    return pl.pallas_call(
        kern,
        grid=(nseq, nt),
        in_specs=[
            pl.BlockSpec((tt, conv_dim), lambda s, t: (s * nt + t, 0)),
            pl.BlockSpec((SUBLANES, conv_dim),
                         lambda s, t: (jnp.maximum((s * nt + t) * tpb - 1, 0), 0)),
            pl.BlockSpec((None, SUBLANES, conv_dim), lambda s, t: (s, 0, 0)),
            pl.BlockSpec((tt, LANES), lambda s, t: (s * nt + t, ab_blk)),
            pl.BlockSpec((cw, conv_dim), lambda s, t: (0, 0)),
            pl.BlockSpec((1, LANES), lambda s, t: (0, 0)),
            pl.BlockSpec((1, LANES), lambda s, t: (0, 0)),
        ],
        out_specs=[pl.BlockSpec((tt, conv_dim), lambda s, t: (s * nt + t, 0)),
                   pl.BlockSpec((tt, LANES), lambda s, t: (s * nt + t, 0))],
        out_shape=[jax.ShapeDtypeStruct((rows, conv_dim), F32),
                   jax.ShapeDtypeStruct((rows, LANES), F32)],
        scratch_shapes=[pltpu.VMEM((tt + SUBLANES, conv_dim), F32)],
        compiler_params=_cparams(("parallel", "arbitrary")),
        name="gdn_prep",
    )(proj, proj, state8, proj, conv_w, alog, dtb)


def _delta_kernel(q_ref, k_ref, v_ref, gb_ref, s0_ref, o_ref, s_ref, st_scr, *, sb, ct, c, nh, dk):
    t = pl.program_id(1)

    @pl.when(t == 0)
    def _():
        st_scr[...] = s0_ref[...]

    row = lax.broadcasted_iota(jnp.int32, (c, c), 0)
    col = lax.broadcasted_iota(jnp.int32, (c, c), 1)
    causal = row >= col
    strict = row > col
    tril = causal.astype(F32)
    eye = (row == col).astype(F32)
    n_sq = max(int(np.ceil(np.log2(c))) - 1, 0)

    def prepare(rs, seq):
        gbc = gb_ref[rs, :]
        gc = _dot(tril, gbc, HI)
        gt = gc.T
        eg = jnp.exp(gc)
        units = []
        for h in range(nh):
            cs = slice(h * dk, (h + 1) * dk)
            qh = q_ref[rs, cs]
            kh = k_ref[rs, cs]
            vh = v_ref[rs, cs]
            gcol = gc[:, h:h + 1]
            grow = gt[h:h + 1, :]
            bcol = gbc[:, nh + h:nh + h + 1]
            egcol = eg[:, h:h + 1]
            decay = jnp.where(causal, jnp.exp(jnp.where(causal, gcol - grow, 0.0)), 0.0)
            kb = kh.astype(BF16)
            kk = _dot_nt(kb, kb)
            qk = _dot_nt(qh.astype(BF16), kb)
            glast = grow[:, c - 1:c]
            units.append(dict(
                rs=rs, h=h, seq=seq,
                x=-jnp.where(strict, bcol * kk * decay, 0.0),
                qkm=jnp.where(causal, qk * decay, 0.0).astype(BF16),
                rhs=jnp.concatenate([vh * bcol, kh * (bcol * egcol)], axis=-1),
                qd=(qh * egcol).astype(BF16),
                kd=(kh * jnp.exp(glast - gcol)).astype(BF16),
                glast=jnp.exp(glast)))
        return units

    def solve(units):
        ps = [u["x"] for u in units]
        tms = [eye + p for p in ps]
        for _ in range(n_sq):
            sp = [_split(p) for p in ps]
            ps = [_dot3(s, s) for s in sp]
            sp = [_split(p) for p in ps]
            tms = [tm + _dot3(_split(tm), s) for tm, s in zip(tms, sp)]
        return [_dot3(_split(tm), _split(u["rhs"])) for tm, u in zip(tms, units)]

    def update(u, sol):
        h, seq = u["h"], u["seq"]
        s = st_scr[seq, h]
        s_b = s.astype(BF16)
        v_new = sol[:, :dk] - _dot(sol[:, dk:].astype(BF16), s_b)
        vb = v_new.astype(BF16)
        o_ref[u["rs"], h * dk:(h + 1) * dk] = _dot(u["qd"], s_b) + _dot(u["qkm"], vb)
        st_scr[seq, h] = s * u["glast"] + _dot_tn(u["kd"], vb)

    def chunks(r0, n):
        groups = []
        for seq in range(sb):
            for i0 in range(0, n, DELTA_SOLVE_GROUP):
                units = []
                for i in range(i0, min(i0 + DELTA_SOLVE_GROUP, n)):
                    units += prepare(pl.ds(seq * ct + r0 + i * c, c), seq)
                groups.append(units)
        if sb > 1:
            groups = [sum(groups, [])]
        solved = None
        for units in groups:
            sols = solve(units)
            if solved is not None:
                for u, sol in zip(*solved):
                    update(u, sol)
            solved = (units, sols)
        for u, sol in zip(*solved):
            update(u, sol)

    n_chunks = ct // c
    per_iter = DELTA_CHUNKS_PER_ITER if n_chunks % DELTA_CHUNKS_PER_ITER == 0 else 1
    if n_chunks == per_iter:
        chunks(0, per_iter)
    else:
        def body(ci, carry):
            chunks(pl.multiple_of(ci * (per_iter * c), per_iter * c), per_iter)
            return carry
        lax.fori_loop(0, n_chunks // per_iter, body, 0)

    @pl.when(t == pl.num_programs(1) - 1)
    def _():
        s_ref[...] = st_scr[...]


def _delta_rule(qkv, gb, s0, *, nseq, t_len, ct, c, nh, dk):
    rows = nseq * t_len
    nt = t_len // ct
    w = nh * dk
    sb = DELTA_SOLVE_GROUP if (nt == 1 and ct == c and nseq % DELTA_SOLVE_GROUP == 0) else 1
    kern = functools.partial(_delta_kernel, sb=sb, ct=ct, c=c, nh=nh, dk=dk)
    return pl.pallas_call(
        kern,
        grid=(nseq // sb, nt),
        in_specs=[
            pl.BlockSpec((sb * ct, w), lambda s, t: (s * nt + t, 0)),
            pl.BlockSpec((sb * ct, w), lambda s, t: (s * nt + t, 1)),
            pl.BlockSpec((sb * ct, w), lambda s, t: (s * nt + t, 2)),
            pl.BlockSpec((sb * ct, LANES), lambda s, t: (s * nt + t, 0)),
            pl.BlockSpec((sb, nh, dk, dk), lambda s, t: (s, 0, 0, 0)),
        ],
        out_specs=[pl.BlockSpec((sb * ct, w), lambda s, t: (s * nt + t, 0)),
                   pl.BlockSpec((sb, nh, dk, dk), lambda s, t: (s, 0, 0, 0))],
        out_shape=[jax.ShapeDtypeStruct((rows, w), F32),
                   jax.ShapeDtypeStruct((nseq, nh, dk, dk), F32)],
        scratch_shapes=[pltpu.VMEM((sb, nh, dk, dk), F32)],
        compiler_params=_cparams(("parallel", "arbitrary")),
        name="delta_rule",
    )(qkv, qkv, qkv, gb, s0)


def _mixer_ffn_kernel(*refs, gated, nh, dv, tf, alpha):
    if gated:
        o_ref, z_ref, nw_ref, *refs = refs
    else:
        o_ref, *refs = refs
    (x_ref, wo_ref, g1_ref, b1_ref, wg_ref, wu_ref, wd_ref, g2_ref, b2_ref,
     out_ref, a_scr, h_scr, acc_scr) = refs
    step = pl.program_id(0)
    n_tiles = pl.num_programs(0) - 1
    cur = lax.rem(step, 2)
    prev = 1 - cur

    def mix():
        if gated:
            for h in range(nh):
                cs = slice(h * dv, (h + 1) * dv)
                oh = o_ref[:, cs]
                zh = z_ref[:, cs]
                oh = oh * lax.rsqrt(jnp.mean(oh * oh, axis=-1, keepdims=True) + RMS_EPS) * nw_ref[...]
                a_scr[:, cs] = (oh * (zh * _sigmoid(zh))).astype(BF16)
        else:
            a_scr[...] = o_ref[...].astype(BF16)
        h_scr[cur] = _layer_norm(alpha * x_ref[...] + _dot(a_scr[...], wo_ref[...]),
                                 g1_ref[...], b1_ref[...])

    def ffn(with_mix):
        hb = h_scr[prev].astype(BF16)
        for c in range(wg_ref.shape[1] // tf):
            fs = slice(c * tf, (c + 1) * tf)
            gate = _dot(hb, wg_ref[:, fs])
            up = _dot(hb, wu_ref[:, fs])
            act = (gate * _sigmoid(gate) * up).astype(BF16)
            down = _dot(act, wd_ref[fs, :])
            if c == 0:
                acc_scr[...] = down
            else:
                acc_scr[...] += down
            if with_mix and c == 0:
                mix()
        out_ref[...] = _layer_norm(alpha * h_scr[prev] + acc_scr[...], g2_ref[...], b2_ref[...])

    @pl.when(step == 0)
    def _():
        mix()

    @pl.when(jnp.logical_and(step > 0, step < n_tiles))
    def _():
        ffn(True)

    @pl.when(step == n_tiles)
    def _():
        ffn(False)


def _mixer_ffn(o, x, wo, ln1, wg, wu, wd, ln2, *, tm, tf, alpha, z_src=None, z_blk=0,
               norm_w=None, nh=1, dv=1):
    rows, d_in = o.shape
    d = x.shape[1]
    f = wg.shape[1]
    gated = z_src is not None
    kern = functools.partial(_mixer_ffn_kernel, gated=gated, nh=nh, dv=dv, tf=tf, alpha=alpha)
    n_tiles = rows // tm
    in_row = lambda width, col=0: pl.BlockSpec(
        (tm, width), lambda s: (jnp.minimum(s, n_tiles - 1), col))
    const = lambda shape: pl.BlockSpec(shape, lambda s: (0, 0))
    resident = lambda shape: pl.BlockSpec(shape, lambda s: (0, 0), pipeline_mode=pl.Buffered(1))
    in_specs = [in_row(d_in)]
    args = [o]
    if gated:
        in_specs += [in_row(d_in, z_blk), const((1, dv))]
        args += [z_src, norm_w]
    in_specs += [in_row(d), resident((d_in, d)), const((1, d)), const((1, d)),
                 resident((d, f)), resident((d, f)), resident((f, d)), const((1, d)), const((1, d))]
    args += [x, wo, ln1[0], ln1[1], wg, wu, wd, ln2[0], ln2[1]]
    return pl.pallas_call(
        kern,
        grid=(n_tiles + 1,),
        in_specs=in_specs,
        out_specs=pl.BlockSpec((tm, d), lambda s: (jnp.maximum(s - 1, 0), 0)),
        out_shape=jax.ShapeDtypeStruct((rows, d), F32),
        scratch_shapes=[pltpu.VMEM((tm, d_in), BF16), pltpu.VMEM((2, tm, d), F32),
                        pltpu.VMEM((tm, d), F32)],
        compiler_params=_cparams(("arbitrary",)),
        name="mixer_ffn_ln",
    )(*args)


def _qkv_kernel(x_ref, w_ref, cos_ref, sin_ref, q_ref, k_ref, v_ref, *t_refs, nq, nk, half):
    y = _dot(x_ref[...].astype(BF16), w_ref[...])
    cos = cos_ref[...]
    sin = sin_ref[...]
    lane = lax.broadcasted_iota(jnp.int32, cos.shape, 1)
    first = jnp.bitwise_and(lane, 2 * half - 1) < half

    def rope(xg):
        partner = jnp.where(first, pltpu.roll(xg, LANES - half, 1), pltpu.roll(xg, half, 1))
        return xg * cos + partner * sin

    for c in range(nq // LANES):
        q_ref[:, c * LANES:(c + 1) * LANES] = rope(y[:, c * LANES:(c + 1) * LANES])
    k = jnp.concatenate([rope(y[:, nq + c * LANES:nq + (c + 1) * LANES])
                         for c in range(nk // LANES)], axis=1)
    k_ref[...] = k
    v_ref[...] = y[:, nq + nk:]
    if t_refs:
        kt_ref, vt_ref = t_refs
        kt_ref[...] = k.T
        vt_ref[...] = y[:, nq + nk:].T


def _qkv_rope(x, w, cos, sin, *, tm, nq, nk, half, t_seq=None):
    rows, d = x.shape
    n_tab = cos.shape[0] // tm
    kern = functools.partial(_qkv_kernel, nq=nq, nk=nk, half=half)
    out_specs = [pl.BlockSpec((tm, nq), lambda i: (i, 0)),
                 pl.BlockSpec((tm, nk), lambda i: (i, 0)),
                 pl.BlockSpec((tm, nk), lambda i: (i, 0))]
    out_shape = [jax.ShapeDtypeStruct((rows, nq), F32),
                 jax.ShapeDtypeStruct((rows, nk), F32),
                 jax.ShapeDtypeStruct((rows, nk), F32)]
    if t_seq is not None:
        nt = t_seq // tm
        out_specs += [pl.BlockSpec((None, nk, tm), lambda i: (i // nt, 0, i % nt))] * 2
        out_shape += [jax.ShapeDtypeStruct((rows // t_seq, nk, t_seq), F32)] * 2
    return pl.pallas_call(
        kern,
        grid=(rows // tm,),
        in_specs=[pl.BlockSpec((tm, d), lambda i: (i, 0)),
                  pl.BlockSpec((d, nq + 2 * nk), lambda i: (0, 0)),
                  pl.BlockSpec((tm, LANES), lambda i: (i % n_tab, 0)),
                  pl.BlockSpec((tm, LANES), lambda i: (i % n_tab, 0))],
        out_specs=out_specs,
        out_shape=out_shape,
        compiler_params=_cparams(("parallel",)),
        name="qkv_rope",
    )(x, w, cos, sin)


def _kmeans_kernel(k_ref, m_ref, *, nblk):
    kb = k_ref[...].reshape(nblk, MOBA_BLOCK, k_ref.shape[1])
    m_ref[...] = jnp.sum(kb, axis=1) * (1.0 / MOBA_BLOCK)


def _block_means(k, *, nblk_step):
    rows, w = k.shape
    total = rows // MOBA_BLOCK
    return pl.pallas_call(
        functools.partial(_kmeans_kernel, nblk=nblk_step),
        grid=(total // nblk_step,),
        in_specs=[pl.BlockSpec((nblk_step * MOBA_BLOCK, w), lambda i: (i, 0))],
        out_specs=pl.BlockSpec((nblk_step, w), lambda i: (i, 0)),
        out_shape=jax.ShapeDtypeStruct((total, w), F32),
        compiler_params=_cparams(("parallel",)),
        name="block_means",
    )(k)


def _top_select(gate, n_past, axis):
    n = gate.shape[axis]
    idx = lax.broadcasted_iota(jnp.int32, gate.shape, axis).astype(F32)
    gm = jnp.where(idx < n_past, gate, NEG_INF)
    sel = jnp.zeros(gate.shape, F32)
    for r in range(MOBA_TOPK):
        mx = jnp.max(gm, axis=axis, keepdims=True)
        first = jnp.min(jnp.where(gm == mx, idx, float(n)), axis=axis, keepdims=True)
        pick = idx == first
        keep = jnp.where(r < n_past, 1.0, 0.0).astype(F32)
        sel = jnp.maximum(sel, jnp.where(pick, keep, 0.0))
        gm = jnp.where(pick, NEG_INF, gm)
    return sel


def _moba_prompt_kernel(*refs, tq, ngrp, grp, hd, scale):
    npast = MOBA_PAST_PER_STEP
    qi_ref, kj_refs, last_ref = refs[0], refs[1:1 + npast], refs[1 + npast]
    q_ref = refs[2 + npast]
    k_refs = refs[3 + npast:3 + 2 * npast]
    vt_refs = refs[3 + 2 * npast:3 + 3 * npast]
    mean_ref, o_ref, qs_scr, sel_scr, m_scr, l_scr, acc_scr = refs[3 + 3 * npast:]
    p = pl.program_id(1)
    qi = qi_ref[p]
    kjs = [r[p] for r in kj_refs]
    rows = grp * tq

    @pl.when(kjs[0] == qi)
    def _():
        qt = q_ref[...].T
        for g in range(ngrp):
            qg = jnp.concatenate(
                [qt[(g * grp + hh) * hd:(g * grp + hh + 1) * hd, :] for hh in range(grp)], axis=1)
            qs_scr[g] = (qg * (scale * LOG2E)).astype(BF16)
            gate = _dot(mean_ref[:, g * hd:(g + 1) * hd], qg, HI)
            sel_scr[g] = _top_select(gate, qi, 0)
        m_scr[...] = jnp.full(m_scr.shape, NEG_INF, F32)
        l_scr[...] = jnp.zeros(l_scr.shape, F32)
        acc_scr[...] = jnp.zeros(acc_scr.shape, F32)

    ones_rows = jnp.ones((2 * SUBLANES, MOBA_BLOCK), BF16)

    def attend(k_ref, vt_ref, kj, diagonal):
        kgs = [k_ref[:, g * hd:(g + 1) * hd].astype(BF16) for g in range(ngrp)]
        vgs = [jnp.concatenate([vt_ref[g * hd:(g + 1) * hd, :].astype(BF16), ones_rows], axis=0)
               for g in range(ngrp)]
        sel_rows = None if diagonal else [sel_scr[g, pl.ds(kj, 1), :] for g in range(ngrp)]
        units = [(g, slice(c * UNIT_LANES, (c + 1) * UNIT_LANES))
                 for g in range(ngrp) for c in range(rows // UNIT_LANES)]
        def scores(g, ls):
            s = _dot(kgs[g], qs_scr[g, :, ls])
            if diagonal:
                kpos = lax.broadcasted_iota(jnp.int32, s.shape, 0)
                lane = lax.broadcasted_iota(jnp.int32, s.shape, 1)
                s = jnp.where(kpos <= jnp.bitwise_and(lane + ls.start, tq - 1), s, NEG_INF)
            return s

        def probs(g, ls, s):
            m_prev = m_scr[g, :, ls]
            m_cand = jnp.maximum(m_prev, jnp.max(s, axis=0, keepdims=True))
            if diagonal:
                m_new = m_ref = m_cand
            else:
                picked = sel_rows[g][:, ls] > 0.5
                m_new = jnp.where(picked, m_cand, m_prev)
                m_ref = jnp.where(picked, m_cand, POS_INF)
            return jnp.exp2(m_prev - m_new), m_new, jnp.exp2(s - m_ref).astype(BF16)

        def finish(g, ls, alpha, m_new, p):
            pv = _dot(vgs[g], p)
            acc_scr[g, :, ls] = alpha * acc_scr[g, :, ls] + pv[:hd, :]
            l_scr[g, :, ls] = alpha * l_scr[g, :, ls] + pv[hd:hd + 1, :]
            m_scr[g, :, ls] = m_new

        n = len(units)
        s_q, p_q = {}, {}
        for step in range(n + SKEW_S + SKEW_P):
            if step < n:
                s_q[step] = scores(*units[step])
            j = step - SKEW_S
            if 0 <= j < n:
                p_q[j] = probs(*units[j], s_q.pop(j))
            k = j - SKEW_P
            if 0 <= k < n:
                finish(*units[k], *p_q.pop(k))

    @pl.when(kjs[0] == qi)
    def _():
        attend(k_refs[0], vt_refs[0], kjs[0], True)

    for k_ref, vt_ref, kj in zip(k_refs, vt_refs, kjs):
        @pl.when(kj < qi)
        def _(k_ref=k_ref, vt_ref=vt_ref, kj=kj):
            attend(k_ref, vt_ref, kj, False)

    @pl.when(last_ref[p] == 1)
    def _():
        outs = []
        for g in range(ngrp):
            og = acc_scr[g] / l_scr[g]
            outs += [og[:, hh * tq:(hh + 1) * tq] for hh in range(grp)]
        o_ref[...] = jnp.concatenate(outs, axis=0).T


def _moba_prompt(q, k, vt, means, *, nb_batch, t_len, ngrp, grp, hd):
    tq = MOBA_BLOCK
    nq = t_len // tq
    nb = t_len // MOBA_BLOCK
    npast = MOBA_PAST_PER_STEP
    steps = []
    for i in range(nq):
        steps.append([i] + [i] * npast + [int(i == 0)])
        for j in range(0, i, npast):
            steps.append([i] + [min(j + s, i) for s in range(npast)] + [int(j + npast >= i)])
    tabs = [jnp.asarray(np.array([st[c] for st in steps], np.int32)) for c in range(npast + 2)]
    rows = grp * tq
    w = ngrp * grp * hd
    kvw = ngrp * hd
    kern = functools.partial(_moba_prompt_kernel, tq=tq, ngrp=ngrp, grp=grp, hd=hd,
                             scale=float(hd) ** -0.5)
    q_map = lambda b, p, *tb: (b * nq + tb[0][p], 0)
    k_spec = lambda s: pl.BlockSpec((MOBA_BLOCK, kvw), lambda b, p, *tb: (b * nb + tb[1 + s][p], 0))
    vt_spec = lambda s: pl.BlockSpec((None, kvw, MOBA_BLOCK), lambda b, p, *tb: (b, 0, tb[1 + s][p]))
    grid_spec = pltpu.PrefetchScalarGridSpec(
        num_scalar_prefetch=npast + 2,
        grid=(nb_batch, len(steps)),
        in_specs=[pl.BlockSpec((tq, w), q_map)]
        + [k_spec(s) for s in range(npast)] + [vt_spec(s) for s in range(npast)]
        + [pl.BlockSpec((nb, kvw), lambda b, p, *tb: (b, 0))],
        out_specs=pl.BlockSpec((tq, w), q_map),
        scratch_shapes=[pltpu.VMEM((ngrp, hd, rows), BF16),
                        pltpu.VMEM((ngrp, nb, rows), F32),
                        pltpu.VMEM((ngrp, 1, rows), F32),
                        pltpu.VMEM((ngrp, 1, rows), F32),
                        pltpu.VMEM((ngrp, hd, rows), F32)],
    )
    return pl.pallas_call(
        kern,
        grid_spec=grid_spec,
        out_shape=jax.ShapeDtypeStruct(q.shape, F32),
        compiler_params=_cparams(("parallel", "arbitrary")),
        name="moba_prompt",
    )(*tabs, q, *([k] * npast), *([vt] * npast), means)


def _sample_partial_kernel(pt_ref, q_ref, *refs, npg, ngrp, grp, hd, scale):
    k_pages = refs[:npg]
    v_pages = refs[npg:2 * npg]
    mean_ref, m_ref, l_ref, po_ref, qs_scr = refs[2 * npg:]
    step = pl.program_id(1)
    t_new = q_ref.shape[0]
    gr = grp * t_new

    @pl.when(step == 0)
    def _():
        for g in range(ngrp):
            qg = jnp.concatenate(
                [q_ref[:, (g * grp + hh) * hd:(g * grp + hh + 1) * hd] for hh in range(grp)], axis=0)
            qs_scr[g] = (qg * scale).astype(BF16)
        mean_ref[...] = jnp.zeros(mean_ref.shape, F32)
        m_ref[...] = jnp.zeros(m_ref.shape, F32)
        l_ref[...] = jnp.zeros(l_ref.shape, F32)

    ppb = MOBA_BLOCK // k_pages[0].shape[-1]
    bps = npg // ppb
    lane = lax.broadcasted_iota(jnp.int32, m_ref.shape, 1)
    lane_hd = lax.broadcasted_iota(jnp.int32, (hd, LANES), 1)
    blocks = [slice(blk * ppb, (blk + 1) * ppb) for blk in range(bps)]
    kts = [[jnp.concatenate([r[g] for r in k_pages[sl]], axis=1) for g in range(ngrp)]
           for sl in blocks]
    scs = [jnp.concatenate([_dot(qs_scr[g], kt[g].astype(BF16)) for g in range(ngrp)], axis=0)
           for kt in kts]
    ms = [jnp.max(sc, axis=1, keepdims=True) for sc in scs]
    prs = [jnp.exp(sc - m) for sc, m in zip(scs, ms)]
    ls = [jnp.sum(pr, axis=1, keepdims=True) for pr in prs]
    outs = []
    for sl, pr in zip(blocks, prs):
        prb = pr.astype(BF16)
        outs.append(jnp.concatenate(
            [_dot_nt(prb[g * gr:(g + 1) * gr, :],
                     jnp.concatenate([r[g] for r in v_pages[sl]], axis=1).astype(BF16))
             for g in range(ngrp)], axis=0))
    m_tile = m_ref[...]
    l_tile = l_ref[...]
    for blk in range(bps):
        bidx = step * bps + blk
        m_tile = jnp.where(lane == bidx, ms[blk], m_tile)
        l_tile = jnp.where(lane == bidx, ls[blk], l_tile)
        for g in range(ngrp):
            mean_col = jnp.sum(kts[blk][g], axis=1, keepdims=True) * (1.0 / MOBA_BLOCK)
            rs = slice(g * hd, (g + 1) * hd)
            mean_ref[rs, :] = jnp.where(lane_hd == bidx, mean_col, mean_ref[rs, :])
    m_ref[...] = m_tile
    l_ref[...] = l_tile
    for blk in range(bps):
        po_ref[blk] = outs[blk].T


def _sample_partials(q, cache_k, cache_v, page_table, *, npg, ngrp, grp, hd):
    db, n_pages = page_table.shape
    page = cache_k.shape[-1]
    kvw = ngrp * hd
    t_new = q.shape[0] // db
    ppb = MOBA_BLOCK // page
    bps = npg // ppb
    nsteps = n_pages // npg
    nblk = n_pages // ppb
    nrow = ngrp * grp * t_new
    assert nblk <= LANES
    kern = functools.partial(_sample_partial_kernel, npg=npg, ngrp=ngrp, grp=grp, hd=hd,
                             scale=float(hd) ** -0.5)

    def page_spec(i):
        return pl.BlockSpec((None, ngrp, hd, page),
                            lambda b, s, pt: (pt[b * n_pages + s * npg + i], 0, 0, 0))

    grid_spec = pltpu.PrefetchScalarGridSpec(
        num_scalar_prefetch=1,
        grid=(db, nsteps),
        in_specs=[pl.BlockSpec((t_new, q.shape[1]), lambda b, s, pt: (b, 0))]
        + [page_spec(i) for i in range(npg)] + [page_spec(i) for i in range(npg)],
        out_specs=[pl.BlockSpec((None, kvw, LANES), lambda b, s, pt: (b, 0, 0)),
                   pl.BlockSpec((None, nrow, LANES), lambda b, s, pt: (b, 0, 0)),
                   pl.BlockSpec((None, nrow, LANES), lambda b, s, pt: (b, 0, 0)),
                   pl.BlockSpec((None, bps, hd, nrow), lambda b, s, pt: (b, s, 0, 0))],
        scratch_shapes=[pltpu.VMEM((ngrp, grp * t_new, hd), BF16)],
    )
    return pl.pallas_call(
        kern,
        grid_spec=grid_spec,
        out_shape=[jax.ShapeDtypeStruct((db, kvw, LANES), F32),
                   jax.ShapeDtypeStruct((db, nrow, LANES), F32),
                   jax.ShapeDtypeStruct((db, nrow, LANES), F32),
                   jax.ShapeDtypeStruct((db, nblk, hd, nrow), F32)],
        compiler_params=_cparams(("parallel", "arbitrary")),
        name="moba_sample_partials",
    )(page_table.reshape(-1), q, *([cache_k] * npg), *([cache_v] * npg))


def _sample_combine_kernel(q_ref, kn_ref, vn_ref, mean_ref, m_ref, l_ref, po_ref, o_ref,
                           *, nblk, ngrp, grp, hd, scale):
    t_new = q_ref.shape[0]
    gr = grp * t_new
    nrow = ngrp * gr
    gates, qss = [], []
    for g in range(ngrp):
        qg = jnp.concatenate(
            [q_ref[:, (g * grp + hh) * hd:(g * grp + hh + 1) * hd] for hh in range(grp)], axis=0)
        qss.append((qg * scale).astype(BF16))
        gates.append(_dot(qg, mean_ref[g * hd:(g + 1) * hd, :], HI))
    sel = _top_select(jnp.concatenate(gates, axis=0), nblk, 1) > 0.5
    m_all = m_ref[...]
    m_past = jnp.max(jnp.where(sel, m_all, NEG_INF), axis=1, keepdims=True)
    kpos = lax.broadcasted_iota(jnp.int32, (nrow, t_new), 1)
    qpos = jnp.bitwise_and(lax.broadcasted_iota(jnp.int32, (nrow, t_new), 0), t_new - 1)
    kn = kn_ref[...].astype(BF16)
    vn = vn_ref[...].astype(BF16)
    s_own = jnp.concatenate(
        [_dot_nt(qss[g], kn[:, g * hd:(g + 1) * hd]) for g in range(ngrp)], axis=0)
    s_own = jnp.where(kpos <= qpos, s_own, NEG_INF)
    m_fin = jnp.maximum(m_past, jnp.max(s_own, axis=1, keepdims=True))
    w = jnp.where(sel, jnp.exp(m_all - m_fin), 0.0)
    p_own = jnp.exp(s_own - m_fin)
    l_fin = (jnp.sum(w * l_ref[...], axis=1, keepdims=True)
             + jnp.sum(p_own, axis=1, keepdims=True))
    pb = p_own.astype(BF16)
    o_own = jnp.concatenate(
        [_dot(pb[g * gr:(g + 1) * gr, :], vn[:, g * hd:(g + 1) * hd]) for g in range(ngrp)], axis=0)
    w_t = w.T
    acc_t = jnp.zeros((hd, nrow), F32)
    for j in range(nblk):
        acc_t = acc_t + w_t[j:j + 1, :] * po_ref[j]
    o = (acc_t.T + o_own) / l_fin
    o_ref[...] = jnp.concatenate(
        [o[r * t_new:(r + 1) * t_new, :] for r in range(ngrp * grp)], axis=1)


def _sample_combine(q, k_new, v_new, means_t, m_part, l_part, po, *, db, nblk, ngrp, grp, hd):
    t_new = q.shape[0] // db
    nrow = ngrp * grp * t_new
    kvw = ngrp * hd
    kern = functools.partial(_sample_combine_kernel, nblk=nblk, ngrp=ngrp, grp=grp, hd=hd,
                             scale=float(hd) ** -0.5)
    return pl.pallas_call(
        kern,
        grid=(db,),
        in_specs=[pl.BlockSpec((t_new, q.shape[1]), lambda b: (b, 0)),
                  pl.BlockSpec((t_new, kvw), lambda b: (b, 0)),
                  pl.BlockSpec((t_new, kvw), lambda b: (b, 0)),
                  pl.BlockSpec((None, kvw, LANES), lambda b: (b, 0, 0)),
                  pl.BlockSpec((None, nrow, LANES), lambda b: (b, 0, 0)),
                  pl.BlockSpec((None, nrow, LANES), lambda b: (b, 0, 0)),
                  pl.BlockSpec((None,) + po.shape[1:], lambda b: (b, 0, 0, 0))],
        out_specs=pl.BlockSpec((t_new, q.shape[1]), lambda b: (b, 0)),
        out_shape=jax.ShapeDtypeStruct(q.shape, F32),
        compiler_params=_cparams(("parallel",)),
        name="moba_sample_combine",
    )(q, k_new, v_new, means_t, m_part, l_part, po)


def _rope_tables(pos, hd, reps):
    half = hd // 2
    inv_freq = ROPE_THETA ** (-jnp.arange(half, dtype=F32) / half)
    ang = pos.astype(F32)[:, None] * inv_freq[None, :]
    cos = jnp.cos(ang)
    sin = jnp.sin(ang)
    per_vreg = LANES // hd
    cos_t = jnp.tile(jnp.concatenate([cos, cos], axis=1), (reps, per_vreg))
    sin_t = jnp.tile(jnp.concatenate([-sin, sin], axis=1), (reps, per_vreg))
    return cos_t, sin_t


def _trunk(x, pos, conv_state8, delta_state, attend, wts, *, nseq, t_len, tm, prep_tt,
           delta_ct, delta_c, dims, want_t=False):
    nh, dk, conv_dim, n_a, depth, alpha, bh, kvh, hd = dims
    rows = nseq * t_len
    new_conv, new_delta = [], []
    k_sh = v_sh = t_sh = None
    for layer in range(depth):
        if layer < n_a:
            cw = wts["a_conv"][layer].shape[0]
            prep_args = (conv_state8[layer], wts["a_conv"][layer], wts["a_log"][layer],
                         wts["a_dtb"][layer])
            prep_kw = dict(nseq=nseq, t_len=t_len, tt=prep_tt, conv_dim=conv_dim, nh=nh, dk=dk)
            if t_len >= tm:
                qkv, gb, z_src, tail = _gdn_in_fused(x, wts["a_w_in"][layer], *prep_args, **prep_kw)
                z_blk = 0
                new_conv.append(tail[:, SUBLANES - (cw - 1):, :])
            else:
                proj = _matmul(x, wts["a_w_in"][layer], rows, wts["a_w_in"][layer].shape[1] // 3,
                               "in_proj")
                qkv, gb = _gdn_prep(proj, *prep_args, **prep_kw)
                z_src, z_blk = proj, conv_dim // (nh * dk)
                new_conv.append(proj.reshape(nseq, t_len, -1)[:, t_len - (cw - 1):, :conv_dim])
            o, s_new = _delta_rule(qkv, gb, delta_state[layer], nseq=nseq, t_len=t_len,
                                   ct=delta_ct, c=delta_c, nh=nh, dk=dk)
            new_delta.append(s_new)
            mix, w_mix = o, wts["a_w_out"][layer]
            mix_kw = dict(z_src=z_src, z_blk=z_blk, norm_w=wts["a_norm"][layer], nh=nh, dv=dk)
        else:
            j = layer - n_a
            w_all = wts["b_w_qkv"][j]
            cos_t, sin_t = _rope_tables(pos, hd, (tm // t_len) if tm > t_len else 1)
            q, k_l, v_l, *t_l = _qkv_rope(x, w_all, cos_t, sin_t, tm=tm, nq=bh * hd, nk=kvh * hd,
                                          half=hd // 2, t_seq=t_len if want_t else None)
            if layer == n_a:
                k_sh, v_sh, t_sh = k_l, v_l, t_l
            mix, w_mix, mix_kw = attend(q, k_sh, v_sh, *t_sh), wts["b_w_o"][j], {}
        ln1 = (wts["ln_g"][layer][0], wts["ln_b"][layer][0])
        ln2 = (wts["ln_g"][layer][1], wts["ln_b"][layer][1])
        x = _mixer_ffn(mix, x, w_mix, ln1, wts["ffn_wg"][layer], wts["ffn_wu"][layer],
                       wts["ffn_wd"][layer], ln2, tm=tm, tf=256, alpha=alpha, **mix_kw)
    return x, jnp.stack(new_conv), jnp.stack(new_delta), k_sh, v_sh, t_sh


def kernel(x_prompt, x_sample, state_conv, state_delta, cache_k, cache_v, page_table, ln_g, ln_b,
           a_w_in, a_conv, a_log_decay, a_dt_bias, a_norm, a_w_out, kv_w_k, kv_w_v, b_w_q, b_w_o,
           ffn_w_gate, ffn_w_up, ffn_w_down):
    bp, tp, d = x_prompt.shape
    db, ts, _ = x_sample.shape
    depth = ln_g.shape[0]
    n_a = a_w_in.shape[0]
    nh = a_log_decay.shape[1]
    dk = state_delta.shape[-2]
    conv_dim = a_conv.shape[-1]
    cw = a_conv.shape[1]
    n_pool, page, kvh, hd = cache_k.shape
    bh = b_w_q.shape[-1] // hd
    grp = bh // kvh
    n_pages = page_table.shape[1]
    past = n_pages * page
    alpha = (2.0 * depth) ** 0.25
    assert past % MOBA_BLOCK == 0 and ts <= MOBA_BLOCK and ts >= cw - 1 and tp >= cw - 1
    assert state_delta.shape[-1] == dk and conv_dim == 3 * nh * dk

    in_w = a_w_in.shape[-1]
    in_pad = -(-in_w // (3 * LANES)) * (3 * LANES)
    pad_lanes = lambda v: jnp.pad(v, ((0, 0), (0, LANES - v.shape[-1])))[:, None, :]
    wts = {
        "a_w_in": [jnp.pad(a_w_in[l].astype(BF16), ((0, 0), (0, in_pad - in_w))) for l in range(n_a)],
        "a_conv": a_conv,
        "a_log": pad_lanes(a_log_decay),
        "a_dtb": pad_lanes(a_dt_bias),
        "a_norm": a_norm[:, None, :],
        "a_w_out": a_w_out.astype(BF16),
        "b_w_qkv": jnp.concatenate(
            [b_w_q, jnp.broadcast_to(kv_w_k, (b_w_q.shape[0],) + kv_w_k.shape),
             jnp.broadcast_to(kv_w_v, (b_w_q.shape[0],) + kv_w_v.shape)], axis=-1).astype(BF16),
        "b_w_o": b_w_o.astype(BF16),
        "ffn_wg": ffn_w_gate.astype(BF16),
        "ffn_wu": ffn_w_up.astype(BF16),
        "ffn_wd": ffn_w_down.astype(BF16),
        "ln_g": ln_g[:, :, None, :],
        "ln_b": ln_b[:, :, None, :],
    }
    dims = (nh, dk, conv_dim, n_a, depth, alpha, bh, kvh, hd)

    nb = tp // MOBA_BLOCK

    def attend_prompt(q, k, v, kt, vt):
        total = bp * nb
        step = SUBLANES if total % SUBLANES == 0 else total
        means = _block_means(k, nblk_step=step)
        return _moba_prompt(q, k, vt, means, nb_batch=bp, t_len=tp, ngrp=kvh, grp=grp, hd=hd)

    tm_p = 512 if (bp * tp) % 512 == 0 else 256
    y_p, p_conv, p_delta, _, _, (p_kt, p_vt) = _trunk(
        x_prompt.reshape(bp * tp, d), jnp.arange(tp, dtype=jnp.int32),
        jnp.zeros((n_a, bp, SUBLANES, conv_dim), F32), jnp.zeros((n_a, bp, nh, dk, dk), F32),
        attend_prompt, wts, nseq=bp, t_len=tp, tm=tm_p, prep_tt=tm_p,
        delta_ct=DELTA_CHUNK * DELTA_CHUNKS_PER_ITER,
        delta_c=DELTA_CHUNK, dims=dims, want_t=True)
    p_k = jnp.transpose(p_kt.reshape(bp, kvh, hd, tp), (0, 3, 1, 2))
    p_v = jnp.transpose(p_vt.reshape(bp, kvh, hd, tp), (0, 3, 1, 2))

    ck = jnp.transpose(cache_k, (0, 2, 3, 1))
    cv = jnp.transpose(cache_v, (0, 2, 3, 1))
    npg = next(n for n in (16, 8, 2 * (MOBA_BLOCK // page)) if n_pages % n == 0)

    def attend_sample(q, k_new, v_new):
        means_t, m_part, l_part, po = _sample_partials(q, ck, cv, page_table, npg=npg, ngrp=kvh,
                                                       grp=grp, hd=hd)
        return _sample_combine(q, k_new, v_new, means_t, m_part, l_part, po, db=db,
                               nblk=past // MOBA_BLOCK, ngrp=kvh, grp=grp, hd=hd)

    conv8 = jnp.pad(state_conv, ((0, 0), (0, 0), (SUBLANES - (cw - 1), 0), (0, 0)))
    y_s, s_conv, s_delta, s_k, s_v, _ = _trunk(
        x_sample.reshape(db * ts, d), past + jnp.arange(ts, dtype=jnp.int32), conv8, state_delta,
        attend_sample, wts, nseq=db, t_len=ts, tm=db * ts, prep_tt=ts, delta_ct=ts, delta_c=ts,
        dims=dims)

    return (y_p.reshape(bp, tp, d), y_s.reshape(db, ts, d), p_conv, p_delta, p_k, p_v,
            s_conv, s_delta, s_k.reshape(db, ts, kvh, hd), s_v.reshape(db, ts, kvh, hd))
```
